```python
import math
import jax, jax.numpy as jnp
from jax import lax
import numpy as np

D_MODEL = 1024
BATCH = 2
SEQ = 16384
DEPTH = 4

CHUNK = 64
Q_BLOCK = 128
A_WIDTH = D_MODEL // 2
A_HEADS = 4
A_DK = A_WIDTH // A_HEADS
A_DV = A_WIDTH // A_HEADS
B_WIDTH = D_MODEL - A_WIDTH
B_HEADS = 4
B_HEAD_DIM = B_WIDTH // (2 * B_HEADS)
EVEN_IN = 4 * A_WIDTH + 3 * B_WIDTH
C_HEADS = 16
C_HEAD_DIM = D_MODEL // C_HEADS
ODD_IN = 4 * D_MODEL + C_HEADS
D_FF = -(-8 * D_MODEL // (3 * 256)) * 256
N_EVEN = (DEPTH + 1) // 2
N_ODD = DEPTH // 2
ALPHA = (2 * DEPTH) ** 0.25
BETA = (8 * DEPTH) ** -0.25
MASK_VALUE = -1e30

kernel_name = "hybrid_hgrn2_diffattn_fox_deepnorm"

F32 = jnp.float32


def _layer_norm(x, g, b, eps=1e-5):
    xf = x.astype(F32)
    mu = jnp.mean(xf, axis=-1, keepdims=True)
    var = jnp.mean(jnp.square(xf - mu), axis=-1, keepdims=True)
    return ((xf - mu) * lax.rsqrt(var + eps) * g.astype(F32) + b.astype(F32)).astype(x.dtype)


def _rms_norm(x, g, eps=1e-6):
    xf = x.astype(F32)
    return xf * lax.rsqrt(jnp.mean(xf * xf, axis=-1, keepdims=True) + eps) * g.astype(F32)


def _swiglu(x, w1, w2):
    gate, up = jnp.split(x @ w1, 2, axis=-1)
    return (jax.nn.silu(gate) * up) @ w2


def _hgrn2(q_raw, f_raw, i, lb):
    Bsz, S = q_raw.shape[:2]
    nc = S // CHUNK
    lbh = lb.reshape(A_HEADS, A_DK).astype(F32)
    z = f_raw.astype(F32)
    q = jax.nn.silu(q_raw.astype(F32))
    f = lbh + (1.0 - lbh) * jax.nn.sigmoid(z)
    k = 1.0 - f
    logf = jnp.log(f)

    def to_chunks(t):
        return t.reshape(Bsz, nc, CHUNK, A_HEADS, -1).transpose(1, 0, 3, 2, 4)

    qc, kc, ic, lc = to_chunks(q), to_chunks(k), to_chunks(i.astype(F32)), to_chunks(logf)
    causal = jnp.tril(jnp.ones((CHUNK, CHUNK), bool))[:, :, None]

    def step(state, inp):
        qt, kt, it, lt = inp
        b = jnp.cumsum(lt, axis=2)
        o_inter = jnp.einsum('bhtk,bhkv->bhtv', qt * jnp.exp(b), state)
        rel = b[:, :, :, None, :] - b[:, :, None, :, :]
        decay = jnp.where(causal, jnp.exp(jnp.minimum(rel, 0.0)), 0.0)
        scores = jnp.einsum('bhtk,bhsk,bhtsk->bhts', qt, kt, decay)
        o_intra = jnp.einsum('bhts,bhsv->bhtv', scores, it)
        b_last = b[:, :, -1, :]
        state = jnp.exp(b_last)[..., None] * state + jnp.einsum(
            'bhsk,bhsv->bhkv', kt * jnp.exp(b_last[:, :, None, :] - b), it)
        return state, o_inter + o_intra

    s0 = jnp.zeros((Bsz, A_HEADS, A_DK, A_DV), F32)
    _, o = lax.scan(step, s0, (qc, kc, ic, lc))
    return o.transpose(1, 0, 3, 2, 4).reshape(Bsz, S, A_HEADS, A_DV)


def _diff_attention(q, k, v, lam):
    Bsz, H, _, S, d = q.shape
    scale = d ** -0.5
    k_chunk = jnp.arange(S) // CHUNK

    def block(bi):
        start = bi * Q_BLOCK
        qb = lax.dynamic_slice_in_dim(q, start, Q_BLOCK, axis=3)
        s = jnp.einsum('bhcqd,bhckd->bhcqk', qb, k) * scale
        q_chunk = (start + jnp.arange(Q_BLOCK)) // CHUNK
        mask = k_chunk[None, :] <= q_chunk[:, None]
        p = jax.nn.softmax(jnp.where(mask, s, MASK_VALUE), axis=-1)
        attn = p[:, :, 0] - lam * p[:, :, 1]
        return jnp.einsum('bhqk,bhkv->bhqv', attn, v)

    o = lax.map(block, jnp.arange(S // Q_BLOCK))
    return o.transpose(1, 0, 3, 2, 4).reshape(Bsz, S, H, -1)


def _fox_attention(q, k, v, c):
    Bsz, H, S, d = q.shape
    scale = d ** -0.5
    k_pos = jnp.arange(S)

    def block(bi):
        start = bi * Q_BLOCK
        qb = lax.dynamic_slice_in_dim(q, start, Q_BLOCK, axis=2)
        cb = lax.dynamic_slice_in_dim(c, start, Q_BLOCK, axis=2)
        s = jnp.einsum('bhqd,bhkd->bhqk', qb, k) * scale + (cb[..., :, None] - c[..., None, :])
        q_pos = start + jnp.arange(Q_BLOCK)
        mask = k_pos[None, :] <= q_pos[:, None]
        p = jax.nn.softmax(jnp.where(mask, s, MASK_VALUE), axis=-1)
        return jnp.einsum('bhqk,bhkd->bhqd', p, v)

    o = lax.map(block, jnp.arange(S // Q_BLOCK))
    return o.transpose(1, 0, 3, 2, 4).reshape(Bsz, S, H, d)


def _even_mixer(x, w_in, w_out, lb, lq1, lk1, lq2, lk2, hgrn_g, diff_g, layer_idx):
    Bsz, S, _ = x.shape
    proj = x @ w_in
    cuts = [A_WIDTH, 2 * A_WIDTH, 3 * A_WIDTH, 4 * A_WIDTH,
            4 * A_WIDTH + B_WIDTH, 4 * A_WIDTH + 2 * B_WIDTH]
    a_q, a_f, a_i, a_g, b_q, b_k, b_v = jnp.split(proj, cuts, axis=-1)
    heads_a = lambda t: t.reshape(Bsz, S, A_HEADS, -1)
    o_a = _hgrn2(heads_a(a_q), heads_a(a_f), heads_a(a_i), lb)
    o_a = _rms_norm(o_a, hgrn_g) * jax.nn.silu(heads_a(a_g).astype(F32))
    lam_init = 0.8 - 0.6 * math.exp(-0.3 * layer_idx)
    lam = (jnp.exp(jnp.sum(lq1.astype(F32) * lk1.astype(F32)))
           - jnp.exp(jnp.sum(lq2.astype(F32) * lk2.astype(F32))) + lam_init)
    qd = b_q.reshape(Bsz, S, B_HEADS, 2, B_HEAD_DIM).transpose(0, 2, 3, 1, 4).astype(F32)
    kd = b_k.reshape(Bsz, S, B_HEADS, 2, B_HEAD_DIM).transpose(0, 2, 3, 1, 4).astype(F32)
    vd = b_v.reshape(Bsz, S, B_HEADS, 2 * B_HEAD_DIM).transpose(0, 2, 1, 3).astype(F32)
    o_b = _diff_attention(qd, kd, vd, lam)
    o_b = _rms_norm(o_b, diff_g) * (1.0 - lam_init)
    o = jnp.concatenate([o_a.reshape(Bsz, S, A_WIDTH), o_b.reshape(Bsz, S, B_WIDTH)], axis=-1)
    return o.astype(x.dtype) @ w_out


def _odd_mixer(x, w_in, w_out, b_f, qn_g, kn_g):
    Bsz, S, _ = x.shape
    proj = x @ w_in
    q, k, v, g, f_logit = jnp.split(proj, [D_MODEL, 2 * D_MODEL, 3 * D_MODEL, 4 * D_MODEL], axis=-1)
    heads = lambda t: t.reshape(Bsz, S, C_HEADS, C_HEAD_DIM).transpose(0, 2, 1, 3)
    qh = _rms_norm(heads(q), qn_g)
    kh = _rms_norm(heads(k), kn_g)
    vh = heads(v).astype(F32)
    logf = jax.nn.log_sigmoid(f_logit.astype(F32) + b_f.astype(F32))
    c = jnp.cumsum(logf, axis=1).transpose(0, 2, 1)
    o = _fox_attention(qh, kh, vh, c).reshape(Bsz, S, D_MODEL)
    o = o * jax.nn.sigmoid(g.astype(F32))
    return o.astype(x.dtype) @ w_out


def setup_inputs(seed: int = 0) -> dict:
    key = jax.random.key(seed)
    ks = jax.random.split(key, 21)
    nrm = lambda k, shape, s: jax.random.normal(k, shape, F32) * s
    d_in = D_MODEL ** -0.5
    return {
        "x": nrm(ks[0], (BATCH, SEQ, D_MODEL), 1.0),
        "even_w_in": nrm(ks[1], (N_EVEN, D_MODEL, EVEN_IN), d_in),
        "even_w_out": nrm(ks[2], (N_EVEN, D_MODEL, D_MODEL), d_in * BETA),
        "hgrn_lb_logits": nrm(ks[3], (N_EVEN, A_WIDTH), 0.1),
        "diff_lq1": nrm(ks[4], (N_EVEN, B_HEAD_DIM), 0.1),
        "diff_lk1": nrm(ks[5], (N_EVEN, B_HEAD_DIM), 0.1),
        "diff_lq2": nrm(ks[6], (N_EVEN, B_HEAD_DIM), 0.1),
        "diff_lk2": nrm(ks[7], (N_EVEN, B_HEAD_DIM), 0.1),
        "hgrn_norm_g": 1.0 + nrm(ks[8], (N_EVEN, A_DV), 0.02),
        "diff_norm_g": 1.0 + nrm(ks[9], (N_EVEN, 2 * B_HEAD_DIM), 0.02),
        "fox_w_in": nrm(ks[10], (N_ODD, D_MODEL, ODD_IN), d_in),
        "fox_w_out": nrm(ks[11], (N_ODD, D_MODEL, D_MODEL), d_in * BETA),
        "fox_b_f": 2.0 + nrm(ks[12], (N_ODD, C_HEADS), 0.1),
        "fox_qnorm_g": 1.0 + nrm(ks[13], (N_ODD, C_HEAD_DIM), 0.02),
        "fox_knorm_g": 1.0 + nrm(ks[14], (N_ODD, C_HEAD_DIM), 0.02),
        "ffn_w1": nrm(ks[15], (DEPTH, D_MODEL, 2 * D_FF), d_in),
        "ffn_w2": nrm(ks[16], (DEPTH, D_FF, D_MODEL), D_FF ** -0.5 * BETA),
        "ln1_g": 1.0 + nrm(ks[17], (DEPTH, D_MODEL), 0.02),
        "ln1_b": nrm(ks[18], (DEPTH, D_MODEL), 0.02),
        "ln2_g": 1.0 + nrm(ks[19], (DEPTH, D_MODEL), 0.02),
        "ln2_b": nrm(ks[20], (DEPTH, D_MODEL), 0.02),
    }


def reference(x, even_w_in, even_w_out, hgrn_lb_logits, diff_lq1, diff_lk1, diff_lq2, diff_lk2,
              hgrn_norm_g, diff_norm_g, fox_w_in, fox_w_out, fox_b_f, fox_qnorm_g, fox_knorm_g,
              ffn_w1, ffn_w2, ln1_g, ln1_b, ln2_g, ln2_b):
    lb_soft = jax.nn.softmax(hgrn_lb_logits.astype(F32), axis=0)
    lb_all = jnp.cumsum(lb_soft, axis=0) - lb_soft[0]
    for l in range(DEPTH):
        j = l // 2
        if l % 2 == 0:
            h = _even_mixer(x, even_w_in[j], even_w_out[j], lb_all[j],
                            diff_lq1[j], diff_lk1[j], diff_lq2[j], diff_lk2[j],
                            hgrn_norm_g[j], diff_norm_g[j], l)
        else:
            h = _odd_mixer(x, fox_w_in[j], fox_w_out[j], fox_b_f[j], fox_qnorm_g[j], fox_knorm_g[j])
        x = _layer_norm(ALPHA * x + h, ln1_g[l], ln1_b[l])
        x = _layer_norm(ALPHA * x + _swiglu(x, ffn_w1[l], ffn_w2[l]), ln2_g[l], ln2_b[l])
    return x
```

```python
import functools
import math

import numpy as np
import jax
import jax.numpy as jnp
from jax import lax
from jax.experimental import pallas as pl
from jax.experimental.pallas import tpu as pltpu

F32 = jnp.float32
BF16 = jnp.bfloat16

D_MODEL = 1024
DEPTH = 4
A_WIDTH = 512
A_HEADS = 4
A_DK = 128
B_WIDTH = 512
B_HEADS = 4
B_HEAD_DIM = 64
DIFF_CHUNK = 64
C_HEADS = 16
C_HEAD_DIM = 64
D_FF = 2816
ALPHA = (2 * DEPTH) ** 0.25
MASK_VALUE = -1e30

LANES = 128
MXU_DIM = 256

PROJ_TM = 512
FFN_TM = 512
ATT_T = 512
HGRN_L = 256
VMEM_LIMIT = 56 * 1024 * 1024


def _cparams(sem):
    return pltpu.CompilerParams(dimension_semantics=sem, vmem_limit_bytes=VMEM_LIMIT)


def _full_spec(shape):
    nd = len(shape)
    return pl.BlockSpec(shape, lambda *_: (0,) * nd)


def _split3(x):
    h = x.astype(BF16)
    r = x - h.astype(F32)
    m = r.astype(BF16)
    l = (r - m.astype(F32)).astype(BF16)
    return jnp.concatenate([h, m, l], axis=1)


def _sum3(y):
    return y[:, :LANES] + y[:, LANES:2 * LANES] + y[:, 2 * LANES:]


def _layer_norm_rows(y, g, b):
    mu = jnp.mean(y, axis=-1, keepdims=True)
    d = y - mu
    var = jnp.mean(d * d, axis=-1, keepdims=True)
    return d * lax.rsqrt(var + 1e-5) * g + b


def _even_proj_kernel(x_ref, w_ref, aq, af, ai, ag, bq, bk, bv):
    xb = x_ref[...].astype(BF16)
    outs = (aq, af, ai, ag, bq, bk, bv)
    for n, o in enumerate(outs):
        acc = jnp.dot(xb, w_ref[:, n * 512:(n + 1) * 512], preferred_element_type=F32)
        if o is bq:
            acc = acc * (B_HEAD_DIM ** -0.5)
        o[...] = acc.astype(o.dtype)


def _even_proj(x2, w_bf):
    T = x2.shape[0]
    tm = PROJ_TM
    tok = lambda dt: jax.ShapeDtypeStruct((T, 512), dt)
    spec = pl.BlockSpec((tm, 512), lambda i: (i, 0))
    return pl.pallas_call(
        _even_proj_kernel,
        grid=(T // tm,),
        in_specs=[pl.BlockSpec((tm, D_MODEL), lambda i: (i, 0)), _full_spec(w_bf.shape)],
        out_specs=[spec] * 7,
        out_shape=[tok(BF16), tok(F32), tok(BF16), tok(BF16), tok(BF16), tok(BF16), tok(BF16)],
        compiler_params=_cparams(("parallel",)),
        name="even_proj",
    )(x2, w_bf)


def _hgrn_cumsum_matrix(L):
    idx = np.arange(L)
    mats = []
    c = 1
    while c < L:
        start = (idx // c) * c
        end = start + c - 1
        right = ((idx // c) % 2) == 1
        u = idx[None, :]
        m_right = (u >= start[:, None]) & (u <= idx[:, None])
        m_left = (u > idx[:, None]) & (u <= end[:, None])
        mats.append(np.where(right[:, None], m_right, m_left))
        c *= 2
    u = idx[None, :]
    mats.append(u <= idx[:, None])
    mats.append(u > idx[:, None])
    return np.concatenate(mats, axis=0).astype(np.float32)


def _hgrn_kernel(q_ref, f_ref, i_ref, g_ref, lbl_ref, gn_ref, cmat_ref, o_ref, state_ref, *, layer_j):
    L = HGRN_L
    n_levels = int(math.log2(L))

    @pl.when(pl.program_id(2) == 0)
    def _():
        state_ref[...] = jnp.zeros_like(state_ref)

    lbl = lbl_ref[...]
    e = jnp.exp(lbl - jnp.max(lbl, axis=0, keepdims=True))
    soft = e / jnp.sum(e, axis=0, keepdims=True)
    lb = jnp.sum(soft[:layer_j + 1], axis=0, keepdims=True) - soft[0:1]

    z = f_ref[...]
    f = lb + (1.0 - lb) * jax.nn.sigmoid(z)
    logf = jnp.log(f)
    k = 1.0 - f
    qr = q_ref[...].astype(F32)
    q = qr * jax.nn.sigmoid(qr)
    iv = i_ref[...]

    logf3 = _split3(logf)
    row = lax.broadcasted_iota(jnp.int32, (L, 1), 0)
    xor = lax.broadcasted_iota(jnp.int32, (L, L), 0) ^ lax.broadcasted_iota(jnp.int32, (L, L), 1)

    nt = (((1,), (1,)), ((), ()))
    qb = q.astype(BF16)
    kb = k.astype(BF16)
    scores = jnp.where(xor == 0, lax.dot_general(qb, kb, nt, preferred_element_type=F32), 0.0)
    for lvl in range(n_levels):
        c = 1 << lvl
        ex = _sum3(jnp.dot(cmat_ref[lvl * L:(lvl + 1) * L, :], logf3, preferred_element_type=F32))
        gdec = jnp.exp(ex)
        right = ((row >> lvl) & 1) == 1
        ql = jnp.where(right, q * gdec, 0.0).astype(BF16)
        kl = jnp.where(right, 0.0, k * gdec).astype(BF16)
        s_l = lax.dot_general(ql, kl, nt, preferred_element_type=F32)
        scores = scores + jnp.where(xor < 2 * c, s_l, 0.0)

    b_full = _sum3(jnp.dot(cmat_ref[n_levels * L:(n_levels + 1) * L, :], logf3,
                           preferred_element_type=F32))
    b_rest = _sum3(jnp.dot(cmat_ref[(n_levels + 1) * L:(n_levels + 2) * L, :], logf3,
                           preferred_element_type=F32))
    state_t = state_ref[...]
    q_in = (q * jnp.exp(b_full)).astype(BF16)
    o = lax.dot_general(q_in, state_t.astype(BF16), nt, preferred_element_type=F32)
    o = o + jnp.dot(scores.astype(BF16), iv, preferred_element_type=F32)

    k_out = (k * jnp.exp(b_rest)).astype(BF16)
    upd = jnp.dot(iv.astype(F32).T.astype(BF16), k_out, preferred_element_type=F32)
    state_ref[...] = state_t * jnp.exp(b_full[L - 1:L, :]) + upd

    ms = jnp.mean(o * o, axis=-1, keepdims=True)
    gate = g_ref[...].astype(F32)
    o = o * lax.rsqrt(ms + 1e-6) * gn_ref[...] * (gate * jax.nn.sigmoid(gate))
    o_ref[...] = o.astype(o_ref.dtype)


def _hgrn(aq, af, ai, ag, lb_logits, gn, cmat, layer_j, B, S):
    L = HGRN_L
    nb = S // L
    blk = pl.BlockSpec((L, LANES), lambda b, h, t: (b * nb + t, h))
    n_even = lb_logits.shape[0]
    return pl.pallas_call(
        functools.partial(_hgrn_kernel, layer_j=layer_j),
        grid=(B, A_HEADS, nb),
        in_specs=[blk, blk, blk, blk,
                  pl.BlockSpec((n_even, LANES), lambda b, h, t: (0, h)),
                  _full_spec(gn.shape), _full_spec(cmat.shape)],
        out_specs=blk,
        out_shape=jax.ShapeDtypeStruct((B * S, A_WIDTH), BF16),
        scratch_shapes=[pltpu.VMEM((LANES, LANES), F32)],
        compiler_params=_cparams(("parallel", "parallel", "arbitrary")),
        name="hgrn2",
    )(aq, af, ai, ag, lb_logits, gn, cmat)


def _attend(qz, k, v, m_ref, l_ref, acc_ref, *, mask, key_bias, row_shift):
    s = lax.dot_general(qz, k, (((1,), (1,)), ((), ())), preferred_element_type=F32)
    if key_bias is not None:
        s = s - key_bias
    if mask is not None:
        s = jnp.where(mask, s, MASK_VALUE)
    m_prev = m_ref[...]
    row_max = jnp.max(s, axis=-1, keepdims=True)
    if row_shift is not None:
        row_max = row_max + row_shift
    m_new = jnp.maximum(m_prev, row_max)
    shift = m_new if row_shift is None else m_new - row_shift
    p = jnp.exp(s - shift)
    alpha = jnp.exp(m_prev - m_new)
    l_ref[...] = alpha * l_ref[...] + jnp.sum(p, axis=-1, keepdims=True)
    acc_ref[...] = alpha * acc_ref[...] + jnp.dot(p.astype(BF16), v, preferred_element_type=F32)
    m_ref[...] = m_new


def _tri_schedule(n):
    qi = [i for i in range(n) for _ in range(i + 1)]
    ki = [j for i in range(n) for j in range(i + 1)]
    return jnp.asarray(qi, jnp.int32), jnp.asarray(ki, jnp.int32)


def _init_stats(m_ref, l_ref, acc_ref):
    m_ref[...] = jnp.full_like(m_ref, MASK_VALUE)
    l_ref[...] = jnp.zeros_like(l_ref)
    acc_ref[...] = jnp.zeros_like(acc_ref)


def _diff_kernel(qi_ref, ki_ref, q_ref, k_ref, v_ref, lam_ref, gn_ref, o_ref,
                 m1, l1, a1, m2, l2, a2, *, lam_init):
    step = pl.program_id(2)
    qi = qi_ref[step]
    ki = ki_ref[step]

    @pl.when(ki == 0)
    def _():
        _init_stats(m1, l1, a1)
        _init_stats(m2, l2, a2)

    q = q_ref[...]
    lane = lax.broadcasted_iota(jnp.int32, q.shape, 1)
    zero = jnp.zeros_like(q)
    q1 = jnp.where(lane < B_HEAD_DIM, q, zero)
    q2 = jnp.where(lane >= B_HEAD_DIM, q, zero)
    k = k_ref[...]
    v = v_ref[...]

    def run(mask):
        _attend(q1, k, v, m1, l1, a1, mask=mask, key_bias=None, row_shift=None)
        _attend(q2, k, v, m2, l2, a2, mask=mask, key_bias=None, row_shift=None)

    @pl.when(ki < qi)
    def _():
        run(None)

    @pl.when(ki == qi)
    def _():
        t = q.shape[0]
        r = lax.broadcasted_iota(jnp.int32, (t, t), 0) >> 6
        c = lax.broadcasted_iota(jnp.int32, (t, t), 1) >> 6
        run(c <= r)
        lv = lam_ref[...]
        lam = (jnp.exp(jnp.sum(lv[0:1] * lv[1:2], axis=-1, keepdims=True))
               - jnp.exp(jnp.sum(lv[2:3] * lv[3:4], axis=-1, keepdims=True)) + lam_init)
        o = a1[...] / l1[...] - lam * (a2[...] / l2[...])
        ms = jnp.mean(o * o, axis=-1, keepdims=True)
        o = o * lax.rsqrt(ms + 1e-6) * gn_ref[...] * (1.0 - lam_init)
        o_ref[...] = o.astype(o_ref.dtype)


def _diff_attention(bq, bk, bv, lamv, gn, lam_init, B, S):
    t = ATT_T
    nq = S // t
    qi, ki = _tri_schedule(nq)
    grid_spec = pltpu.PrefetchScalarGridSpec(
        num_scalar_prefetch=2,
        grid=(B, B_HEADS, int(qi.shape[0])),
        in_specs=[
            pl.BlockSpec((t, LANES), lambda b, h, s, qi, ki: (b * nq + qi[s], h)),
            pl.BlockSpec((t, LANES), lambda b, h, s, qi, ki: (b * nq + ki[s], h)),
            pl.BlockSpec((t, LANES), lambda b, h, s, qi, ki: (b * nq + ki[s], h)),
            pl.BlockSpec(lamv.shape, lambda b, h, s, qi, ki: (0, 0)),
            pl.BlockSpec(gn.shape, lambda b, h, s, qi, ki: (0, 0)),
        ],
        out_specs=pl.BlockSpec((t, LANES), lambda b, h, s, qi, ki: (b * nq + qi[s], h)),
        scratch_shapes=[pltpu.VMEM((t, 1), F32), pltpu.VMEM((t, 1), F32), pltpu.VMEM((t, LANES), F32),
                        pltpu.VMEM((t, 1), F32), pltpu.VMEM((t, 1), F32), pltpu.VMEM((t, LANES), F32)],
    )
    return pl.pallas_call(
        functools.partial(_diff_kernel, lam_init=lam_init),
        grid_spec=grid_spec,
        out_shape=jax.ShapeDtypeStruct((B * S, B_WIDTH), BF16),
        compiler_params=_cparams(("parallel", "parallel", "arbitrary")),
        name="diff_attn",
    )(qi, ki, bq, bk, bv, lamv, gn)


def _odd_proj_kernel(x_ref, w_ref, bf_ref, qg_ref, kg_ref, gmat_ref, tri_ref,
                     q_o, k_o, v_o, g_o, c_o, ct_o, carry_ref):
    @pl.when(pl.program_id(1) == 0)
    def _():
        carry_ref[...] = jnp.zeros_like(carry_ref)

    xb = x_ref[...].astype(BF16)
    ch = MXU_DIM
    gmat = gmat_ref[...]

    def headnorm(acc, g):
        ms = jnp.dot((acc * acc).astype(BF16), gmat, preferred_element_type=F32)
        return acc * lax.rsqrt(ms + 1e-6) * g

    for j in range(D_MODEL // ch):
        sl = slice(j * ch, (j + 1) * ch)
        acc = jnp.dot(xb, w_ref[:, j * ch:(j + 1) * ch], preferred_element_type=F32)
        q_o[:, sl] = (headnorm(acc, qg_ref[...]) * (C_HEAD_DIM ** -0.5)).astype(q_o.dtype)
        acc = jnp.dot(xb, w_ref[:, D_MODEL + j * ch:D_MODEL + (j + 1) * ch], preferred_element_type=F32)
        k_o[:, sl] = headnorm(acc, kg_ref[...]).astype(k_o.dtype)
        acc = jnp.dot(xb, w_ref[:, 2 * D_MODEL + j * ch:2 * D_MODEL + (j + 1) * ch],
                      preferred_element_type=F32)
        v_o[:, sl] = acc.astype(v_o.dtype)
        acc = jnp.dot(xb, w_ref[:, 3 * D_MODEL + j * ch:3 * D_MODEL + (j + 1) * ch],
                      preferred_element_type=F32)
        g_o[:, sl] = acc.astype(g_o.dtype)

    fl = jnp.dot(xb, w_ref[:, 4 * D_MODEL:4 * D_MODEL + LANES], preferred_element_type=F32) + bf_ref[...]
    logf = jnp.minimum(fl, 0.0) - jnp.log(1.0 + jnp.exp(-jnp.abs(fl)))
    c = _sum3(jnp.dot(tri_ref[...], _split3(logf), preferred_element_type=F32)) + carry_ref[...]
    tm = c.shape[0]
    carry_ref[...] = c[tm - 1:tm, :]
    c_o[...] = c
    ct_o[...] = c.T[:C_HEADS, :]


def _odd_proj(x2, w_bf, bf_pad, qg, kg, gmat, tri, B, S):
    T = B * S
    tm = PROJ_TM
    nt = S // tm
    tok = pl.BlockSpec((tm, D_MODEL), lambda b, t: (b * nt + t, 0))
    out_tok = jax.ShapeDtypeStruct((T, D_MODEL), BF16)
    return pl.pallas_call(
        _odd_proj_kernel,
        grid=(B, nt),
        in_specs=[tok, _full_spec(w_bf.shape), _full_spec(bf_pad.shape), _full_spec(qg.shape),
                  _full_spec(kg.shape), _full_spec(gmat.shape), _full_spec(tri.shape)],
        out_specs=[tok, tok, tok, tok,
                   pl.BlockSpec((tm, LANES), lambda b, t: (b * nt + t, 0)),
                   pl.BlockSpec((None, C_HEADS, tm), lambda b, t: (b, 0, t))],
        out_shape=[out_tok, out_tok, out_tok, out_tok,
                   jax.ShapeDtypeStruct((T, LANES), F32),
                   jax.ShapeDtypeStruct((B, C_HEADS, S), F32)],
        scratch_shapes=[pltpu.VMEM((1, LANES), F32)],
        compiler_params=_cparams(("parallel", "arbitrary")),
        name="odd_proj",
    )(x2, w_bf, bf_pad, qg, kg, gmat, tri)


def _fox_kernel(qi_ref, ki_ref, q_ref, k_ref, v_ref, ctm_ref, chm_ref, o_ref,
                ma, la, aa, mb, lb, ab):
    pair = pl.program_id(1)
    step = pl.program_id(2)
    qi = qi_ref[step]
    ki = ki_ref[step]

    @pl.when(ki == 0)
    def _():
        _init_stats(ma, la, aa)
        _init_stats(mb, lb, ab)

    q = q_ref[...]
    lane = lax.broadcasted_iota(jnp.int32, q.shape, 1)
    zero = jnp.zeros_like(q)
    q_a = jnp.where(lane < C_HEAD_DIM, q, zero)
    q_b = jnp.where(lane >= C_HEAD_DIM, q, zero)
    k = k_ref[...]
    v = v_ref[...]
    c_keys = chm_ref[...]
    c_rows = ctm_ref[...]
    ct_a = jnp.sum(jnp.where(lane == 2 * pair, c_rows, 0.0), axis=-1, keepdims=True)
    ct_b = jnp.sum(jnp.where(lane == 2 * pair + 1, c_rows, 0.0), axis=-1, keepdims=True)

    def run(mask):
        _attend(q_a, k, v, ma, la, aa, mask=mask, key_bias=c_keys[0:1, :], row_shift=ct_a)
        _attend(q_b, k, v, mb, lb, ab, mask=mask, key_bias=c_keys[1:2, :], row_shift=ct_b)

    @pl.when(ki < qi)
    def _():
        run(None)

    @pl.when(ki == qi)
    def _():
        t = q.shape[0]
        r = lax.broadcasted_iota(jnp.int32, (t, t), 0)
        c = lax.broadcasted_iota(jnp.int32, (t, t), 1)
        run(c <= r)
        o = jnp.where(lane < C_HEAD_DIM, aa[...] / la[...], ab[...] / lb[...])
        o_ref[...] = o.astype(o_ref.dtype)


def _fox_attention(q, k, v, c_tm, c_hm, B, S):
    t = ATT_T
    nq = S // t
    qi, ki = _tri_schedule(nq)
    n_pairs = C_HEADS // 2
    c_hm4 = c_hm.reshape(B, n_pairs, 2, S)
    grid_spec = pltpu.PrefetchScalarGridSpec(
        num_scalar_prefetch=2,
        grid=(B, n_pairs, int(qi.shape[0])),
        in_specs=[
            pl.BlockSpec((t, LANES), lambda b, h, s, qi, ki: (b * nq + qi[s], h)),
            pl.BlockSpec((t, LANES), lambda b, h, s, qi, ki: (b * nq + ki[s], h)),
            pl.BlockSpec((t, LANES), lambda b, h, s, qi, ki: (b * nq + ki[s], h)),
            pl.BlockSpec((t, LANES), lambda b, h, s, qi, ki: (b * nq + qi[s], 0)),
            pl.BlockSpec((None, None, 2, t), lambda b, h, s, qi, ki: (b, h, 0, ki[s])),
        ],
        out_specs=pl.BlockSpec((t, LANES), lambda b, h, s, qi, ki: (b * nq + qi[s], h)),
        scratch_shapes=[pltpu.VMEM((t, 1), F32), pltpu.VMEM((t, 1), F32), pltpu.VMEM((t, LANES), F32),
                        pltpu.VMEM((t, 1), F32), pltpu.VMEM((t, 1), F32), pltpu.VMEM((t, LANES), F32)],
    )
    return pl.pallas_call(
        _fox_kernel,
        grid_spec=grid_spec,
        out_shape=jax.ShapeDtypeStruct((B * S, D_MODEL), BF16),
        compiler_params=_cparams(("parallel", "parallel", "arbitrary")),
        name="fox_attn",
    )(qi, ki, q, k, v, c_tm, c_hm4)


def _even_out_kernel(oa_ref, ob_ref, x_ref, w_ref, g_ref, b_ref, y_ref):
    h = jnp.dot(oa_ref[...], w_ref[:A_WIDTH, :], preferred_element_type=F32)
    h = h + jnp.dot(ob_ref[...], w_ref[A_WIDTH:, :], preferred_element_type=F32)
    y_ref[...] = _layer_norm_rows(ALPHA * x_ref[...] + h, g_ref[...], b_ref[...])


def _odd_out_kernel(o_ref, gate_ref, x_ref, w_ref, g_ref, b_ref, y_ref):
    o = o_ref[...].astype(F32) * jax.nn.sigmoid(gate_ref[...].astype(F32))
    h = jnp.dot(o.astype(BF16), w_ref[...], preferred_element_type=F32)
    y_ref[...] = _layer_norm_rows(ALPHA * x_ref[...] + h, g_ref[...], b_ref[...])


def _out_proj(kernel, acts, x2, w_bf, g, b, name):
    T = x2.shape[0]
    tm = PROJ_TM
    tok = lambda n: pl.BlockSpec((tm, n), lambda i: (i, 0))
    return pl.pallas_call(
        kernel,
        grid=(T // tm,),
        in_specs=[tok(a.shape[1]) for a in acts] + [tok(D_MODEL), _full_spec(w_bf.shape),
                                                    _full_spec(g.shape), _full_spec(b.shape)],
        out_specs=tok(D_MODEL),
        out_shape=jax.ShapeDtypeStruct((T, D_MODEL), F32),
        compiler_params=_cparams(("parallel",)),
        name=name,
    )(*acts, x2, w_bf, g, b)


def _ffn_kernel(x_ref, w1_ref, w2_ref, g_ref, b_ref, y_ref, act_ref):
    x = x_ref[...]
    xb = x.astype(BF16)
    ch = MXU_DIM
    for j in range(D_FF // ch):
        gate = jnp.dot(xb, w1_ref[:, j * ch:(j + 1) * ch], preferred_element_type=F32)
        up = jnp.dot(xb, w1_ref[:, D_FF + j * ch:D_FF + (j + 1) * ch], preferred_element_type=F32)
        act_ref[:, j * ch:(j + 1) * ch] = (gate * jax.nn.sigmoid(gate) * up).astype(BF16)
    h = jnp.dot(act_ref[...], w2_ref[...], preferred_element_type=F32)
    y_ref[...] = _layer_norm_rows(ALPHA * x + h, g_ref[...], b_ref[...])


def _ffn(x2, w1_bf, w2_bf, g, b):
    T = x2.shape[0]
    tm = FFN_TM
    tok = pl.BlockSpec((tm, D_MODEL), lambda i: (i, 0))
    return pl.pallas_call(
        _ffn_kernel,
        grid=(T // tm,),
        in_specs=[tok, _full_spec(w1_bf.shape), _full_spec(w2_bf.shape),
                  _full_spec(g.shape), _full_spec(b.shape)],
        out_specs=tok,
        out_shape=jax.ShapeDtypeStruct((T, D_MODEL), F32),
        scratch_shapes=[pltpu.VMEM((tm, D_FF), BF16)],
        compiler_params=_cparams(("parallel",)),
        name="ffn",
    )(x2, w1_bf, w2_bf, g, b)


def kernel(x, even_w_in, even_w_out, hgrn_lb_logits, diff_lq1, diff_lk1, diff_lq2, diff_lk2,
           hgrn_norm_g, diff_norm_g, fox_w_in, fox_w_out, fox_b_f, fox_qnorm_g, fox_knorm_g,
           ffn_w1, ffn_w2, ln1_g, ln1_b, ln2_g, ln2_b):
    B, S, D = x.shape
    assert D == D_MODEL and S % ATT_T == 0 and S % PROJ_TM == 0 and S % HGRN_L == 0
    T = B * S
    x2 = x.reshape(T, D).astype(F32)

    cmat = jnp.asarray(_hgrn_cumsum_matrix(HGRN_L), BF16)
    tri = jnp.asarray(np.tril(np.ones((PROJ_TM, PROJ_TM), np.float32)), BF16)
    head_of = np.arange(MXU_DIM) // C_HEAD_DIM
    gmat = jnp.asarray((head_of[:, None] == head_of[None, :]).astype(np.float32) / C_HEAD_DIM, BF16)
    row = lambda v: v.astype(F32).reshape(1, -1)

    for l in range(DEPTH):
        j = l // 2
        if l % 2 == 0:
            aq, af, ai, ag, bq, bk, bv = _even_proj(x2, even_w_in[j].astype(BF16))
            o_a = _hgrn(aq, af, ai, ag, hgrn_lb_logits.astype(F32), row(hgrn_norm_g[j]), cmat, j, B, S)
            lamv = jnp.zeros((8, B_HEAD_DIM), F32).at[0:4].set(
                jnp.stack([diff_lq1[j], diff_lk1[j], diff_lq2[j], diff_lk2[j]]).astype(F32))
            lam_init = 0.8 - 0.6 * math.exp(-0.3 * l)
            o_b = _diff_attention(bq, bk, bv, lamv, row(diff_norm_g[j]), lam_init, B, S)
            x2 = _out_proj(_even_out_kernel, [o_a, o_b], x2, even_w_out[j].astype(BF16),
                           row(ln1_g[l]), row(ln1_b[l]), "even_out")
        else:
            w_pad = jnp.pad(fox_w_in[j], ((0, 0), (0, LANES - C_HEADS))).astype(BF16)
            bf_pad = jnp.pad(fox_b_f[j].astype(F32), (0, LANES - C_HEADS)).reshape(1, LANES)
            tile4 = lambda v: jnp.tile(v.astype(F32), MXU_DIM // C_HEAD_DIM).reshape(1, MXU_DIM)
            q, k, v, g, c_tm, c_hm = _odd_proj(x2, w_pad, bf_pad, tile4(fox_qnorm_g[j]),
                                               tile4(fox_knorm_g[j]), gmat, tri, B, S)
            o = _fox_attention(q, k, v, c_tm, c_hm, B, S)
            x2 = _out_proj(_odd_out_kernel, [o, g], x2, fox_w_out[j].astype(BF16),
                           row(ln1_g[l]), row(ln1_b[l]), "odd_out")
        x2 = _ffn(x2, ffn_w1[l].astype(BF16), ffn_w2[l].astype(BF16), row(ln2_g[l]), row(ln2_b[l]))
    return x2.reshape(B, S, D).astype(x.dtype)
```

```python
import functools
import math

import numpy as np
import jax
import jax.numpy as jnp
from jax import lax
from jax.experimental import pallas as pl
from jax.experimental.pallas import tpu as pltpu

F32 = jnp.float32
BF16 = jnp.bfloat16

D_MODEL = 1024
DEPTH = 4
A_WIDTH = 512
A_HEADS = 4
A_DK = 128
B_WIDTH = 512
B_HEADS = 4
B_HEAD_DIM = 64
DIFF_CHUNK = 64
C_HEADS = 16
C_HEAD_DIM = 64
D_FF = 2816
ALPHA = (2 * DEPTH) ** 0.25
MASK_VALUE = -1e30

LANES = 128
MXU_DIM = 256

PROJ_TM = 512
FFN_TM = 512
ATT_T = 1024
ATT_ROWS = 128
HGRN_L = 256
LOG2E = math.log2(math.e)
VMEM_LIMIT = 56 * 1024 * 1024


def _cparams(sem):
    return pltpu.CompilerParams(dimension_semantics=sem, vmem_limit_bytes=VMEM_LIMIT)


def _full_spec(shape):
    nd = len(shape)
    return pl.BlockSpec(shape, lambda *_: (0,) * nd)


def _split3(x):
    h = x.astype(BF16)
    r = x - h.astype(F32)
    m = r.astype(BF16)
    l = (r - m.astype(F32)).astype(BF16)
    return jnp.concatenate([h, m, l], axis=1)


def _sum3(y):
    return y[:, :LANES] + y[:, LANES:2 * LANES] + y[:, 2 * LANES:]


def _layer_norm_rows(y, g, b):
    mu = jnp.mean(y, axis=-1, keepdims=True)
    d = y - mu
    var = jnp.mean(d * d, axis=-1, keepdims=True)
    return d * lax.rsqrt(var + 1e-5) * g + b


def _even_proj_kernel(x_ref, w_ref, aq, af, ai, ag, bq, bk, bv):
    xb = x_ref[...].astype(BF16)
    outs = (aq, af, ai, ag, bq, bk, bv)
    for n, o in enumerate(outs):
        acc = jnp.dot(xb, w_ref[:, n * 512:(n + 1) * 512], preferred_element_type=F32)
        if o is bq:
            acc = (acc * (B_HEAD_DIM ** -0.5 * LOG2E)).astype(o.dtype)
            lane = lax.broadcasted_iota(jnp.int32, (acc.shape[0], LANES), 1)
            zero = jnp.zeros((acc.shape[0], LANES), o.dtype)
            for h in range(B_HEADS):
                qh = acc[:, h * LANES:(h + 1) * LANES]
                o[:, 2 * h * LANES:(2 * h + 1) * LANES] = jnp.where(lane < B_HEAD_DIM, qh, zero)
                o[:, (2 * h + 1) * LANES:(2 * h + 2) * LANES] = jnp.where(lane >= B_HEAD_DIM, qh, zero)
        else:
            o[...] = acc.astype(o.dtype)


def _even_proj(x2, w_bf):
    T = x2.shape[0]
    tm = PROJ_TM
    tok = lambda dt, n=512: jax.ShapeDtypeStruct((T, n), dt)
    spec = lambda n=512: pl.BlockSpec((tm, n), lambda i: (i, 0))
    return pl.pallas_call(
        _even_proj_kernel,
        grid=(T // tm,),
        in_specs=[pl.BlockSpec((tm, D_MODEL), lambda i: (i, 0)), _full_spec(w_bf.shape)],
        out_specs=[spec(), spec(), spec(), spec(), spec(2 * B_WIDTH), spec(), spec()],
        out_shape=[tok(BF16), tok(F32), tok(BF16), tok(BF16), tok(BF16, 2 * B_WIDTH), tok(BF16), tok(BF16)],
        compiler_params=_cparams(("parallel",)),
        name="even_proj",
    )(x2, w_bf)


def _hgrn_cumsum_matrix(L):
    idx = np.arange(L)
    mats = []
    c = 1
    while c < L:
        start = (idx // c) * c
        end = start + c - 1
        right = ((idx // c) % 2) == 1
        u = idx[None, :]
        m_right = (u >= start[:, None]) & (u <= idx[:, None])
        m_left = (u > idx[:, None]) & (u <= end[:, None])
        mats.append(np.where(right[:, None], m_right, m_left))
        c *= 2
    u = idx[None, :]
    mats.append(u <= idx[:, None])
    mats.append(u > idx[:, None])
    return np.concatenate(mats, axis=0).astype(np.float32)


def _hgrn_kernel(q_ref, f_ref, i_ref, g_ref, lbl_ref, gn_ref, cmat_ref, o_ref, state_ref, *, layer_j):
    L = HGRN_L
    n_levels = int(math.log2(L))

    @pl.when(pl.program_id(2) == 0)
    def _():
        state_ref[...] = jnp.zeros_like(state_ref)

    lbl = lbl_ref[...]
    e = jnp.exp(lbl - jnp.max(lbl, axis=0, keepdims=True))
    soft = e / jnp.sum(e, axis=0, keepdims=True)
    lb = jnp.sum(soft[:layer_j + 1], axis=0, keepdims=True) - soft[0:1]

    z = f_ref[...]
    f = lb + (1.0 - lb) * jax.nn.sigmoid(z)
    logf = jnp.log(f)
    k = 1.0 - f
    qr = q_ref[...].astype(F32)
    q = qr * jax.nn.sigmoid(qr)
    iv = i_ref[...]

    logf3 = _split3(logf)
    row = lax.broadcasted_iota(jnp.int32, (L, 1), 0)
    xor = lax.broadcasted_iota(jnp.int32, (L, L), 0) ^ lax.broadcasted_iota(jnp.int32, (L, L), 1)

    nt = (((1,), (1,)), ((), ()))
    qb = q.astype(BF16)
    kb = k.astype(BF16)
    scores = jnp.where(xor == 0, lax.dot_general(qb, kb, nt, preferred_element_type=F32), 0.0)
    for lvl in range(n_levels):
        c = 1 << lvl
        ex = _sum3(jnp.dot(cmat_ref[lvl * L:(lvl + 1) * L, :], logf3, preferred_element_type=F32))
        gdec = jnp.exp(ex)
        right = ((row >> lvl) & 1) == 1
        ql = jnp.where(right, q * gdec, 0.0).astype(BF16)
        kl = jnp.where(right, 0.0, k * gdec).astype(BF16)
        s_l = lax.dot_general(ql, kl, nt, preferred_element_type=F32)
        scores = scores + jnp.where(xor < 2 * c, s_l, 0.0)

    b_full = _sum3(jnp.dot(cmat_ref[n_levels * L:(n_levels + 1) * L, :], logf3,
                           preferred_element_type=F32))
    b_rest = _sum3(jnp.dot(cmat_ref[(n_levels + 1) * L:(n_levels + 2) * L, :], logf3,
                           preferred_element_type=F32))
    state_t = state_ref[...]
    q_in = (q * jnp.exp(b_full)).astype(BF16)
    o = lax.dot_general(q_in, state_t.astype(BF16), nt, preferred_element_type=F32)
    o = o + jnp.dot(scores.astype(BF16), iv, preferred_element_type=F32)

    k_out = (k * jnp.exp(b_rest)).astype(BF16)
    upd = jnp.dot(iv.astype(F32).T.astype(BF16), k_out, preferred_element_type=F32)
    state_ref[...] = state_t * jnp.exp(b_full[L - 1:L, :]) + upd

    ms = jnp.mean(o * o, axis=-1, keepdims=True)
    gate = g_ref[...].astype(F32)
    o = o * lax.rsqrt(ms + 1e-6) * gn_ref[...] * (gate * jax.nn.sigmoid(gate))
    o_ref[...] = o.astype(o_ref.dtype)


def _hgrn(aq, af, ai, ag, lb_logits, gn, cmat, layer_j, B, S):
    L = HGRN_L
    nb = S // L
    blk = pl.BlockSpec((L, LANES), lambda b, h, t: (b * nb + t, h))
    n_even = lb_logits.shape[0]
    return pl.pallas_call(
        functools.partial(_hgrn_kernel, layer_j=layer_j),
        grid=(B, A_HEADS, nb),
        in_specs=[blk, blk, blk, blk,
                  pl.BlockSpec((n_even, LANES), lambda b, h, t: (0, h)),
                  _full_spec(gn.shape), _full_spec(cmat.shape)],
        out_specs=blk,
        out_shape=jax.ShapeDtypeStruct((B * S, A_WIDTH), BF16),
        scratch_shapes=[pltpu.VMEM((LANES, LANES), F32)],
        compiler_params=_cparams(("parallel", "parallel", "arbitrary")),
        name="hgrn2",
    )(aq, af, ai, ag, lb_logits, gn, cmat)


def _attend(q_ref, k, v, streams, mask_fn):
    t = q_ref.shape[0]
    reps = k.shape[0] // LANES
    chains = [(st, r0) for st in streams for r0 in range(0, t, ATT_ROWS)]

    def scores(st, r0):
        col0, _, _, _, key_bias, _ = st
        s = lax.dot_general(q_ref[r0:r0 + ATT_ROWS, col0:col0 + LANES], k, (((1,), (1,)), ((), ())),
                            preferred_element_type=F32)
        if key_bias is not None:
            s = s - key_bias
        if mask_fn is not None:
            s = jnp.where(mask_fn(r0), s, MASK_VALUE)
        return s

    def softmax(st, r0, s):
        _, m_ref, l_ref, _, _, shift_ref = st
        rows = slice(r0, r0 + ATT_ROWS)
        m_prev = m_ref[rows, :]
        row_max = jnp.max(s, axis=-1, keepdims=True)
        if shift_ref is not None:
            row_shift = shift_ref[rows, :]
            m_new = jnp.maximum(m_prev, row_max + row_shift)
            sub = m_new - row_shift
        else:
            m_new = jnp.maximum(m_prev, row_max)
            sub = m_new
        p = jnp.exp2(s - jnp.concatenate([sub] * reps, axis=1))
        alpha = jnp.exp2(m_prev - m_new)
        l_ref[rows, :] = alpha * l_ref[rows, :] + jnp.sum(p, axis=-1, keepdims=True)
        m_ref[rows, :] = m_new
        return p.astype(BF16), alpha

    def weighted_values(st, r0, p, alpha):
        acc_ref = st[3]
        rows = slice(r0, r0 + ATT_ROWS)
        acc_ref[rows, :] = alpha * acc_ref[rows, :] + jnp.dot(p, v, preferred_element_type=F32)

    n = len(chains)
    s_next = scores(*chains[0])
    pending = None
    for i, chain in enumerate(chains):
        s = s_next
        if i + 1 < n:
            s_next = scores(*chains[i + 1])
        pa = softmax(*chain, s)
        if pending is not None:
            weighted_values(*pending)
        pending = (*chain, *pa)
    weighted_values(*pending)


def _tri_schedule(n):
    qi = [i for i in range(n) for _ in range(i + 1)]
    ki = [j for i in range(n) for j in range(i + 1)]
    return jnp.asarray(qi, jnp.int32), jnp.asarray(ki, jnp.int32)


def _init_stats(m_ref, l_ref, acc_ref):
    m_ref[...] = jnp.full_like(m_ref, MASK_VALUE)
    l_ref[...] = jnp.zeros_like(l_ref)
    acc_ref[...] = jnp.zeros_like(acc_ref)


def _diff_kernel(qi_ref, ki_ref, q_ref, k_ref, v_ref, lam_ref, gn_ref, o_ref,
                 m1, l1, a1, m2, l2, a2, *, lam_init):
    step = pl.program_id(2)
    qi = qi_ref[step]
    ki = ki_ref[step]

    @pl.when(ki == 0)
    def _():
        _init_stats(m1, l1, a1)
        _init_stats(m2, l2, a2)

    k = k_ref[...]
    v = v_ref[...]
    t = k.shape[0]

    def run(mask_fn):
        _attend(q_ref, k, v, [(0, m1, l1, a1, None, None), (LANES, m2, l2, a2, None, None)], mask_fn)

    @pl.when(ki < qi)
    def _():
        run(None)

    @pl.when(ki == qi)
    def _():
        def chunk_mask(r0):
            sh = DIFF_CHUNK.bit_length() - 1
            r = (lax.broadcasted_iota(jnp.int32, (ATT_ROWS, t), 0) + r0) >> sh
            c = lax.broadcasted_iota(jnp.int32, (ATT_ROWS, t), 1) >> sh
            return c <= r
        run(chunk_mask)
        lv = lam_ref[...]
        lam = (jnp.exp(jnp.sum(lv[0:1] * lv[1:2], axis=-1, keepdims=True))
               - jnp.exp(jnp.sum(lv[2:3] * lv[3:4], axis=-1, keepdims=True)) + lam_init)
        o = a1[...] / l1[...] - lam * (a2[...] / l2[...])
        ms = jnp.mean(o * o, axis=-1, keepdims=True)
        o = o * lax.rsqrt(ms + 1e-6) * gn_ref[...] * (1.0 - lam_init)
        o_ref[...] = o.astype(o_ref.dtype)


def _diff_attention(bq, bk, bv, lamv, gn, lam_init, B, S):
    t = ATT_T
    nq = S // t
    qi, ki = _tri_schedule(nq)
    stat = pltpu.VMEM((t, LANES), F32)
    grid_spec = pltpu.PrefetchScalarGridSpec(
        num_scalar_prefetch=2,
        grid=(B, B_HEADS, int(qi.shape[0])),
        in_specs=[
            pl.BlockSpec((t, 2 * LANES), lambda b, h, s, qi, ki: (b * nq + qi[s], h)),
            pl.BlockSpec((t, LANES), lambda b, h, s, qi, ki: (b * nq + ki[s], h)),
            pl.BlockSpec((t, LANES), lambda b, h, s, qi, ki: (b * nq + ki[s], h)),
            pl.BlockSpec(lamv.shape, lambda b, h, s, qi, ki: (0, 0)),
            pl.BlockSpec(gn.shape, lambda b, h, s, qi, ki: (0, 0)),
        ],
        out_specs=pl.BlockSpec((t, LANES), lambda b, h, s, qi, ki: (b * nq + qi[s], h)),
        scratch_shapes=[stat] * 6,
    )
    return pl.pallas_call(
        functools.partial(_diff_kernel, lam_init=lam_init),
        grid_spec=grid_spec,
        out_shape=jax.ShapeDtypeStruct((B * S, B_WIDTH), BF16),
        compiler_params=_cparams(("parallel", "parallel", "arbitrary")),
        name="diff_attn",
    )(qi, ki, bq, bk, bv, lamv, gn)


def _odd_proj_kernel(x_ref, w_ref, bf_ref, qg_ref, kg_ref, gmat_ref, tri_ref,
                     q_o, k_o, v_o, g_o, c_o, ct_o, carry_ref):
    @pl.when(pl.program_id(1) == 0)
    def _():
        carry_ref[...] = jnp.zeros_like(carry_ref)

    xb = x_ref[...].astype(BF16)
    ch = MXU_DIM
    gmat = gmat_ref[...]

    def headnorm(acc, g):
        ms = jnp.dot((acc * acc).astype(BF16), gmat, preferred_element_type=F32)
        return acc * lax.rsqrt(ms + 1e-6) * g

    lane = lax.broadcasted_iota(jnp.int32, (xb.shape[0], LANES), 1)
    zero = jnp.zeros((xb.shape[0], LANES), q_o.dtype)
    for j in range(D_MODEL // ch):
        sl = slice(j * ch, (j + 1) * ch)
        acc = jnp.dot(xb, w_ref[:, j * ch:(j + 1) * ch], preferred_element_type=F32)
        qn = (headnorm(acc, qg_ref[...]) * (C_HEAD_DIM ** -0.5 * LOG2E)).astype(q_o.dtype)
        for h in range(ch // LANES):
            pair = qn[:, h * LANES:(h + 1) * LANES]
            col = 2 * (j * ch + h * LANES)
            q_o[:, col:col + LANES] = jnp.where(lane < C_HEAD_DIM, pair, zero)
            q_o[:, col + LANES:col + 2 * LANES] = jnp.where(lane >= C_HEAD_DIM, pair, zero)
        acc = jnp.dot(xb, w_ref[:, D_MODEL + j * ch:D_MODEL + (j + 1) * ch], preferred_element_type=F32)
        k_o[:, sl] = headnorm(acc, kg_ref[...]).astype(k_o.dtype)
        acc = jnp.dot(xb, w_ref[:, 2 * D_MODEL + j * ch:2 * D_MODEL + (j + 1) * ch],
                      preferred_element_type=F32)
        v_o[:, sl] = acc.astype(v_o.dtype)
        acc = jnp.dot(xb, w_ref[:, 3 * D_MODEL + j * ch:3 * D_MODEL + (j + 1) * ch],
                      preferred_element_type=F32)
        g_o[:, sl] = acc.astype(g_o.dtype)

    fl = jnp.dot(xb, w_ref[:, 4 * D_MODEL:4 * D_MODEL + LANES], preferred_element_type=F32) + bf_ref[...]
    logf = jnp.minimum(fl, 0.0) - jnp.log(1.0 + jnp.exp(-jnp.abs(fl)))
    c = _sum3(jnp.dot(tri_ref[...], _split3(logf), preferred_element_type=F32)) + carry_ref[...]
    tm = c.shape[0]
    carry_ref[...] = c[tm - 1:tm, :]
    c2 = c * LOG2E
    c_o[...] = c2
    ct_o[...] = c2.T[:C_HEADS, :]


def _odd_proj(x2, w_bf, bf_pad, qg, kg, gmat, tri, B, S):
    T = B * S
    tm = PROJ_TM
    nt = S // tm
    tok = pl.BlockSpec((tm, D_MODEL), lambda b, t: (b * nt + t, 0))
    out_tok = jax.ShapeDtypeStruct((T, D_MODEL), BF16)
    return pl.pallas_call(
        _odd_proj_kernel,
        grid=(B, nt),
        in_specs=[tok, _full_spec(w_bf.shape), _full_spec(bf_pad.shape), _full_spec(qg.shape),
                  _full_spec(kg.shape), _full_spec(gmat.shape), _full_spec(tri.shape)],
        out_specs=[pl.BlockSpec((tm, 2 * D_MODEL), lambda b, t: (b * nt + t, 0)), tok, tok, tok,
                   pl.BlockSpec((tm, LANES), lambda b, t: (b * nt + t, 0)),
                   pl.BlockSpec((None, C_HEADS, tm), lambda b, t: (b, 0, t))],
        out_shape=[jax.ShapeDtypeStruct((T, 2 * D_MODEL), BF16), out_tok, out_tok, out_tok,
                   jax.ShapeDtypeStruct((T, LANES), F32),
                   jax.ShapeDtypeStruct((B, C_HEADS, S), F32)],
        scratch_shapes=[pltpu.VMEM((1, LANES), F32)],
        compiler_params=_cparams(("parallel", "arbitrary")),
        name="odd_proj",
    )(x2, w_bf, bf_pad, qg, kg, gmat, tri)


def _fox_kernel(qi_ref, ki_ref, q_ref, k_ref, v_ref, ctm_ref, chm_ref, o_ref,
                ma, la, aa, mb, lb, ab, cta, ctb):
    pair = pl.program_id(1)
    step = pl.program_id(2)
    qi = qi_ref[step]
    ki = ki_ref[step]
    t = k_ref.shape[0]
    lane = lax.broadcasted_iota(jnp.int32, (t, LANES), 1)

    @pl.when(ki == 0)
    def _():
        _init_stats(ma, la, aa)
        _init_stats(mb, lb, ab)
        c_rows = ctm_ref[...]
        ct_a = jnp.sum(jnp.where(lane == 2 * pair, c_rows, 0.0), axis=-1, keepdims=True)
        ct_b = jnp.sum(jnp.where(lane == 2 * pair + 1, c_rows, 0.0), axis=-1, keepdims=True)
        cta[...] = jnp.broadcast_to(ct_a, cta.shape)
        ctb[...] = jnp.broadcast_to(ct_b, ctb.shape)

    k = k_ref[...]
    v = v_ref[...]
    c_keys = chm_ref[...]

    def run(mask_fn):
        _attend(q_ref, k, v, [(0, ma, la, aa, c_keys[0:1, :], cta),
                              (LANES, mb, lb, ab, c_keys[1:2, :], ctb)], mask_fn)

    @pl.when(ki < qi)
    def _():
        run(None)

    @pl.when(ki == qi)
    def _():
        def chunk_mask(r0):
            r = lax.broadcasted_iota(jnp.int32, (ATT_ROWS, t), 0) + r0
            c = lax.broadcasted_iota(jnp.int32, (ATT_ROWS, t), 1)
            return c <= r
        run(chunk_mask)
        o = jnp.where(lane < C_HEAD_DIM, aa[...] / la[...], ab[...] / lb[...])
        o_ref[...] = o.astype(o_ref.dtype)


def _fox_attention(q, k, v, c_tm, c_hm, B, S):
    t = ATT_T
    nq = S // t
    qi, ki = _tri_schedule(nq)
    n_pairs = C_HEADS // 2
    c_hm4 = c_hm.reshape(B, n_pairs, 2, S)
    stat = pltpu.VMEM((t, LANES), F32)
    grid_spec = pltpu.PrefetchScalarGridSpec(
        num_scalar_prefetch=2,
        grid=(B, n_pairs, int(qi.shape[0])),
        in_specs=[
            pl.BlockSpec((t, 2 * LANES), lambda b, h, s, qi, ki: (b * nq + qi[s], h)),
            pl.BlockSpec((t, LANES), lambda b, h, s, qi, ki: (b * nq + ki[s], h)),
            pl.BlockSpec((t, LANES), lambda b, h, s, qi, ki: (b * nq + ki[s], h)),
            pl.BlockSpec((t, LANES), lambda b, h, s, qi, ki: (b * nq + qi[s], 0)),
            pl.BlockSpec((None, None, 2, t), lambda b, h, s, qi, ki: (b, h, 0, ki[s])),
        ],
        out_specs=pl.BlockSpec((t, LANES), lambda b, h, s, qi, ki: (b * nq + qi[s], h)),
        scratch_shapes=[stat] * 8,
    )
    return pl.pallas_call(
        _fox_kernel,
        grid_spec=grid_spec,
        out_shape=jax.ShapeDtypeStruct((B * S, D_MODEL), BF16),
        compiler_params=_cparams(("parallel", "parallel", "arbitrary")),
        name="fox_attn",
    )(qi, ki, q, k, v, c_tm, c_hm4)


def _even_out_kernel(oa_ref, ob_ref, x_ref, w_ref, g_ref, b_ref, y_ref):
    h = jnp.dot(oa_ref[...], w_ref[:A_WIDTH, :], preferred_element_type=F32)
    h = h + jnp.dot(ob_ref[...], w_ref[A_WIDTH:, :], preferred_element_type=F32)
    y_ref[...] = _layer_norm_rows(ALPHA * x_ref[...] + h, g_ref[...], b_ref[...])


def _odd_out_kernel(o_ref, gate_ref, x_ref, w_ref, g_ref, b_ref, y_ref):
    o = o_ref[...].astype(F32) * jax.nn.sigmoid(gate_ref[...].astype(F32))
    h = jnp.dot(o.astype(BF16), w_ref[...], preferred_element_type=F32)
    y_ref[...] = _layer_norm_rows(ALPHA * x_ref[...] + h, g_ref[...], b_ref[...])


def _out_proj(kernel, acts, x2, w_bf, g, b, name):
    T = x2.shape[0]
    tm = PROJ_TM
    tok = lambda n: pl.BlockSpec((tm, n), lambda i: (i, 0))
    return pl.pallas_call(
        kernel,
        grid=(T // tm,),
        in_specs=[tok(a.shape[1]) for a in acts] + [tok(D_MODEL), _full_spec(w_bf.shape),
                                                    _full_spec(g.shape), _full_spec(b.shape)],
        out_specs=tok(D_MODEL),
        out_shape=jax.ShapeDtypeStruct((T, D_MODEL), F32),
        compiler_params=_cparams(("parallel",)),
        name=name,
    )(*acts, x2, w_bf, g, b)


def _ffn_kernel(x_ref, w1_ref, w2_ref, g_ref, b_ref, y_ref, act_ref):
    x = x_ref[...]
    xb = x.astype(BF16)
    ch = MXU_DIM
    for j in range(D_FF // ch):
        gate = jnp.dot(xb, w1_ref[:, j * ch:(j + 1) * ch], preferred_element_type=F32)
        up = jnp.dot(xb, w1_ref[:, D_FF + j * ch:D_FF + (j + 1) * ch], preferred_element_type=F32)
        act_ref[:, j * ch:(j + 1) * ch] = (gate * jax.nn.sigmoid(gate) * up).astype(BF16)
    h = jnp.dot(act_ref[...], w2_ref[...], preferred_element_type=F32)
    y_ref[...] = _layer_norm_rows(ALPHA * x + h, g_ref[...], b_ref[...])


def _ffn(x2, w1_bf, w2_bf, g, b):
    T = x2.shape[0]
    tm = FFN_TM
    tok = pl.BlockSpec((tm, D_MODEL), lambda i: (i, 0))
    return pl.pallas_call(
        _ffn_kernel,
        grid=(T // tm,),
        in_specs=[tok, _full_spec(w1_bf.shape), _full_spec(w2_bf.shape),
                  _full_spec(g.shape), _full_spec(b.shape)],
        out_specs=tok,
        out_shape=jax.ShapeDtypeStruct((T, D_MODEL), F32),
        scratch_shapes=[pltpu.VMEM((tm, D_FF), BF16)],
        compiler_params=_cparams(("parallel",)),
        name="ffn",
    )(x2, w1_bf, w2_bf, g, b)


def kernel(x, even_w_in, even_w_out, hgrn_lb_logits, diff_lq1, diff_lk1, diff_lq2, diff_lk2,
           hgrn_norm_g, diff_norm_g, fox_w_in, fox_w_out, fox_b_f, fox_qnorm_g, fox_knorm_g,
           ffn_w1, ffn_w2, ln1_g, ln1_b, ln2_g, ln2_b):
    B, S, D = x.shape
    assert D == D_MODEL and S % ATT_T == 0 and S % PROJ_TM == 0 and S % HGRN_L == 0
    T = B * S
    x2 = x.reshape(T, D).astype(F32)

    cmat = jnp.asarray(_hgrn_cumsum_matrix(HGRN_L), BF16)
    tri = jnp.asarray(np.tril(np.ones((PROJ_TM, PROJ_TM), np.float32)), BF16)
    head_of = np.arange(MXU_DIM) // C_HEAD_DIM
    gmat = jnp.asarray((head_of[:, None] == head_of[None, :]).astype(np.float32) / C_HEAD_DIM, BF16)
    row = lambda v: v.astype(F32).reshape(1, -1)

    for l in range(DEPTH):
        j = l // 2
        if l % 2 == 0:
            aq, af, ai, ag, bq, bk, bv = _even_proj(x2, even_w_in[j].astype(BF16))
            o_a = _hgrn(aq, af, ai, ag, hgrn_lb_logits.astype(F32), row(hgrn_norm_g[j]), cmat, j, B, S)
            lamv = jnp.zeros((8, B_HEAD_DIM), F32).at[0:4].set(
                jnp.stack([diff_lq1[j], diff_lk1[j], diff_lq2[j], diff_lk2[j]]).astype(F32))
            lam_init = 0.8 - 0.6 * math.exp(-0.3 * l)
            o_b = _diff_attention(bq, bk, bv, lamv, row(diff_norm_g[j]), lam_init, B, S)
            x2 = _out_proj(_even_out_kernel, [o_a, o_b], x2, even_w_out[j].astype(BF16),
                           row(ln1_g[l]), row(ln1_b[l]), "even_out")
        else:
            w_pad = jnp.pad(fox_w_in[j], ((0, 0), (0, LANES - C_HEADS))).astype(BF16)
            bf_pad = jnp.pad(fox_b_f[j].astype(F32), (0, LANES - C_HEADS)).reshape(1, LANES)
            tile4 = lambda v: jnp.tile(v.astype(F32), MXU_DIM // C_HEAD_DIM).reshape(1, MXU_DIM)
            q, k, v, g, c_tm, c_hm = _odd_proj(x2, w_pad, bf_pad, tile4(fox_qnorm_g[j]),
                                               tile4(fox_knorm_g[j]), gmat, tri, B, S)
            o = _fox_attention(q, k, v, c_tm, c_hm, B, S)
            x2 = _out_proj(_odd_out_kernel, [o, g], x2, fox_w_out[j].astype(BF16),
                           row(ln1_g[l]), row(ln1_b[l]), "odd_out")
        x2 = _ffn(x2, ffn_w1[l].astype(BF16), ffn_w2[l].astype(BF16), row(ln2_g[l]), row(ln2_b[l]))
    return x2.reshape(B, S, D).astype(x.dtype)
```

```python
import functools
import math

import numpy as np
import jax
import jax.numpy as jnp
from jax import lax
from jax.experimental import pallas as pl
from jax.experimental.pallas import tpu as pltpu

F32 = jnp.float32
BF16 = jnp.bfloat16

D_MODEL = 1024
DEPTH = 4
A_WIDTH = 512
A_HEADS = 4
A_DK = 128
B_WIDTH = 512
B_HEADS = 4
B_HEAD_DIM = 64
DIFF_CHUNK = 64
C_HEADS = 16
C_HEAD_DIM = 64
D_FF = 2816
ALPHA = (2 * DEPTH) ** 0.25
MASK_VALUE = -1e30

LANES = 128
MXU_DIM = 256

PROJ_TM = 512
FFN_TM = 512
ATT_T = 1024
ATT_QC = 256
ATT_KC = 256
ATT_AHEAD = 3
ATT_BEHIND = 2
VT_ROWS = LANES + 16
HGRN_L = 256
LOG2E = math.log2(math.e)
VMEM_LIMIT = 56 * 1024 * 1024


def _cparams(sem):
    return pltpu.CompilerParams(dimension_semantics=sem, vmem_limit_bytes=VMEM_LIMIT)


def _full_spec(shape):
    nd = len(shape)
    return pl.BlockSpec(shape, lambda *_: (0,) * nd)


def _split3(x):
    h = x.astype(BF16)
    r = x - h.astype(F32)
    m = r.astype(BF16)
    l = (r - m.astype(F32)).astype(BF16)
    return jnp.concatenate([h, m, l], axis=1)


def _sum3(y):
    return y[:, :LANES] + y[:, LANES:2 * LANES] + y[:, 2 * LANES:]


def _layer_norm_rows(y, g, b):
    mu = jnp.mean(y, axis=-1, keepdims=True)
    d = y - mu
    var = jnp.mean(d * d, axis=-1, keepdims=True)
    return d * lax.rsqrt(var + 1e-5) * g + b


def _even_proj_kernel(x_ref, w_ref, aq, af, ai, ag, bq, bk, bv):
    xb = x_ref[...].astype(BF16)
    outs = (aq, af, ai, ag, bq, bk, bv)
    for n, o in enumerate(outs):
        acc = jnp.dot(xb, w_ref[:, n * 512:(n + 1) * 512], preferred_element_type=F32)
        if o is bq:
            qt = (acc * (B_HEAD_DIM ** -0.5 * LOG2E)).T.astype(o.dtype)
            zero = jnp.zeros((B_HEAD_DIM, qt.shape[1]), o.dtype)
            for h in range(B_HEADS):
                r = 2 * h * LANES
                o[r:r + B_HEAD_DIM, :] = qt[h * LANES:h * LANES + B_HEAD_DIM, :]
                o[r + B_HEAD_DIM:r + LANES, :] = zero
                o[r + LANES:r + LANES + B_HEAD_DIM, :] = zero
                o[r + LANES + B_HEAD_DIM:r + 2 * LANES, :] = qt[h * LANES + B_HEAD_DIM:(h + 1) * LANES, :]
        elif o is bv:
            vt = acc.T.astype(o.dtype)
            ones = jnp.ones((VT_ROWS - LANES, vt.shape[1]), o.dtype)
            for h in range(B_HEADS):
                o[h * VT_ROWS:h * VT_ROWS + LANES, :] = vt[h * LANES:(h + 1) * LANES, :]
                o[h * VT_ROWS + LANES:(h + 1) * VT_ROWS, :] = ones
        else:
            o[...] = acc.astype(o.dtype)


def _even_proj(x2, w_bf, B, S):
    T = x2.shape[0]
    tm = PROJ_TM
    nt = S // tm
    tok = lambda dt: jax.ShapeDtypeStruct((T, 512), dt)
    spec = pl.BlockSpec((tm, 512), lambda i: (i, 0))
    tspec = lambda rows: pl.BlockSpec((None, rows, tm), lambda i: (i // nt, 0, i % nt))
    return pl.pallas_call(
        _even_proj_kernel,
        grid=(T // tm,),
        in_specs=[pl.BlockSpec((tm, D_MODEL), lambda i: (i, 0)), _full_spec(w_bf.shape)],
        out_specs=[spec, spec, spec, spec, tspec(2 * B_WIDTH), spec, tspec(B_HEADS * VT_ROWS)],
        out_shape=[tok(BF16), tok(F32), tok(BF16), tok(BF16),
                   jax.ShapeDtypeStruct((B, 2 * B_WIDTH, S), BF16), tok(BF16),
                   jax.ShapeDtypeStruct((B, B_HEADS * VT_ROWS, S), BF16)],
        compiler_params=_cparams(("parallel",)),
        name="even_proj",
    )(x2, w_bf)


def _hgrn_cumsum_matrix(L):
    idx = np.arange(L)
    mats = []
    c = 1
    while c < L:
        start = (idx // c) * c
        end = start + c - 1
        right = ((idx // c) % 2) == 1
        u = idx[None, :]
        m_right = (u >= start[:, None]) & (u <= idx[:, None])
        m_left = (u > idx[:, None]) & (u <= end[:, None])
        mats.append(np.where(right[:, None], m_right, m_left))
        c *= 2
    u = idx[None, :]
    mats.append(u <= idx[:, None])
    mats.append(u > idx[:, None])
    return np.concatenate(mats, axis=0).astype(np.float32)


def _hgrn_kernel(q_ref, f_ref, i_ref, g_ref, lbl_ref, gn_ref, cmat_ref, o_ref, state_ref, *, layer_j):
    L = HGRN_L
    n_levels = int(math.log2(L))

    @pl.when(pl.program_id(2) == 0)
    def _():
        state_ref[...] = jnp.zeros_like(state_ref)

    lbl = lbl_ref[...]
    e = jnp.exp(lbl - jnp.max(lbl, axis=0, keepdims=True))
    soft = e / jnp.sum(e, axis=0, keepdims=True)
    lb = jnp.sum(soft[:layer_j + 1], axis=0, keepdims=True) - soft[0:1]

    z = f_ref[...]
    f = lb + (1.0 - lb) * jax.nn.sigmoid(z)
    logf = jnp.log(f)
    k = 1.0 - f
    qr = q_ref[...].astype(F32)
    q = qr * jax.nn.sigmoid(qr)
    iv = i_ref[...]

    logf3 = _split3(logf)
    row = lax.broadcasted_iota(jnp.int32, (L, 1), 0)
    xor = lax.broadcasted_iota(jnp.int32, (L, L), 0) ^ lax.broadcasted_iota(jnp.int32, (L, L), 1)

    nt = (((1,), (1,)), ((), ()))
    qb = q.astype(BF16)
    kb = k.astype(BF16)
    scores = jnp.where(xor == 0, lax.dot_general(qb, kb, nt, preferred_element_type=F32), 0.0)
    for lvl in range(n_levels):
        c = 1 << lvl
        ex = _sum3(jnp.dot(cmat_ref[lvl * L:(lvl + 1) * L, :], logf3, preferred_element_type=F32))
        gdec = jnp.exp(ex)
        right = ((row >> lvl) & 1) == 1
        ql = jnp.where(right, q * gdec, 0.0).astype(BF16)
        kl = jnp.where(right, 0.0, k * gdec).astype(BF16)
        s_l = lax.dot_general(ql, kl, nt, preferred_element_type=F32)
        scores = scores + jnp.where(xor < 2 * c, s_l, 0.0)

    b_full = _sum3(jnp.dot(cmat_ref[n_levels * L:(n_levels + 1) * L, :], logf3,
                           preferred_element_type=F32))
    b_rest = _sum3(jnp.dot(cmat_ref[(n_levels + 1) * L:(n_levels + 2) * L, :], logf3,
                           preferred_element_type=F32))
    state_t = state_ref[...]
    q_in = (q * jnp.exp(b_full)).astype(BF16)
    o = lax.dot_general(q_in, state_t.astype(BF16), nt, preferred_element_type=F32)
    o = o + jnp.dot(scores.astype(BF16), iv, preferred_element_type=F32)

    k_out = (k * jnp.exp(b_rest)).astype(BF16)
    upd = jnp.dot(iv.astype(F32).T.astype(BF16), k_out, preferred_element_type=F32)
    state_ref[...] = state_t * jnp.exp(b_full[L - 1:L, :]) + upd

    ms = jnp.mean(o * o, axis=-1, keepdims=True)
    gate = g_ref[...].astype(F32)
    o = o * lax.rsqrt(ms + 1e-6) * gn_ref[...] * (gate * jax.nn.sigmoid(gate))
    o_ref[...] = o.astype(o_ref.dtype)


def _hgrn(aq, af, ai, ag, lb_logits, gn, cmat, layer_j, B, S):
    L = HGRN_L
    nb = S // L
    blk = pl.BlockSpec((L, LANES), lambda b, h, t: (b * nb + t, h))
    n_even = lb_logits.shape[0]
    return pl.pallas_call(
        functools.partial(_hgrn_kernel, layer_j=layer_j),
        grid=(B, A_HEADS, nb),
        in_specs=[blk, blk, blk, blk,
                  pl.BlockSpec((n_even, LANES), lambda b, h, t: (0, h)),
                  _full_spec(gn.shape), _full_spec(cmat.shape)],
        out_specs=blk,
        out_shape=jax.ShapeDtypeStruct((B * S, A_WIDTH), BF16),
        scratch_shapes=[pltpu.VMEM((LANES, LANES), F32)],
        compiler_params=_cparams(("parallel", "parallel", "arbitrary")),
        name="hgrn2",
    )(aq, af, ai, ag, lb_logits, gn, cmat)


def _attend(k_ref, vt_ref, qt_ref, streams, mask_fn):
    tk, kd = k_ref.shape
    tq = qt_ref.shape[1]
    chains = []
    for kc in range(0, tk, ATT_KC):
        for st in streams:
            for qc in range(0, tq, ATT_QC):
                mask = "all" if mask_fn is None else mask_fn(kc, qc)
                if not (isinstance(mask, str) and mask == "none"):
                    chains.append((st, qc, kc, None if isinstance(mask, str) else mask))

    def scores(st, qc, kc, mask):
        s = jnp.dot(k_ref[kc:kc + ATT_KC, :], qt_ref[st[0]:st[0] + kd, qc:qc + ATT_QC],
                    preferred_element_type=F32)
        if mask is not None:
            s = jnp.where(mask, s, MASK_VALUE)
        return s

    def softmax(st, qc, s):
        m_ref = st[1]
        cols = slice(qc, qc + ATT_QC)
        m_prev = m_ref[:, cols]
        m_new = jnp.maximum(m_prev, jnp.max(s, axis=0, keepdims=True))
        p = jnp.exp2((s - m_new).astype(BF16))
        alpha = jnp.exp2(m_prev - m_new)
        m_ref[:, cols] = m_new
        return p, alpha

    def weighted_values(st, qc, kc, p, alpha):
        acc_ref, v0 = st[2], st[3]
        cols = slice(qc, qc + ATT_QC)
        acc_ref[:, cols] = alpha * acc_ref[:, cols] + jnp.dot(
            vt_ref[v0:v0 + acc_ref.shape[0], kc:kc + ATT_KC], p, preferred_element_type=F32)

    n = len(chains)
    ready = [scores(*c) for c in chains[:ATT_AHEAD]]
    pending = []
    for i, (st, qc, kc, _) in enumerate(chains):
        s = ready.pop(0)
        if i + ATT_AHEAD < n:
            ready.append(scores(*chains[i + ATT_AHEAD]))
        pending.append((st, qc, kc, *softmax(st, qc, s)))
        if len(pending) > ATT_BEHIND:
            weighted_values(*pending.pop(0))
    for item in pending:
        weighted_values(*item)


def _tri_schedule(n):
    qi = [i for i in range(n) for _ in range(i + 1)]
    ki = [j for i in range(n) for j in range(i + 1)]
    return jnp.asarray(qi, jnp.int32), jnp.asarray(ki, jnp.int32)


def _init_stats(m_ref, acc_ref):
    m_ref[...] = jnp.full_like(m_ref, MASK_VALUE)
    acc_ref[...] = jnp.zeros_like(acc_ref)


def _diff_kernel(qi_ref, ki_ref, q_ref, k_ref, v_ref, lam_ref, gn_ref, o_ref,
                 m1, a1, m2, a2, *, lam_init):
    step = pl.program_id(2)
    qi = qi_ref[step]
    ki = ki_ref[step]

    @pl.when(ki == 0)
    def _():
        _init_stats(m1, a1)
        _init_stats(m2, a2)

    def run(mask_fn):
        _attend(k_ref, v_ref, q_ref, [(0, m1, a1, 0), (LANES, m2, a2, 0)], mask_fn)

    @pl.when(ki < qi)
    def _():
        run(None)

    @pl.when(ki == qi)
    def _():
        def chunk_mask(k0, q0):
            if k0 + ATT_KC <= q0 + DIFF_CHUNK:
                return "all"
            if k0 >= q0 + ATT_QC:
                return "none"
            sh = DIFF_CHUNK.bit_length() - 1
            kk = (lax.broadcasted_iota(jnp.int32, (ATT_KC, ATT_QC), 0) + k0) >> sh
            qq = (lax.broadcasted_iota(jnp.int32, (ATT_KC, ATT_QC), 1) + q0) >> sh
            return kk <= qq
        run(chunk_mask)
        lv = lam_ref[...]
        lam = (jnp.exp(jnp.sum(lv[0:1] * lv[1:2], axis=-1, keepdims=True))
               - jnp.exp(jnp.sum(lv[2:3] * lv[3:4], axis=-1, keepdims=True)) + lam_init)
        o = (a1[:LANES, :] / a1[LANES:LANES + 1, :]
             - lam * (a2[:LANES, :] / a2[LANES:LANES + 1, :])).T
        ms = jnp.mean(o * o, axis=-1, keepdims=True)
        o = o * lax.rsqrt(ms + 1e-6) * gn_ref[...] * (1.0 - lam_init)
        o_ref[...] = o.astype(o_ref.dtype)


def _diff_attention(bq, bk, bv, lamv, gn, lam_init, B, S):
    t = ATT_T
    nq = S // t
    qi, ki = _tri_schedule(nq)
    stat = pltpu.VMEM((1, t), F32)
    acc = pltpu.VMEM((VT_ROWS, t), F32)
    grid_spec = pltpu.PrefetchScalarGridSpec(
        num_scalar_prefetch=2,
        grid=(B, B_HEADS, int(qi.shape[0])),
        in_specs=[
            pl.BlockSpec((None, 2 * LANES, t), lambda b, h, s, qi, ki: (b, h, qi[s])),
            pl.BlockSpec((t, LANES), lambda b, h, s, qi, ki: (b * nq + ki[s], h)),
            pl.BlockSpec((None, VT_ROWS, t), lambda b, h, s, qi, ki: (b, h, ki[s])),
            pl.BlockSpec(lamv.shape, lambda b, h, s, qi, ki: (0, 0)),
            pl.BlockSpec(gn.shape, lambda b, h, s, qi, ki: (0, 0)),
        ],
        out_specs=pl.BlockSpec((t, LANES), lambda b, h, s, qi, ki: (b * nq + qi[s], h)),
        scratch_shapes=[stat, acc, stat, acc],
    )
    return pl.pallas_call(
        functools.partial(_diff_kernel, lam_init=lam_init),
        grid_spec=grid_spec,
        out_shape=jax.ShapeDtypeStruct((B * S, B_WIDTH), BF16),
        compiler_params=_cparams(("parallel", "parallel", "arbitrary")),
        name="diff_attn",
    )(qi, ki, bq, bk, bv, lamv, gn)


FOX_KD = 2 * LANES
FOX_AUG = 16
FOX_VROWS = C_HEAD_DIM + 16


def _fox_selectors():
    n_pairs = C_HEADS // 2
    selk = np.zeros((3 * LANES, n_pairs * LANES), np.float32)
    onek = np.zeros((1, n_pairs * LANES), np.float32)
    selq = np.zeros((C_HEADS * FOX_AUG, 3 * LANES), np.float32)
    oneq = np.zeros((C_HEADS * FOX_AUG, LANES), np.float32)
    for h in range(C_HEADS):
        p, odd = divmod(h, 2)
        for piece in range(3):
            selk[piece * LANES + h, p * LANES + 6 * odd + 3 + piece] = -1.0
            selq[h * FOX_AUG + 6 * odd + piece, piece * LANES + h] = 1.0
            onek[0, p * LANES + 6 * odd + piece] = 1.0
            oneq[h * FOX_AUG + 6 * odd + 3 + piece, :] = 1.0
    return selk, onek, selq, oneq


def _odd_proj_kernel(x_ref, w_ref, bf_ref, qg_ref, kg_ref, gmat_ref, tri_ref,
                     selk_ref, onek_ref, selq_ref, oneq_ref,
                     qt_o, k_o, vt_o, g_o, carry_ref):
    @pl.when(pl.program_id(1) == 0)
    def _():
        carry_ref[...] = jnp.zeros_like(carry_ref)

    xb = x_ref[...].astype(BF16)
    tm = xb.shape[0]
    ch = MXU_DIM
    hpc = ch // C_HEAD_DIM
    gmat = gmat_ref[...]

    def headnorm(acc, g):
        ms = jnp.dot((acc * acc).astype(BF16), gmat, preferred_element_type=F32)
        return acc * lax.rsqrt(ms + 1e-6) * g

    zero_half = jnp.zeros((C_HEAD_DIM, tm), BF16)
    zero_tail = jnp.zeros((FOX_KD - LANES - FOX_AUG, tm), BF16)
    ones = jnp.ones((FOX_VROWS - C_HEAD_DIM, tm), BF16)
    for j in range(D_MODEL // ch):
        acc = jnp.dot(xb, w_ref[:, j * ch:(j + 1) * ch], preferred_element_type=F32)
        qt = (headnorm(acc, qg_ref[...]) * (C_HEAD_DIM ** -0.5 * LOG2E)).T.astype(BF16)
        acc = jnp.dot(xb, w_ref[:, 2 * D_MODEL + j * ch:2 * D_MODEL + (j + 1) * ch],
                      preferred_element_type=F32)
        vt = acc.T.astype(BF16)
        for i in range(hpc):
            h = j * hpc + i
            src = slice(i * C_HEAD_DIM, (i + 1) * C_HEAD_DIM)
            base = h * FOX_KD
            own, other = (base, base + C_HEAD_DIM) if h % 2 == 0 else (base + C_HEAD_DIM, base)
            qt_o[own:own + C_HEAD_DIM, :] = qt[src, :]
            qt_o[other:other + C_HEAD_DIM, :] = zero_half
            qt_o[base + LANES + FOX_AUG:base + FOX_KD, :] = zero_tail
            vt_o[h * FOX_VROWS:h * FOX_VROWS + C_HEAD_DIM, :] = vt[src, :]
            vt_o[h * FOX_VROWS + C_HEAD_DIM:(h + 1) * FOX_VROWS, :] = ones
        acc = jnp.dot(xb, w_ref[:, D_MODEL + j * ch:D_MODEL + (j + 1) * ch], preferred_element_type=F32)
        kn = headnorm(acc, kg_ref[...]).astype(BF16)
        for i in range(ch // LANES):
            p = j * (ch // LANES) + i
            k_o[:, p * FOX_KD:p * FOX_KD + LANES] = kn[:, i * LANES:(i + 1) * LANES]
        acc = jnp.dot(xb, w_ref[:, 3 * D_MODEL + j * ch:3 * D_MODEL + (j + 1) * ch],
                      preferred_element_type=F32)
        g_o[:, j * ch:(j + 1) * ch] = acc.astype(g_o.dtype)

    fl = jnp.dot(xb, w_ref[:, 4 * D_MODEL:4 * D_MODEL + LANES], preferred_element_type=F32) + bf_ref[...]
    logf = jnp.minimum(fl, 0.0) - jnp.log(1.0 + jnp.exp(-jnp.abs(fl)))
    c = _sum3(jnp.dot(tri_ref[...], _split3(logf), preferred_element_type=F32)) + carry_ref[...]
    carry_ref[...] = c[tm - 1:tm, :]
    c3 = _split3(c * LOG2E)
    aug_k = (jnp.dot(c3, selk_ref[...], preferred_element_type=F32) + onek_ref[...]).astype(BF16)
    for p in range(C_HEADS // 2):
        k_o[:, p * FOX_KD + LANES:(p + 1) * FOX_KD] = aug_k[:, p * LANES:(p + 1) * LANES]
    aug_q = lax.dot_general(selq_ref[...], c3, (((1,), (1,)), ((), ())), preferred_element_type=F32)
    aug_q = (aug_q + jnp.concatenate([oneq_ref[...]] * (tm // LANES), axis=1)).astype(BF16)
    for h in range(C_HEADS):
        qt_o[h * FOX_KD + LANES:h * FOX_KD + LANES + FOX_AUG, :] = aug_q[h * FOX_AUG:(h + 1) * FOX_AUG, :]


def _odd_proj(x2, w_bf, bf_pad, qg, kg, gmat, tri, B, S):
    T = B * S
    tm = PROJ_TM
    nt = S // tm
    tok = lambda n: pl.BlockSpec((tm, n), lambda b, t: (b * nt + t, 0))
    tspec = lambda rows: pl.BlockSpec((None, rows, tm), lambda b, t: (b, 0, t))
    selk, onek, selq, oneq = _fox_selectors()
    consts = [jnp.asarray(selk, BF16), jnp.asarray(onek, F32), jnp.asarray(selq, BF16),
              jnp.asarray(oneq, F32)]
    k_width = (C_HEADS // 2) * FOX_KD
    return pl.pallas_call(
        _odd_proj_kernel,
        grid=(B, nt),
        in_specs=[tok(D_MODEL), _full_spec(w_bf.shape), _full_spec(bf_pad.shape), _full_spec(qg.shape),
                  _full_spec(kg.shape), _full_spec(gmat.shape), _full_spec(tri.shape)]
                 + [_full_spec(c.shape) for c in consts],
        out_specs=[tspec(C_HEADS * FOX_KD), tok(k_width), tspec(C_HEADS * FOX_VROWS), tok(D_MODEL)],
        out_shape=[jax.ShapeDtypeStruct((B, C_HEADS * FOX_KD, S), BF16),
                   jax.ShapeDtypeStruct((T, k_width), BF16),
                   jax.ShapeDtypeStruct((B, C_HEADS * FOX_VROWS, S), BF16),
                   jax.ShapeDtypeStruct((T, D_MODEL), BF16)],
        scratch_shapes=[pltpu.VMEM((1, LANES), F32)],
        compiler_params=_cparams(("parallel", "arbitrary")),
        name="odd_proj",
    )(x2, w_bf, bf_pad, qg, kg, gmat, tri, *consts)


def _fox_kernel(qi_ref, ki_ref, q_ref, k_ref, v_ref, o_ref, ma, aa, mb, ab):
    step = pl.program_id(2)
    qi = qi_ref[step]
    ki = ki_ref[step]

    @pl.when(ki == 0)
    def _():
        _init_stats(ma, aa)
        _init_stats(mb, ab)

    def run(mask_fn):
        _attend(k_ref, v_ref, q_ref, [(0, ma, aa, 0), (FOX_KD, mb, ab, FOX_VROWS)], mask_fn)

    @pl.when(ki < qi)
    def _():
        run(None)

    @pl.when(ki == qi)
    def _():
        def chunk_mask(k0, q0):
            if k0 + ATT_KC <= q0 + 1:
                return "all"
            if k0 >= q0 + ATT_QC:
                return "none"
            kk = lax.broadcasted_iota(jnp.int32, (ATT_KC, ATT_QC), 0) + k0
            qq = lax.broadcasted_iota(jnp.int32, (ATT_KC, ATT_QC), 1) + q0
            return kk <= qq
        run(chunk_mask)
        d = C_HEAD_DIM
        o = jnp.concatenate([aa[:d, :] / aa[d:d + 1, :], ab[:d, :] / ab[d:d + 1, :]], axis=0)
        o_ref[...] = o.T.astype(o_ref.dtype)


def _fox_attention(qt, k, vt, B, S):
    t = ATT_T
    nq = S // t
    qi, ki = _tri_schedule(nq)
    n_pairs = C_HEADS // 2
    stat = pltpu.VMEM((1, t), F32)
    acc = pltpu.VMEM((FOX_VROWS, t), F32)
    grid_spec = pltpu.PrefetchScalarGridSpec(
        num_scalar_prefetch=2,
        grid=(B, n_pairs, int(qi.shape[0])),
        in_specs=[
            pl.BlockSpec((None, 2 * FOX_KD, t), lambda b, h, s, qi, ki: (b, h, qi[s])),
            pl.BlockSpec((t, FOX_KD), lambda b, h, s, qi, ki: (b * nq + ki[s], h)),
            pl.BlockSpec((None, 2 * FOX_VROWS, t), lambda b, h, s, qi, ki: (b, h, ki[s])),
        ],
        out_specs=pl.BlockSpec((t, LANES), lambda b, h, s, qi, ki: (b * nq + qi[s], h)),
        scratch_shapes=[stat, acc, stat, acc],
    )
    return pl.pallas_call(
        _fox_kernel,
        grid_spec=grid_spec,
        out_shape=jax.ShapeDtypeStruct((B * S, D_MODEL), BF16),
        compiler_params=_cparams(("parallel", "parallel", "arbitrary")),
        name="fox_attn",
    )(qi, ki, qt, k, vt)


def _even_out_kernel(oa_ref, ob_ref, x_ref, w_ref, g_ref, b_ref, y_ref):
    h = jnp.dot(oa_ref[...], w_ref[:A_WIDTH, :], preferred_element_type=F32)
    h = h + jnp.dot(ob_ref[...], w_ref[A_WIDTH:, :], preferred_element_type=F32)
    y_ref[...] = _layer_norm_rows(ALPHA * x_ref[...] + h, g_ref[...], b_ref[...])


def _odd_out_kernel(o_ref, gate_ref, x_ref, w_ref, g_ref, b_ref, y_ref):
    o = o_ref[...].astype(F32) * jax.nn.sigmoid(gate_ref[...].astype(F32))
    h = jnp.dot(o.astype(BF16), w_ref[...], preferred_element_type=F32)
    y_ref[...] = _layer_norm_rows(ALPHA * x_ref[...] + h, g_ref[...], b_ref[...])


def _out_proj(kernel, acts, x2, w_bf, g, b, name):
    T = x2.shape[0]
    tm = PROJ_TM
    tok = lambda n: pl.BlockSpec((tm, n), lambda i: (i, 0))
    return pl.pallas_call(
        kernel,
        grid=(T // tm,),
        in_specs=[tok(a.shape[1]) for a in acts] + [tok(D_MODEL), _full_spec(w_bf.shape),
                                                    _full_spec(g.shape), _full_spec(b.shape)],
        out_specs=tok(D_MODEL),
        out_shape=jax.ShapeDtypeStruct((T, D_MODEL), F32),
        compiler_params=_cparams(("parallel",)),
        name=name,
    )(*acts, x2, w_bf, g, b)


def _ffn_kernel(x_ref, w1_ref, w2_ref, g_ref, b_ref, y_ref, act_ref):
    x = x_ref[...]
    xb = x.astype(BF16)
    ch = MXU_DIM
    for j in range(D_FF // ch):
        gate = jnp.dot(xb, w1_ref[:, j * ch:(j + 1) * ch], preferred_element_type=F32)
        up = jnp.dot(xb, w1_ref[:, D_FF + j * ch:D_FF + (j + 1) * ch], preferred_element_type=F32)
        act_ref[:, j * ch:(j + 1) * ch] = (gate * jax.nn.sigmoid(gate) * up).astype(BF16)
    h = jnp.dot(act_ref[...], w2_ref[...], preferred_element_type=F32)
    y_ref[...] = _layer_norm_rows(ALPHA * x + h, g_ref[...], b_ref[...])


def _ffn(x2, w1_bf, w2_bf, g, b):
    T = x2.shape[0]
    tm = FFN_TM
    tok = pl.BlockSpec((tm, D_MODEL), lambda i: (i, 0))
    return pl.pallas_call(
        _ffn_kernel,
        grid=(T // tm,),
        in_specs=[tok, _full_spec(w1_bf.shape), _full_spec(w2_bf.shape),
                  _full_spec(g.shape), _full_spec(b.shape)],
        out_specs=tok,
        out_shape=jax.ShapeDtypeStruct((T, D_MODEL), F32),
        scratch_shapes=[pltpu.VMEM((tm, D_FF), BF16)],
        compiler_params=_cparams(("parallel",)),
        name="ffn",
    )(x2, w1_bf, w2_bf, g, b)


def kernel(x, even_w_in, even_w_out, hgrn_lb_logits, diff_lq1, diff_lk1, diff_lq2, diff_lk2,
           hgrn_norm_g, diff_norm_g, fox_w_in, fox_w_out, fox_b_f, fox_qnorm_g, fox_knorm_g,
           ffn_w1, ffn_w2, ln1_g, ln1_b, ln2_g, ln2_b):
    B, S, D = x.shape
    assert D == D_MODEL and S % ATT_T == 0 and S % PROJ_TM == 0 and S % HGRN_L == 0
    T = B * S
    x2 = x.reshape(T, D).astype(F32)

    cmat = jnp.asarray(_hgrn_cumsum_matrix(HGRN_L), BF16)
    tri = jnp.asarray(np.tril(np.ones((PROJ_TM, PROJ_TM), np.float32)), BF16)
    head_of = np.arange(MXU_DIM) // C_HEAD_DIM
    gmat = jnp.asarray((head_of[:, None] == head_of[None, :]).astype(np.float32) / C_HEAD_DIM, BF16)
    row = lambda v: v.astype(F32).reshape(1, -1)

    for l in range(DEPTH):
        j = l // 2
        if l % 2 == 0:
            aq, af, ai, ag, bq, bk, bv = _even_proj(x2, even_w_in[j].astype(BF16), B, S)
            o_a = _hgrn(aq, af, ai, ag, hgrn_lb_logits.astype(F32), row(hgrn_norm_g[j]), cmat, j, B, S)
            lamv = jnp.zeros((8, B_HEAD_DIM), F32).at[0:4].set(
                jnp.stack([diff_lq1[j], diff_lk1[j], diff_lq2[j], diff_lk2[j]]).astype(F32))
            lam_init = 0.8 - 0.6 * math.exp(-0.3 * l)
            o_b = _diff_attention(bq, bk, bv, lamv, row(diff_norm_g[j]), lam_init, B, S)
            x2 = _out_proj(_even_out_kernel, [o_a, o_b], x2, even_w_out[j].astype(BF16),
                           row(ln1_g[l]), row(ln1_b[l]), "even_out")
        else:
            w_pad = jnp.pad(fox_w_in[j], ((0, 0), (0, LANES - C_HEADS))).astype(BF16)
            bf_pad = jnp.pad(fox_b_f[j].astype(F32), (0, LANES - C_HEADS)).reshape(1, LANES)
            tile4 = lambda v: jnp.tile(v.astype(F32), MXU_DIM // C_HEAD_DIM).reshape(1, MXU_DIM)
            qt, k, vt, g = _odd_proj(x2, w_pad, bf_pad, tile4(fox_qnorm_g[j]),
                                     tile4(fox_knorm_g[j]), gmat, tri, B, S)
            o = _fox_attention(qt, k, vt, B, S)
            x2 = _out_proj(_odd_out_kernel, [o, g], x2, fox_w_out[j].astype(BF16),
                           row(ln1_g[l]), row(ln1_b[l]), "odd_out")
        x2 = _ffn(x2, ffn_w1[l].astype(BF16), ffn_w2[l].astype(BF16), row(ln2_g[l]), row(ln2_b[l]))
    return x2.reshape(B, S, D).astype(x.dtype)
```

```python
import functools
import math

import numpy as np
import jax
import jax.numpy as jnp
from jax import lax
from jax.experimental import pallas as pl
from jax.experimental.pallas import tpu as pltpu

F32 = jnp.float32
BF16 = jnp.bfloat16

D_MODEL = 1024
DEPTH = 4
A_WIDTH = 512
A_HEADS = 4
A_DK = 128
B_WIDTH = 512
B_HEADS = 4
B_HEAD_DIM = 64
DIFF_CHUNK = 64
C_HEADS = 16
C_HEAD_DIM = 64
D_FF = 2816
ALPHA = (2 * DEPTH) ** 0.25
MASK_VALUE = -1e30

LANES = 128
MXU_DIM = 256

PROJ_TM = 512
FFN_TM = 512
ATT_TQ = 2048
ATT_TK = 2048
ATT_QC = 256
ATT_KC = 256
ATT_AHEAD = 3
ATT_BEHIND = 2
VT_ROWS = LANES + 16
HGRN_L = 256
LOG2E = math.log2(math.e)
VMEM_LIMIT = 56 * 1024 * 1024


def _cparams(sem):
    return pltpu.CompilerParams(dimension_semantics=sem, vmem_limit_bytes=VMEM_LIMIT)


def _full_spec(shape):
    nd = len(shape)
    return pl.BlockSpec(shape, lambda *_: (0,) * nd)


def _split3(x):
    h = x.astype(BF16)
    r = x - h.astype(F32)
    m = r.astype(BF16)
    l = (r - m.astype(F32)).astype(BF16)
    return jnp.concatenate([h, m, l], axis=1)


def _sum3(y):
    return y[:, :LANES] + y[:, LANES:2 * LANES] + y[:, 2 * LANES:]


def _layer_norm_rows(y, g, b):
    mu = jnp.mean(y, axis=-1, keepdims=True)
    d = y - mu
    var = jnp.mean(d * d, axis=-1, keepdims=True)
    return d * lax.rsqrt(var + 1e-5) * g + b


def _even_proj_kernel(x_ref, w_ref, aq, af, ai, ag, bq, bk, bv):
    xb = x_ref[...].astype(BF16)
    outs = (aq, af, ai, ag, bq, bk, bv)
    for n, o in enumerate(outs):
        acc = jnp.dot(xb, w_ref[:, n * 512:(n + 1) * 512], preferred_element_type=F32)
        if o is bq:
            qt = (acc * (B_HEAD_DIM ** -0.5 * LOG2E)).T.astype(o.dtype)
            zero = jnp.zeros((B_HEAD_DIM, qt.shape[1]), o.dtype)
            for h in range(B_HEADS):
                r = 2 * h * LANES
                o[r:r + B_HEAD_DIM, :] = qt[h * LANES:h * LANES + B_HEAD_DIM, :]
                o[r + B_HEAD_DIM:r + LANES, :] = zero
                o[r + LANES:r + LANES + B_HEAD_DIM, :] = zero
                o[r + LANES + B_HEAD_DIM:r + 2 * LANES, :] = qt[h * LANES + B_HEAD_DIM:(h + 1) * LANES, :]
        elif o is bv:
            vt = acc.T.astype(o.dtype)
            ones = jnp.ones((VT_ROWS - LANES, vt.shape[1]), o.dtype)
            for h in range(B_HEADS):
                o[h * VT_ROWS:h * VT_ROWS + LANES, :] = vt[h * LANES:(h + 1) * LANES, :]
                o[h * VT_ROWS + LANES:(h + 1) * VT_ROWS, :] = ones
        else:
            o[...] = acc.astype(o.dtype)


def _even_proj(x2, w_bf, B, S):
    T = x2.shape[0]
    tm = PROJ_TM
    nt = S // tm
    tok = lambda dt: jax.ShapeDtypeStruct((T, 512), dt)
    spec = pl.BlockSpec((tm, 512), lambda i: (i, 0))
    tspec = lambda rows: pl.BlockSpec((None, rows, tm), lambda i: (i // nt, 0, i % nt))
    return pl.pallas_call(
        _even_proj_kernel,
        grid=(T // tm,),
        in_specs=[pl.BlockSpec((tm, D_MODEL), lambda i: (i, 0)), _full_spec(w_bf.shape)],
        out_specs=[spec, spec, spec, spec, tspec(2 * B_WIDTH), spec, tspec(B_HEADS * VT_ROWS)],
        out_shape=[tok(BF16), tok(F32), tok(BF16), tok(BF16),
                   jax.ShapeDtypeStruct((B, 2 * B_WIDTH, S), BF16), tok(BF16),
                   jax.ShapeDtypeStruct((B, B_HEADS * VT_ROWS, S), BF16)],
        compiler_params=_cparams(("parallel",)),
        name="even_proj",
    )(x2, w_bf)


def _hgrn_cumsum_matrix(L):
    idx = np.arange(L)
    mats = []
    c = 1
    while c < L:
        start = (idx // c) * c
        end = start + c - 1
        right = ((idx // c) % 2) == 1
        u = idx[None, :]
        m_right = (u >= start[:, None]) & (u <= idx[:, None])
        m_left = (u > idx[:, None]) & (u <= end[:, None])
        mats.append(np.where(right[:, None], m_right, m_left))
        c *= 2
    u = idx[None, :]
    mats.append(u <= idx[:, None])
    mats.append(u > idx[:, None])
    return np.concatenate(mats, axis=0).astype(np.float32)


def _split2(x):
    h = x.astype(BF16)
    m = (x - h.astype(F32)).astype(BF16)
    return jnp.concatenate([h, m], axis=1)


def _hgrn_kernel(q_ref, f_ref, i_ref, g_ref, lbl_ref, gn_ref, cmat_ref, o_ref, state_ref, *, layer_j):
    L = HGRN_L
    n_levels = int(math.log2(L))
    nt = (((1,), (1,)), ((), ()))

    @pl.when(pl.program_id(1) == 0)
    def _():
        state_ref[...] = jnp.zeros_like(state_ref)

    lbl = lbl_ref[...]
    e = jnp.exp(lbl - jnp.max(lbl, axis=0, keepdims=True))
    soft = e / jnp.sum(e, axis=0, keepdims=True)
    lb_all = jnp.sum(soft[:layer_j + 1], axis=0, keepdims=True) - soft[0:1]

    row = lax.broadcasted_iota(jnp.int32, (L, 1), 0)
    xor = lax.broadcasted_iota(jnp.int32, (L, L), 0) ^ lax.broadcasted_iota(jnp.int32, (L, L), 1)
    cmat = cmat_ref[...]

    heads = range(A_HEADS)
    cols = [slice(h * LANES, (h + 1) * LANES) for h in heads]
    q, k, gdec = [], [], []
    for h in heads:
        lb = lb_all[:, cols[h]]
        f = lb + (1.0 - lb) * jax.nn.sigmoid(f_ref[:, cols[h]])
        k.append(1.0 - f)
        qr = q_ref[:, cols[h]].astype(F32)
        q.append(qr * jax.nn.sigmoid(qr))
        ex = jnp.dot(cmat, _split2(jnp.log(f)), preferred_element_type=F32)
        gdec.append(jnp.exp(ex[:, :LANES] + ex[:, LANES:]))

    scores = []
    for h in heads:
        s = jnp.where(xor == 0, lax.dot_general(q[h].astype(BF16), k[h].astype(BF16), nt,
                                                preferred_element_type=F32), 0.0)
        for lvl in range(n_levels):
            g_l = gdec[h][lvl * L:(lvl + 1) * L, :]
            right = ((row >> lvl) & 1).astype(F32)
            ql = (q[h] * g_l * right).astype(BF16)
            kl = (k[h] * g_l * (1.0 - right)).astype(BF16)
            s_l = lax.dot_general(ql, kl, nt, preferred_element_type=F32)
            s = s + jnp.where(xor < (2 << lvl), s_l, 0.0)
        scores.append(s.astype(BF16))

    for h in heads:
        iv = i_ref[:, cols[h]]
        g_full = gdec[h][n_levels * L:(n_levels + 1) * L, :]
        g_rest = gdec[h][(n_levels + 1) * L:(n_levels + 2) * L, :]
        state_t = state_ref[h]
        o = lax.dot_general((q[h] * g_full).astype(BF16), state_t.astype(BF16), nt,
                            preferred_element_type=F32)
        o = o + jnp.dot(scores[h], iv, preferred_element_type=F32)
        k_out = (k[h] * g_rest).astype(BF16)
        upd = jnp.dot(iv.astype(F32).T.astype(BF16), k_out, preferred_element_type=F32)
        state_ref[h] = state_t * g_full[L - 1:L, :] + upd
        ms = jnp.mean(o * o, axis=-1, keepdims=True)
        gate = g_ref[:, cols[h]].astype(F32)
        o = o * lax.rsqrt(ms + 1e-6) * gn_ref[...] * (gate * jax.nn.sigmoid(gate))
        o_ref[:, cols[h]] = o.astype(o_ref.dtype)


def _hgrn(aq, af, ai, ag, lb_logits, gn, cmat, layer_j, B, S):
    L = HGRN_L
    nb = S // L
    blk = pl.BlockSpec((L, A_WIDTH), lambda b, t: (b * nb + t, 0))
    return pl.pallas_call(
        functools.partial(_hgrn_kernel, layer_j=layer_j),
        grid=(B, nb),
        in_specs=[blk, blk, blk, blk, _full_spec(lb_logits.shape),
                  _full_spec(gn.shape), _full_spec(cmat.shape)],
        out_specs=blk,
        out_shape=jax.ShapeDtypeStruct((B * S, A_WIDTH), BF16),
        scratch_shapes=[pltpu.VMEM((A_HEADS, LANES, LANES), F32)],
        compiler_params=_cparams(("parallel", "arbitrary")),
        name="hgrn2",
    )(aq, af, ai, ag, lb_logits, gn, cmat)


def _attend(k_ref, vt_ref, qt_ref, streams, mask_fn):
    tk, kd = k_ref.shape
    tq = qt_ref.shape[1]
    chains = []
    for kc in range(0, tk, ATT_KC):
        for st in streams:
            for qc in range(0, tq, ATT_QC):
                mask = "all" if mask_fn is None else mask_fn(kc, qc)
                if not (isinstance(mask, str) and mask == "none"):
                    chains.append((st, qc, kc, None if isinstance(mask, str) else mask))

    def scores(st, qc, kc, mask):
        s = jnp.dot(k_ref[kc:kc + ATT_KC, :], qt_ref[st[0]:st[0] + kd, qc:qc + ATT_QC],
                    preferred_element_type=F32)
        if mask is not None:
            s = jnp.where(mask, s, MASK_VALUE)
        return s

    def softmax(st, qc, s):
        m_ref = st[1]
        cols = slice(qc, qc + ATT_QC)
        m_prev = m_ref[:, cols]
        m_new = jnp.maximum(m_prev, jnp.max(s, axis=0, keepdims=True))
        p = jnp.exp2((s - m_new).astype(BF16))
        alpha = jnp.exp2(m_prev - m_new)
        m_ref[:, cols] = m_new
        return p, alpha

    def weighted_values(st, qc, kc, p, alpha):
        acc_ref, v0 = st[2], st[3]
        cols = slice(qc, qc + ATT_QC)
        acc_ref[:, cols] = alpha * acc_ref[:, cols] + jnp.dot(
            vt_ref[v0:v0 + acc_ref.shape[0], kc:kc + ATT_KC], p, preferred_element_type=F32)

    n = len(chains)
    ready = [scores(*c) for c in chains[:ATT_AHEAD]]
    pending = []
    for i, (st, qc, kc, _) in enumerate(chains):
        s = ready.pop(0)
        if i + ATT_AHEAD < n:
            ready.append(scores(*chains[i + ATT_AHEAD]))
        pending.append((st, qc, kc, *softmax(st, qc, s)))
        if len(pending) > ATT_BEHIND:
            weighted_values(*pending.pop(0))
    for item in pending:
        weighted_values(*item)


ATT_MULT = ATT_TQ // ATT_TK


def _tri_schedule(nq):
    qi = [i for i in range(nq) for _ in range((i + 1) * ATT_MULT)]
    ki = [j for i in range(nq) for j in range((i + 1) * ATT_MULT)]
    return jnp.asarray(qi, jnp.int32), jnp.asarray(ki, jnp.int32)


def _attend_tile(qi, ki, run, mask_for, finalize):
    rel = ki - qi * ATT_MULT

    @pl.when(rel < 0)
    def _():
        run(None)

    for r in range(ATT_MULT):
        @pl.when(rel == r)
        def _(r=r):
            run(mask_for(r * ATT_TK))
            if r == ATT_MULT - 1:
                finalize()


def _init_stats(m_ref, acc_ref):
    m_ref[...] = jnp.full_like(m_ref, MASK_VALUE)
    acc_ref[...] = jnp.zeros_like(acc_ref)


def _diff_kernel(qi_ref, ki_ref, q_ref, k_ref, v_ref, lam_ref, gn_ref, o_ref,
                 m1, a1, m2, a2, *, lam_init):
    step = pl.program_id(2)
    qi = qi_ref[step]
    ki = ki_ref[step]

    @pl.when(ki == 0)
    def _():
        _init_stats(m1, a1)
        _init_stats(m2, a2)

    def run(mask_fn):
        _attend(k_ref, v_ref, q_ref, [(0, m1, a1, 0), (LANES, m2, a2, 0)], mask_fn)

    def mask_for(key_offset):
        def chunk_mask(k0, q0):
            k0 = k0 + key_offset
            if k0 + ATT_KC <= q0 + DIFF_CHUNK:
                return "all"
            if k0 >= q0 + ATT_QC:
                return "none"
            sh = DIFF_CHUNK.bit_length() - 1
            kk = (lax.broadcasted_iota(jnp.int32, (ATT_KC, ATT_QC), 0) + k0) >> sh
            qq = (lax.broadcasted_iota(jnp.int32, (ATT_KC, ATT_QC), 1) + q0) >> sh
            return kk <= qq
        return chunk_mask

    def finalize():
        lv = lam_ref[...]
        lam = (jnp.exp(jnp.sum(lv[0:1] * lv[1:2], axis=-1, keepdims=True))
               - jnp.exp(jnp.sum(lv[2:3] * lv[3:4], axis=-1, keepdims=True)) + lam_init)
        o = (a1[:LANES, :] / a1[LANES:LANES + 1, :]
             - lam * (a2[:LANES, :] / a2[LANES:LANES + 1, :])).T
        ms = jnp.mean(o * o, axis=-1, keepdims=True)
        o = o * lax.rsqrt(ms + 1e-6) * gn_ref[...] * (1.0 - lam_init)
        o_ref[...] = o.astype(o_ref.dtype)

    _attend_tile(qi, ki, run, mask_for, finalize)


def _diff_attention(bq, bk, bv, lamv, gn, lam_init, B, S):
    tq, tk = ATT_TQ, ATT_TK
    nq, nk = S // tq, S // tk
    qi, ki = _tri_schedule(nq)
    stat = pltpu.VMEM((1, tq), F32)
    acc = pltpu.VMEM((VT_ROWS, tq), F32)
    grid_spec = pltpu.PrefetchScalarGridSpec(
        num_scalar_prefetch=2,
        grid=(B, B_HEADS, int(qi.shape[0])),
        in_specs=[
            pl.BlockSpec((None, 2 * LANES, tq), lambda b, h, s, qi, ki: (b, h, qi[s])),
            pl.BlockSpec((tk, LANES), lambda b, h, s, qi, ki: (b * nk + ki[s], h)),
            pl.BlockSpec((None, VT_ROWS, tk), lambda b, h, s, qi, ki: (b, h, ki[s])),
            pl.BlockSpec(lamv.shape, lambda b, h, s, qi, ki: (0, 0)),
            pl.BlockSpec(gn.shape, lambda b, h, s, qi, ki: (0, 0)),
        ],
        out_specs=pl.BlockSpec((tq, LANES), lambda b, h, s, qi, ki: (b * nq + qi[s], h)),
        scratch_shapes=[stat, acc, stat, acc],
    )
    return pl.pallas_call(
        functools.partial(_diff_kernel, lam_init=lam_init),
        grid_spec=grid_spec,
        out_shape=jax.ShapeDtypeStruct((B * S, B_WIDTH), BF16),
        compiler_params=_cparams(("parallel", "parallel", "arbitrary")),
        name="diff_attn",
    )(qi, ki, bq, bk, bv, lamv, gn)


FOX_KD = 2 * LANES
FOX_AUG = 16
FOX_VROWS = C_HEAD_DIM + 16


def _fox_selectors():
    n_pairs = C_HEADS // 2
    selk = np.zeros((3 * LANES, n_pairs * LANES), np.float32)
    onek = np.zeros((1, n_pairs * LANES), np.float32)
    selq = np.zeros((C_HEADS * FOX_AUG, 3 * LANES), np.float32)
    oneq = np.zeros((C_HEADS * FOX_AUG, LANES), np.float32)
    for h in range(C_HEADS):
        p, odd = divmod(h, 2)
        for piece in range(3):
            selk[piece * LANES + h, p * LANES + 6 * odd + 3 + piece] = -1.0
            selq[h * FOX_AUG + 6 * odd + piece, piece * LANES + h] = 1.0
            onek[0, p * LANES + 6 * odd + piece] = 1.0
            oneq[h * FOX_AUG + 6 * odd + 3 + piece, :] = 1.0
    return selk, onek, selq, oneq


def _odd_proj_kernel(x_ref, w_ref, bf_ref, qg_ref, kg_ref, gmat_ref, tri_ref,
                     selk_ref, onek_ref, selq_ref, oneq_ref,
                     qt_o, k_o, vt_o, g_o, carry_ref):
    @pl.when(pl.program_id(1) == 0)
    def _():
        carry_ref[...] = jnp.zeros_like(carry_ref)

    xb = x_ref[...].astype(BF16)
    tm = xb.shape[0]
    ch = MXU_DIM
    hpc = ch // C_HEAD_DIM
    gmat = gmat_ref[...]

    def headnorm(acc, g):
        ms = jnp.dot((acc * acc).astype(BF16), gmat, preferred_element_type=F32)
        return acc * lax.rsqrt(ms + 1e-6) * g

    zero_half = jnp.zeros((C_HEAD_DIM, tm), BF16)
    zero_tail = jnp.zeros((FOX_KD - LANES - FOX_AUG, tm), BF16)
    ones = jnp.ones((FOX_VROWS - C_HEAD_DIM, tm), BF16)
    for j in range(D_MODEL // ch):
        acc = jnp.dot(xb, w_ref[:, j * ch:(j + 1) * ch], preferred_element_type=F32)
        qt = (headnorm(acc, qg_ref[...]) * (C_HEAD_DIM ** -0.5 * LOG2E)).T.astype(BF16)
        acc = jnp.dot(xb, w_ref[:, 2 * D_MODEL + j * ch:2 * D_MODEL + (j + 1) * ch],
                      preferred_element_type=F32)
        vt = acc.T.astype(BF16)
        for i in range(hpc):
            h = j * hpc + i
            src = slice(i * C_HEAD_DIM, (i + 1) * C_HEAD_DIM)
            base = h * FOX_KD
            own, other = (base, base + C_HEAD_DIM) if h % 2 == 0 else (base + C_HEAD_DIM, base)
            qt_o[own:own + C_HEAD_DIM, :] = qt[src, :]
            qt_o[other:other + C_HEAD_DIM, :] = zero_half
            qt_o[base + LANES + FOX_AUG:base + FOX_KD, :] = zero_tail
            vt_o[h * FOX_VROWS:h * FOX_VROWS + C_HEAD_DIM, :] = vt[src, :]
            vt_o[h * FOX_VROWS + C_HEAD_DIM:(h + 1) * FOX_VROWS, :] = ones
        acc = jnp.dot(xb, w_ref[:, D_MODEL + j * ch:D_MODEL + (j + 1) * ch], preferred_element_type=F32)
        kn = headnorm(acc, kg_ref[...]).astype(BF16)
        for i in range(ch // LANES):
            p = j * (ch // LANES) + i
            k_o[:, p * FOX_KD:p * FOX_KD + LANES] = kn[:, i * LANES:(i + 1) * LANES]
        acc = jnp.dot(xb, w_ref[:, 3 * D_MODEL + j * ch:3 * D_MODEL + (j + 1) * ch],
                      preferred_element_type=F32)
        g_o[:, j * ch:(j + 1) * ch] = acc.astype(g_o.dtype)

    fl = jnp.dot(xb, w_ref[:, 4 * D_MODEL:4 * D_MODEL + LANES], preferred_element_type=F32) + bf_ref[...]
    logf = jnp.minimum(fl, 0.0) - jnp.log(1.0 + jnp.exp(-jnp.abs(fl)))
    c = _sum3(jnp.dot(tri_ref[...], _split3(logf), preferred_element_type=F32)) + carry_ref[...]
    carry_ref[...] = c[tm - 1:tm, :]
    c3 = _split3(c * LOG2E)
    aug_k = (jnp.dot(c3, selk_ref[...], preferred_element_type=F32) + onek_ref[...]).astype(BF16)
    for p in range(C_HEADS // 2):
        k_o[:, p * FOX_KD + LANES:(p + 1) * FOX_KD] = aug_k[:, p * LANES:(p + 1) * LANES]
    aug_q = lax.dot_general(selq_ref[...], c3, (((1,), (1,)), ((), ())), preferred_element_type=F32)
    aug_q = (aug_q + jnp.concatenate([oneq_ref[...]] * (tm // LANES), axis=1)).astype(BF16)
    for h in range(C_HEADS):
        qt_o[h * FOX_KD + LANES:h * FOX_KD + LANES + FOX_AUG, :] = aug_q[h * FOX_AUG:(h + 1) * FOX_AUG, :]


def _odd_proj(x2, w_bf, bf_pad, qg, kg, gmat, tri, B, S):
    T = B * S
    tm = PROJ_TM
    nt = S // tm
    tok = lambda n: pl.BlockSpec((tm, n), lambda b, t: (b * nt + t, 0))
    tspec = lambda rows: pl.BlockSpec((None, rows, tm), lambda b, t: (b, 0, t))
    selk, onek, selq, oneq = _fox_selectors()
    consts = [jnp.asarray(selk, BF16), jnp.asarray(onek, F32), jnp.asarray(selq, BF16),
              jnp.asarray(oneq, F32)]
    k_width = (C_HEADS // 2) * FOX_KD
    return pl.pallas_call(
        _odd_proj_kernel,
        grid=(B, nt),
        in_specs=[tok(D_MODEL), _full_spec(w_bf.shape), _full_spec(bf_pad.shape), _full_spec(qg.shape),
                  _full_spec(kg.shape), _full_spec(gmat.shape), _full_spec(tri.shape)]
                 + [_full_spec(c.shape) for c in consts],
        out_specs=[tspec(C_HEADS * FOX_KD), tok(k_width), tspec(C_HEADS * FOX_VROWS), tok(D_MODEL)],
        out_shape=[jax.ShapeDtypeStruct((B, C_HEADS * FOX_KD, S), BF16),
                   jax.ShapeDtypeStruct((T, k_width), BF16),
                   jax.ShapeDtypeStruct((B, C_HEADS * FOX_VROWS, S), BF16),
                   jax.ShapeDtypeStruct((T, D_MODEL), BF16)],
        scratch_shapes=[pltpu.VMEM((1, LANES), F32)],
        compiler_params=_cparams(("parallel", "arbitrary")),
        name="odd_proj",
    )(x2, w_bf, bf_pad, qg, kg, gmat, tri, *consts)


def _fox_kernel(qi_ref, ki_ref, q_ref, k_ref, v_ref, o_ref, ma, aa, mb, ab):
    step = pl.program_id(2)
    qi = qi_ref[step]
    ki = ki_ref[step]

    @pl.when(ki == 0)
    def _():
        _init_stats(ma, aa)
        _init_stats(mb, ab)

    def run(mask_fn):
        _attend(k_ref, v_ref, q_ref, [(0, ma, aa, 0), (FOX_KD, mb, ab, FOX_VROWS)], mask_fn)

    def mask_for(key_offset):
        def chunk_mask(k0, q0):
            k0 = k0 + key_offset
            if k0 + ATT_KC <= q0 + 1:
                return "all"
            if k0 >= q0 + ATT_QC:
                return "none"
            kk = lax.broadcasted_iota(jnp.int32, (ATT_KC, ATT_QC), 0) + k0
            qq = lax.broadcasted_iota(jnp.int32, (ATT_KC, ATT_QC), 1) + q0
            return kk <= qq
        return chunk_mask

    def finalize():
        d = C_HEAD_DIM
        o = jnp.concatenate([aa[:d, :] / aa[d:d + 1, :], ab[:d, :] / ab[d:d + 1, :]], axis=0)
        o_ref[...] = o.T.astype(o_ref.dtype)

    _attend_tile(qi, ki, run, mask_for, finalize)


def _fox_attention(qt, k, vt, B, S):
    tq, tk = ATT_TQ, ATT_TK
    nq, nk = S // tq, S // tk
    qi, ki = _tri_schedule(nq)
    n_pairs = C_HEADS // 2
    stat = pltpu.VMEM((1, tq), F32)
    acc = pltpu.VMEM((FOX_VROWS, tq), F32)
    grid_spec = pltpu.PrefetchScalarGridSpec(
        num_scalar_prefetch=2,
        grid=(B, n_pairs, int(qi.shape[0])),
        in_specs=[
            pl.BlockSpec((None, 2 * FOX_KD, tq), lambda b, h, s, qi, ki: (b, h, qi[s])),
            pl.BlockSpec((tk, FOX_KD), lambda b, h, s, qi, ki: (b * nk + ki[s], h)),
            pl.BlockSpec((None, 2 * FOX_VROWS, tk), lambda b, h, s, qi, ki: (b, h, ki[s])),
        ],
        out_specs=pl.BlockSpec((tq, LANES), lambda b, h, s, qi, ki: (b * nq + qi[s], h)),
        scratch_shapes=[stat, acc, stat, acc],
    )
    return pl.pallas_call(
        _fox_kernel,
        grid_spec=grid_spec,
        out_shape=jax.ShapeDtypeStruct((B * S, D_MODEL), BF16),
        compiler_params=_cparams(("parallel", "parallel", "arbitrary")),
        name="fox_attn",
    )(qi, ki, qt, k, vt)


def _even_out_kernel(oa_ref, ob_ref, x_ref, w_ref, g_ref, b_ref, y_ref):
    h = jnp.dot(oa_ref[...], w_ref[:A_WIDTH, :], preferred_element_type=F32)
    h = h + jnp.dot(ob_ref[...], w_ref[A_WIDTH:, :], preferred_element_type=F32)
    y_ref[...] = _layer_norm_rows(ALPHA * x_ref[...] + h, g_ref[...], b_ref[...])


def _odd_out_kernel(o_ref, gate_ref, x_ref, w_ref, g_ref, b_ref, y_ref):
    o = o_ref[...].astype(F32) * jax.nn.sigmoid(gate_ref[...].astype(F32))
    h = jnp.dot(o.astype(BF16), w_ref[...], preferred_element_type=F32)
    y_ref[...] = _layer_norm_rows(ALPHA * x_ref[...] + h, g_ref[...], b_ref[...])


def _out_proj(kernel, acts, x2, w_bf, g, b, name):
    T = x2.shape[0]
    tm = PROJ_TM
    tok = lambda n: pl.BlockSpec((tm, n), lambda i: (i, 0))
    return pl.pallas_call(
        kernel,
        grid=(T // tm,),
        in_specs=[tok(a.shape[1]) for a in acts] + [tok(D_MODEL), _full_spec(w_bf.shape),
                                                    _full_spec(g.shape), _full_spec(b.shape)],
        out_specs=tok(D_MODEL),
        out_shape=jax.ShapeDtypeStruct((T, D_MODEL), F32),
        compiler_params=_cparams(("parallel",)),
        name=name,
    )(*acts, x2, w_bf, g, b)


def _ffn_kernel(x_ref, w1_ref, w2_ref, g_ref, b_ref, y_ref, act_ref):
    x = x_ref[...]
    xb = x.astype(BF16)
    ch = MXU_DIM
    for j in range(D_FF // ch):
        gate = jnp.dot(xb, w1_ref[:, j * ch:(j + 1) * ch], preferred_element_type=F32)
        up = jnp.dot(xb, w1_ref[:, D_FF + j * ch:D_FF + (j + 1) * ch], preferred_element_type=F32)
        act_ref[:, j * ch:(j + 1) * ch] = (gate * jax.nn.sigmoid(gate) * up).astype(BF16)
    h = jnp.dot(act_ref[...], w2_ref[...], preferred_element_type=F32)
    y_ref[...] = _layer_norm_rows(ALPHA * x + h, g_ref[...], b_ref[...])


def _ffn(x2, w1_bf, w2_bf, g, b):
    T = x2.shape[0]
    tm = FFN_TM
    tok = pl.BlockSpec((tm, D_MODEL), lambda i: (i, 0))
    return pl.pallas_call(
        _ffn_kernel,
        grid=(T // tm,),
        in_specs=[tok, _full_spec(w1_bf.shape), _full_spec(w2_bf.shape),
                  _full_spec(g.shape), _full_spec(b.shape)],
        out_specs=tok,
        out_shape=jax.ShapeDtypeStruct((T, D_MODEL), F32),
        scratch_shapes=[pltpu.VMEM((tm, D_FF), BF16)],
        compiler_params=_cparams(("parallel",)),
        name="ffn",
    )(x2, w1_bf, w2_bf, g, b)


def kernel(x, even_w_in, even_w_out, hgrn_lb_logits, diff_lq1, diff_lk1, diff_lq2, diff_lk2,
           hgrn_norm_g, diff_norm_g, fox_w_in, fox_w_out, fox_b_f, fox_qnorm_g, fox_knorm_g,
           ffn_w1, ffn_w2, ln1_g, ln1_b, ln2_g, ln2_b):
    B, S, D = x.shape
    assert D == D_MODEL and S % ATT_TQ == 0 and S % PROJ_TM == 0 and S % HGRN_L == 0
    T = B * S
    x2 = x.reshape(T, D).astype(F32)

    cmat = jnp.asarray(_hgrn_cumsum_matrix(HGRN_L), BF16)
    tri = jnp.asarray(np.tril(np.ones((PROJ_TM, PROJ_TM), np.float32)), BF16)
    head_of = np.arange(MXU_DIM) // C_HEAD_DIM
    gmat = jnp.asarray((head_of[:, None] == head_of[None, :]).astype(np.float32) / C_HEAD_DIM, BF16)
    row = lambda v: v.astype(F32).reshape(1, -1)

    for l in range(DEPTH):
        j = l // 2
        if l % 2 == 0:
            aq, af, ai, ag, bq, bk, bv = _even_proj(x2, even_w_in[j].astype(BF16), B, S)
            o_a = _hgrn(aq, af, ai, ag, hgrn_lb_logits.astype(F32), row(hgrn_norm_g[j]), cmat, j, B, S)
            lamv = jnp.zeros((8, B_HEAD_DIM), F32).at[0:4].set(
                jnp.stack([diff_lq1[j], diff_lk1[j], diff_lq2[j], diff_lk2[j]]).astype(F32))
            lam_init = 0.8 - 0.6 * math.exp(-0.3 * l)
            o_b = _diff_attention(bq, bk, bv, lamv, row(diff_norm_g[j]), lam_init, B, S)
            x2 = _out_proj(_even_out_kernel, [o_a, o_b], x2, even_w_out[j].astype(BF16),
                           row(ln1_g[l]), row(ln1_b[l]), "even_out")
        else:
            w_pad = jnp.pad(fox_w_in[j], ((0, 0), (0, LANES - C_HEADS))).astype(BF16)
            bf_pad = jnp.pad(fox_b_f[j].astype(F32), (0, LANES - C_HEADS)).reshape(1, LANES)
            tile4 = lambda v: jnp.tile(v.astype(F32), MXU_DIM // C_HEAD_DIM).reshape(1, MXU_DIM)
            qt, k, vt, g = _odd_proj(x2, w_pad, bf_pad, tile4(fox_qnorm_g[j]),
                                     tile4(fox_knorm_g[j]), gmat, tri, B, S)
            o = _fox_attention(qt, k, vt, B, S)
            x2 = _out_proj(_odd_out_kernel, [o, g], x2, fox_w_out[j].astype(BF16),
                           row(ln1_g[l]), row(ln1_b[l]), "odd_out")
        x2 = _ffn(x2, ffn_w1[l].astype(BF16), ffn_w2[l].astype(BF16), row(ln2_g[l]), row(ln2_b[l]))
    return x2.reshape(B, S, D).astype(x.dtype)
```

```python
import functools
import math

import numpy as np
import jax
import jax.numpy as jnp
from jax import lax
from jax.experimental import pallas as pl
from jax.experimental.pallas import tpu as pltpu

F32 = jnp.float32
BF16 = jnp.bfloat16

D_MODEL = 1024
DEPTH = 4
A_WIDTH = 512
A_HEADS = 4
A_DK = 128
B_WIDTH = 512
B_HEADS = 4
B_HEAD_DIM = 64
DIFF_CHUNK = 64
C_HEADS = 16
C_HEAD_DIM = 64
D_FF = 2816
ALPHA = (2 * DEPTH) ** 0.25
MASK_VALUE = -1e30

LANES = 128
MXU_DIM = 256

PROJ_TM = 512
FFN_TM = 512
ATT_TQ = 2048
ATT_TK = 2048
ATT_QC = 256
ATT_KC = 512
ATT_SUB = 64
ATT_AHEAD = 2
ATT_BEHIND = 2
VT_ROWS = LANES + 16
HGRN_L = 256
LOG2E = math.log2(math.e)
VMEM_LIMIT = 56 * 1024 * 1024


def _cparams(sem):
    return pltpu.CompilerParams(dimension_semantics=sem, vmem_limit_bytes=VMEM_LIMIT)


def _full_spec(shape):
    nd = len(shape)
    return pl.BlockSpec(shape, lambda *_: (0,) * nd)


def _split3(x):
    h = x.astype(BF16)
    r = x - h.astype(F32)
    m = r.astype(BF16)
    l = (r - m.astype(F32)).astype(BF16)
    return jnp.concatenate([h, m, l], axis=1)


def _sum3(y):
    return y[:, :LANES] + y[:, LANES:2 * LANES] + y[:, 2 * LANES:]


def _layer_norm_rows(y, g, b):
    mu = jnp.mean(y, axis=-1, keepdims=True)
    d = y - mu
    var = jnp.mean(d * d, axis=-1, keepdims=True)
    return d * lax.rsqrt(var + 1e-5) * g + b


def _even_proj_kernel(x_ref, w_ref, aq, af, ai, ag, bq, bk, bv):
    xb = x_ref[...].astype(BF16)
    outs = (aq, af, ai, ag, bq, bk, bv)
    for n, o in enumerate(outs):
        acc = jnp.dot(xb, w_ref[:, n * 512:(n + 1) * 512], preferred_element_type=F32)
        if o is bq:
            qt = (acc * (B_HEAD_DIM ** -0.5 * LOG2E)).T.astype(o.dtype)
            zero = jnp.zeros((B_HEAD_DIM, qt.shape[1]), o.dtype)
            for h in range(B_HEADS):
                r = 2 * h * LANES
                o[r:r + B_HEAD_DIM, :] = qt[h * LANES:h * LANES + B_HEAD_DIM, :]
                o[r + B_HEAD_DIM:r + LANES, :] = zero
                o[r + LANES:r + LANES + B_HEAD_DIM, :] = zero
                o[r + LANES + B_HEAD_DIM:r + 2 * LANES, :] = qt[h * LANES + B_HEAD_DIM:(h + 1) * LANES, :]
        elif o is bv:
            vt = acc.T.astype(o.dtype)
            ones = jnp.ones((VT_ROWS - LANES, vt.shape[1]), o.dtype)
            for h in range(B_HEADS):
                o[h * VT_ROWS:h * VT_ROWS + LANES, :] = vt[h * LANES:(h + 1) * LANES, :]
                o[h * VT_ROWS + LANES:(h + 1) * VT_ROWS, :] = ones
        else:
            o[...] = acc.astype(o.dtype)


def _even_proj(x2, w_bf, B, S):
    T = x2.shape[0]
    tm = PROJ_TM
    nt = S // tm
    tok = lambda dt: jax.ShapeDtypeStruct((T, 512), dt)
    spec = pl.BlockSpec((tm, 512), lambda i: (i, 0))
    tspec = lambda rows: pl.BlockSpec((None, rows, tm), lambda i: (i // nt, 0, i % nt))
    return pl.pallas_call(
        _even_proj_kernel,
        grid=(T // tm,),
        in_specs=[pl.BlockSpec((tm, D_MODEL), lambda i: (i, 0)), _full_spec(w_bf.shape)],
        out_specs=[spec, spec, spec, spec, tspec(2 * B_WIDTH), spec, tspec(B_HEADS * VT_ROWS)],
        out_shape=[tok(BF16), tok(F32), tok(BF16), tok(BF16),
                   jax.ShapeDtypeStruct((B, 2 * B_WIDTH, S), BF16), tok(BF16),
                   jax.ShapeDtypeStruct((B, B_HEADS * VT_ROWS, S), BF16)],
        compiler_params=_cparams(("parallel",)),
        name="even_proj",
    )(x2, w_bf)


def _hgrn_cumsum_matrix(L):
    idx = np.arange(L)
    mats = []
    c = 1
    while c < L:
        start = (idx // c) * c
        end = start + c - 1
        right = ((idx // c) % 2) == 1
        u = idx[None, :]
        m_right = (u >= start[:, None]) & (u <= idx[:, None])
        m_left = (u > idx[:, None]) & (u <= end[:, None])
        mats.append(np.where(right[:, None], m_right, m_left))
        c *= 2
    u = idx[None, :]
    mats.append(u <= idx[:, None])
    mats.append(u > idx[:, None])
    return np.concatenate(mats, axis=0).astype(np.float32)


def _split2(x):
    h = x.astype(BF16)
    m = (x - h.astype(F32)).astype(BF16)
    return jnp.concatenate([h, m], axis=1)


def _hgrn_kernel(q_ref, f_ref, i_ref, g_ref, lbl_ref, gn_ref, cmat_ref, o_ref, state_ref, *, layer_j):
    L = HGRN_L
    n_levels = int(math.log2(L))
    nt = (((1,), (1,)), ((), ()))

    @pl.when(pl.program_id(1) == 0)
    def _():
        state_ref[...] = jnp.zeros_like(state_ref)

    lbl = lbl_ref[...]
    e = jnp.exp(lbl - jnp.max(lbl, axis=0, keepdims=True))
    soft = e / jnp.sum(e, axis=0, keepdims=True)
    lb_all = jnp.sum(soft[:layer_j + 1], axis=0, keepdims=True) - soft[0:1]

    row = lax.broadcasted_iota(jnp.int32, (L, 1), 0)
    xor = lax.broadcasted_iota(jnp.int32, (L, L), 0) ^ lax.broadcasted_iota(jnp.int32, (L, L), 1)
    cmat = cmat_ref[...]

    heads = range(A_HEADS)
    cols = [slice(h * LANES, (h + 1) * LANES) for h in heads]
    q, k, gdec = [], [], []
    for h in heads:
        lb = lb_all[:, cols[h]]
        f = lb + (1.0 - lb) * jax.nn.sigmoid(f_ref[:, cols[h]])
        k.append(1.0 - f)
        qr = q_ref[:, cols[h]].astype(F32)
        q.append(qr * jax.nn.sigmoid(qr))
        ex = jnp.dot(cmat, _split2(jnp.log(f)), preferred_element_type=F32)
        gdec.append(jnp.exp(ex[:, :LANES] + ex[:, LANES:]))

    scores = []
    for h in heads:
        s = jnp.where(xor == 0, lax.dot_general(q[h].astype(BF16), k[h].astype(BF16), nt,
                                                preferred_element_type=F32), 0.0)
        for lvl in range(n_levels):
            g_l = gdec[h][lvl * L:(lvl + 1) * L, :]
            right = ((row >> lvl) & 1).astype(F32)
            ql = (q[h] * g_l * right).astype(BF16)
            kl = (k[h] * g_l * (1.0 - right)).astype(BF16)
            s_l = lax.dot_general(ql, kl, nt, preferred_element_type=F32)
            s = s + jnp.where(xor < (2 << lvl), s_l, 0.0)
        scores.append(s.astype(BF16))

    for h in heads:
        iv = i_ref[:, cols[h]]
        g_full = gdec[h][n_levels * L:(n_levels + 1) * L, :]
        g_rest = gdec[h][(n_levels + 1) * L:(n_levels + 2) * L, :]
        state_t = state_ref[h]
        o = lax.dot_general((q[h] * g_full).astype(BF16), state_t.astype(BF16), nt,
                            preferred_element_type=F32)
        o = o + jnp.dot(scores[h], iv, preferred_element_type=F32)
        k_out = (k[h] * g_rest).astype(BF16)
        upd = jnp.dot(iv.astype(F32).T.astype(BF16), k_out, preferred_element_type=F32)
        state_ref[h] = state_t * g_full[L - 1:L, :] + upd
        ms = jnp.mean(o * o, axis=-1, keepdims=True)
        gate = g_ref[:, cols[h]].astype(F32)
        o = o * lax.rsqrt(ms + 1e-6) * gn_ref[...] * (gate * jax.nn.sigmoid(gate))
        o_ref[:, cols[h]] = o.astype(o_ref.dtype)


def _hgrn(aq, af, ai, ag, lb_logits, gn, cmat, layer_j, B, S):
    L = HGRN_L
    nb = S // L
    blk = pl.BlockSpec((L, A_WIDTH), lambda b, t: (b * nb + t, 0))
    return pl.pallas_call(
        functools.partial(_hgrn_kernel, layer_j=layer_j),
        grid=(B, nb),
        in_specs=[blk, blk, blk, blk, _full_spec(lb_logits.shape),
                  _full_spec(gn.shape), _full_spec(cmat.shape)],
        out_specs=blk,
        out_shape=jax.ShapeDtypeStruct((B * S, A_WIDTH), BF16),
        scratch_shapes=[pltpu.VMEM((A_HEADS, LANES, LANES), F32)],
        compiler_params=_cparams(("parallel", "arbitrary")),
        name="hgrn2",
    )(aq, af, ai, ag, lb_logits, gn, cmat)


def _attend(k_ref, vt_ref, qt_ref, streams, mask_fn):
    tk, kd = k_ref.shape
    tq = qt_ref.shape[1]
    chains = []
    for kc in range(0, tk, ATT_KC):
        for st in streams:
            for qc in range(0, tq, ATT_QC):
                mask = "all" if mask_fn is None else mask_fn(kc, qc)
                if not (isinstance(mask, str) and mask == "none"):
                    chains.append((st, qc, kc, None if isinstance(mask, str) else mask))

    def scores(st, qc, kc, mask):
        s = jnp.dot(k_ref[kc:kc + ATT_KC, :], qt_ref[st[0]:st[0] + kd, qc:qc + ATT_QC],
                    preferred_element_type=F32)
        if mask is not None:
            s = jnp.where(mask, s, MASK_VALUE)
        return s

    def softmax(st, qc, s):
        m_ref = st[1]
        cols = slice(qc, qc + ATT_QC)
        m_prev = m_ref[:, cols]
        m_run, parts, maxes = m_prev, [], []
        for r0 in range(0, s.shape[0], ATT_SUB):
            s_g = s[r0:r0 + ATT_SUB, :]
            m_run = jnp.maximum(m_run, jnp.max(s_g, axis=0, keepdims=True))
            parts.append(jnp.exp2((s_g - m_run).astype(BF16)))
            maxes.append(m_run)
        m_new = m_run
        parts = [p_g if m_g is m_new else p_g * jnp.exp2(m_g - m_new).astype(BF16)
                 for p_g, m_g in zip(parts, maxes)]
        p = parts[0] if len(parts) == 1 else jnp.concatenate(parts, axis=0)
        alpha = jnp.exp2(m_prev - m_new)
        m_ref[:, cols] = m_new
        return p, alpha

    def weighted_values(st, qc, kc, p, alpha):
        acc_ref, v0 = st[2], st[3]
        cols = slice(qc, qc + ATT_QC)
        acc_ref[:, cols] = alpha * acc_ref[:, cols] + jnp.dot(
            vt_ref[v0:v0 + acc_ref.shape[0], kc:kc + ATT_KC], p, preferred_element_type=F32)

    n = len(chains)
    ready = [scores(*c) for c in chains[:ATT_AHEAD]]
    pending = []
    for i, (st, qc, kc, _) in enumerate(chains):
        s = ready.pop(0)
        if i + ATT_AHEAD < n:
            ready.append(scores(*chains[i + ATT_AHEAD]))
        pending.append((st, qc, kc, *softmax(st, qc, s)))
        if len(pending) > ATT_BEHIND:
            weighted_values(*pending.pop(0))
    for item in pending:
        weighted_values(*item)


ATT_MULT = ATT_TQ // ATT_TK


def _tri_schedule(nq):
    qi = [i for i in range(nq) for _ in range((i + 1) * ATT_MULT)]
    ki = [j for i in range(nq) for j in range((i + 1) * ATT_MULT)]
    return jnp.asarray(qi, jnp.int32), jnp.asarray(ki, jnp.int32)


def _attend_tile(qi, ki, run, mask_for, finalize):
    rel = ki - qi * ATT_MULT

    @pl.when(rel < 0)
    def _():
        run(None)

    for r in range(ATT_MULT):
        @pl.when(rel == r)
        def _(r=r):
            run(mask_for(r * ATT_TK))
            if r == ATT_MULT - 1:
                finalize()


def _init_stats(m_ref, acc_ref):
    m_ref[...] = jnp.full_like(m_ref, MASK_VALUE)
    acc_ref[...] = jnp.zeros_like(acc_ref)


def _diff_kernel(qi_ref, ki_ref, q_ref, k_ref, v_ref, lam_ref, gn_ref, o_ref,
                 m1, a1, m2, a2, *, lam_init):
    step = pl.program_id(2)
    qi = qi_ref[step]
    ki = ki_ref[step]

    @pl.when(ki == 0)
    def _():
        _init_stats(m1, a1)
        _init_stats(m2, a2)

    def run(mask_fn):
        _attend(k_ref, v_ref, q_ref, [(0, m1, a1, 0), (LANES, m2, a2, 0)], mask_fn)

    def mask_for(key_offset):
        def chunk_mask(k0, q0):
            k0 = k0 + key_offset
            if k0 + ATT_KC <= q0 + DIFF_CHUNK:
                return "all"
            if k0 >= q0 + ATT_QC:
                return "none"
            sh = DIFF_CHUNK.bit_length() - 1
            kk = (lax.broadcasted_iota(jnp.int32, (ATT_KC, ATT_QC), 0) + k0) >> sh
            qq = (lax.broadcasted_iota(jnp.int32, (ATT_KC, ATT_QC), 1) + q0) >> sh
            return kk <= qq
        return chunk_mask

    def finalize():
        lv = lam_ref[...]
        lam = (jnp.exp(jnp.sum(lv[0:1] * lv[1:2], axis=-1, keepdims=True))
               - jnp.exp(jnp.sum(lv[2:3] * lv[3:4], axis=-1, keepdims=True)) + lam_init)
        o = (a1[:LANES, :] / a1[LANES:LANES + 1, :]
             - lam * (a2[:LANES, :] / a2[LANES:LANES + 1, :])).T
        ms = jnp.mean(o * o, axis=-1, keepdims=True)
        o = o * lax.rsqrt(ms + 1e-6) * gn_ref[...] * (1.0 - lam_init)
        o_ref[...] = o.astype(o_ref.dtype)

    _attend_tile(qi, ki, run, mask_for, finalize)


def _diff_attention(bq, bk, bv, lamv, gn, lam_init, B, S):
    tq, tk = ATT_TQ, ATT_TK
    nq, nk = S // tq, S // tk
    qi, ki = _tri_schedule(nq)
    stat = pltpu.VMEM((1, tq), F32)
    acc = pltpu.VMEM((VT_ROWS, tq), F32)
    grid_spec = pltpu.PrefetchScalarGridSpec(
        num_scalar_prefetch=2,
        grid=(B, B_HEADS, int(qi.shape[0])),
        in_specs=[
            pl.BlockSpec((None, 2 * LANES, tq), lambda b, h, s, qi, ki: (b, h, qi[s])),
            pl.BlockSpec((tk, LANES), lambda b, h, s, qi, ki: (b * nk + ki[s], h)),
            pl.BlockSpec((None, VT_ROWS, tk), lambda b, h, s, qi, ki: (b, h, ki[s])),
            pl.BlockSpec(lamv.shape, lambda b, h, s, qi, ki: (0, 0)),
            pl.BlockSpec(gn.shape, lambda b, h, s, qi, ki: (0, 0)),
        ],
        out_specs=pl.BlockSpec((tq, LANES), lambda b, h, s, qi, ki: (b * nq + qi[s], h)),
        scratch_shapes=[stat, acc, stat, acc],
    )
    return pl.pallas_call(
        functools.partial(_diff_kernel, lam_init=lam_init),
        grid_spec=grid_spec,
        out_shape=jax.ShapeDtypeStruct((B * S, B_WIDTH), BF16),
        compiler_params=_cparams(("parallel", "parallel", "arbitrary")),
        name="diff_attn",
    )(qi, ki, bq, bk, bv, lamv, gn)


FOX_KD = 2 * LANES
FOX_AUG = 16
FOX_VROWS = C_HEAD_DIM + 16


def _fox_selectors():
    n_pairs = C_HEADS // 2
    selk = np.zeros((3 * LANES, n_pairs * LANES), np.float32)
    onek = np.zeros((1, n_pairs * LANES), np.float32)
    selq = np.zeros((C_HEADS * FOX_AUG, 3 * LANES), np.float32)
    oneq = np.zeros((C_HEADS * FOX_AUG, LANES), np.float32)
    for h in range(C_HEADS):
        p, odd = divmod(h, 2)
        for piece in range(3):
            selk[piece * LANES + h, p * LANES + 6 * odd + 3 + piece] = -1.0
            selq[h * FOX_AUG + 6 * odd + piece, piece * LANES + h] = 1.0
            onek[0, p * LANES + 6 * odd + piece] = 1.0
            oneq[h * FOX_AUG + 6 * odd + 3 + piece, :] = 1.0
    return selk, onek, selq, oneq


def _odd_proj_kernel(x_ref, w_ref, bf_ref, qg_ref, kg_ref, gmat_ref, tri_ref,
                     selk_ref, onek_ref, selq_ref, oneq_ref,
                     qt_o, k_o, vt_o, g_o, carry_ref):
    @pl.when(pl.program_id(1) == 0)
    def _():
        carry_ref[...] = jnp.zeros_like(carry_ref)

    xb = x_ref[...].astype(BF16)
    tm = xb.shape[0]
    ch = MXU_DIM
    hpc = ch // C_HEAD_DIM
    gmat = gmat_ref[...]

    def headnorm(acc, g):
        ms = jnp.dot((acc * acc).astype(BF16), gmat, preferred_element_type=F32)
        return acc * lax.rsqrt(ms + 1e-6) * g

    zero_half = jnp.zeros((C_HEAD_DIM, tm), BF16)
    zero_tail = jnp.zeros((FOX_KD - LANES - FOX_AUG, tm), BF16)
    ones = jnp.ones((FOX_VROWS - C_HEAD_DIM, tm), BF16)
    for j in range(D_MODEL // ch):
        acc = jnp.dot(xb, w_ref[:, j * ch:(j + 1) * ch], preferred_element_type=F32)
        qt = (headnorm(acc, qg_ref[...]) * (C_HEAD_DIM ** -0.5 * LOG2E)).T.astype(BF16)
        acc = jnp.dot(xb, w_ref[:, 2 * D_MODEL + j * ch:2 * D_MODEL + (j + 1) * ch],
                      preferred_element_type=F32)
        vt = acc.T.astype(BF16)
        for i in range(hpc):
            h = j * hpc + i
            src = slice(i * C_HEAD_DIM, (i + 1) * C_HEAD_DIM)
            base = h * FOX_KD
            own, other = (base, base + C_HEAD_DIM) if h % 2 == 0 else (base + C_HEAD_DIM, base)
            qt_o[own:own + C_HEAD_DIM, :] = qt[src, :]
            qt_o[other:other + C_HEAD_DIM, :] = zero_half
            qt_o[base + LANES + FOX_AUG:base + FOX_KD, :] = zero_tail
            vt_o[h * FOX_VROWS:h * FOX_VROWS + C_HEAD_DIM, :] = vt[src, :]
            vt_o[h * FOX_VROWS + C_HEAD_DIM:(h + 1) * FOX_VROWS, :] = ones
        acc = jnp.dot(xb, w_ref[:, D_MODEL + j * ch:D_MODEL + (j + 1) * ch], preferred_element_type=F32)
        kn = headnorm(acc, kg_ref[...]).astype(BF16)
        for i in range(ch // LANES):
            p = j * (ch // LANES) + i
            k_o[:, p * FOX_KD:p * FOX_KD + LANES] = kn[:, i * LANES:(i + 1) * LANES]
        acc = jnp.dot(xb, w_ref[:, 3 * D_MODEL + j * ch:3 * D_MODEL + (j + 1) * ch],
                      preferred_element_type=F32)
        g_o[:, j * ch:(j + 1) * ch] = acc.astype(g_o.dtype)

    fl = jnp.dot(xb, w_ref[:, 4 * D_MODEL:4 * D_MODEL + LANES], preferred_element_type=F32) + bf_ref[...]
    logf = jnp.minimum(fl, 0.0) - jnp.log(1.0 + jnp.exp(-jnp.abs(fl)))
    c = _sum3(jnp.dot(tri_ref[...], _split3(logf), preferred_element_type=F32)) + carry_ref[...]
    carry_ref[...] = c[tm - 1:tm, :]
    c3 = _split3(c * LOG2E)
    aug_k = (jnp.dot(c3, selk_ref[...], preferred_element_type=F32) + onek_ref[...]).astype(BF16)
    for p in range(C_HEADS // 2):
        k_o[:, p * FOX_KD + LANES:(p + 1) * FOX_KD] = aug_k[:, p * LANES:(p + 1) * LANES]
    aug_q = lax.dot_general(selq_ref[...], c3, (((1,), (1,)), ((), ())), preferred_element_type=F32)
    aug_q = (aug_q + jnp.concatenate([oneq_ref[...]] * (tm // LANES), axis=1)).astype(BF16)
    for h in range(C_HEADS):
        qt_o[h * FOX_KD + LANES:h * FOX_KD + LANES + FOX_AUG, :] = aug_q[h * FOX_AUG:(h + 1) * FOX_AUG, :]


def _odd_proj(x2, w_bf, bf_pad, qg, kg, gmat, tri, B, S):
    T = B * S
    tm = PROJ_TM
    nt = S // tm
    tok = lambda n: pl.BlockSpec((tm, n), lambda b, t: (b * nt + t, 0))
    tspec = lambda rows: pl.BlockSpec((None, rows, tm), lambda b, t: (b, 0, t))
    selk, onek, selq, oneq = _fox_selectors()
    consts = [jnp.asarray(selk, BF16), jnp.asarray(onek, F32), jnp.asarray(selq, BF16),
              jnp.asarray(oneq, F32)]
    k_width = (C_HEADS // 2) * FOX_KD
    return pl.pallas_call(
        _odd_proj_kernel,
        grid=(B, nt),
        in_specs=[tok(D_MODEL), _full_spec(w_bf.shape), _full_spec(bf_pad.shape), _full_spec(qg.shape),
                  _full_spec(kg.shape), _full_spec(gmat.shape), _full_spec(tri.shape)]
                 + [_full_spec(c.shape) for c in consts],
        out_specs=[tspec(C_HEADS * FOX_KD), tok(k_width), tspec(C_HEADS * FOX_VROWS), tok(D_MODEL)],
        out_shape=[jax.ShapeDtypeStruct((B, C_HEADS * FOX_KD, S), BF16),
                   jax.ShapeDtypeStruct((T, k_width), BF16),
                   jax.ShapeDtypeStruct((B, C_HEADS * FOX_VROWS, S), BF16),
                   jax.ShapeDtypeStruct((T, D_MODEL), BF16)],
        scratch_shapes=[pltpu.VMEM((1, LANES), F32)],
        compiler_params=_cparams(("parallel", "arbitrary")),
        name="odd_proj",
    )(x2, w_bf, bf_pad, qg, kg, gmat, tri, *consts)


def _fox_kernel(qi_ref, ki_ref, q_ref, k_ref, v_ref, o_ref, ma, aa, mb, ab):
    step = pl.program_id(2)
    qi = qi_ref[step]
    ki = ki_ref[step]

    @pl.when(ki == 0)
    def _():
        _init_stats(ma, aa)
        _init_stats(mb, ab)

    def run(mask_fn):
        _attend(k_ref, v_ref, q_ref, [(0, ma, aa, 0), (FOX_KD, mb, ab, FOX_VROWS)], mask_fn)

    def mask_for(key_offset):
        def chunk_mask(k0, q0):
            k0 = k0 + key_offset
            if k0 + ATT_KC <= q0 + 1:
                return "all"
            if k0 >= q0 + ATT_QC:
                return "none"
            kk = lax.broadcasted_iota(jnp.int32, (ATT_KC, ATT_QC), 0) + k0
            qq = lax.broadcasted_iota(jnp.int32, (ATT_KC, ATT_QC), 1) + q0
            return kk <= qq
        return chunk_mask

    def finalize():
        d = C_HEAD_DIM
        o = jnp.concatenate([aa[:d, :] / aa[d:d + 1, :], ab[:d, :] / ab[d:d + 1, :]], axis=0)
        o_ref[...] = o.T.astype(o_ref.dtype)

    _attend_tile(qi, ki, run, mask_for, finalize)


def _fox_attention(qt, k, vt, B, S):
    tq, tk = ATT_TQ, ATT_TK
    nq, nk = S // tq, S // tk
    qi, ki = _tri_schedule(nq)
    n_pairs = C_HEADS // 2
    stat = pltpu.VMEM((1, tq), F32)
    acc = pltpu.VMEM((FOX_VROWS, tq), F32)
    grid_spec = pltpu.PrefetchScalarGridSpec(
        num_scalar_prefetch=2,
        grid=(B, n_pairs, int(qi.shape[0])),
        in_specs=[
            pl.BlockSpec((None, 2 * FOX_KD, tq), lambda b, h, s, qi, ki: (b, h, qi[s])),
            pl.BlockSpec((tk, FOX_KD), lambda b, h, s, qi, ki: (b * nk + ki[s], h)),
            pl.BlockSpec((None, 2 * FOX_VROWS, tk), lambda b, h, s, qi, ki: (b, h, ki[s])),
        ],
        out_specs=pl.BlockSpec((tq, LANES), lambda b, h, s, qi, ki: (b * nq + qi[s], h)),
        scratch_shapes=[stat, acc, stat, acc],
    )
    return pl.pallas_call(
        _fox_kernel,
        grid_spec=grid_spec,
        out_shape=jax.ShapeDtypeStruct((B * S, D_MODEL), BF16),
        compiler_params=_cparams(("parallel", "parallel", "arbitrary")),
        name="fox_attn",
    )(qi, ki, qt, k, vt)


def _even_out_kernel(oa_ref, ob_ref, x_ref, w_ref, g_ref, b_ref, y_ref):
    h = jnp.dot(oa_ref[...], w_ref[:A_WIDTH, :], preferred_element_type=F32)
    h = h + jnp.dot(ob_ref[...], w_ref[A_WIDTH:, :], preferred_element_type=F32)
    y_ref[...] = _layer_norm_rows(ALPHA * x_ref[...] + h, g_ref[...], b_ref[...])


def _odd_out_kernel(o_ref, gate_ref, x_ref, w_ref, g_ref, b_ref, y_ref):
    o = o_ref[...].astype(F32) * jax.nn.sigmoid(gate_ref[...].astype(F32))
    h = jnp.dot(o.astype(BF16), w_ref[...], preferred_element_type=F32)
    y_ref[...] = _layer_norm_rows(ALPHA * x_ref[...] + h, g_ref[...], b_ref[...])


def _out_proj(kernel, acts, x2, w_bf, g, b, name):
    T = x2.shape[0]
    tm = PROJ_TM
    tok = lambda n: pl.BlockSpec((tm, n), lambda i: (i, 0))
    return pl.pallas_call(
        kernel,
        grid=(T // tm,),
        in_specs=[tok(a.shape[1]) for a in acts] + [tok(D_MODEL), _full_spec(w_bf.shape),
                                                    _full_spec(g.shape), _full_spec(b.shape)],
        out_specs=tok(D_MODEL),
        out_shape=jax.ShapeDtypeStruct((T, D_MODEL), F32),
        compiler_params=_cparams(("parallel",)),
        name=name,
    )(*acts, x2, w_bf, g, b)


def _ffn_kernel(x_ref, w1_ref, w2_ref, g_ref, b_ref, y_ref, act_ref):
    x = x_ref[...]
    xb = x.astype(BF16)
    ch = MXU_DIM
    for j in range(D_FF // ch):
        gate = jnp.dot(xb, w1_ref[:, j * ch:(j + 1) * ch], preferred_element_type=F32)
        up = jnp.dot(xb, w1_ref[:, D_FF + j * ch:D_FF + (j + 1) * ch], preferred_element_type=F32)
        act_ref[:, j * ch:(j + 1) * ch] = (gate * jax.nn.sigmoid(gate) * up).astype(BF16)
    h = jnp.dot(act_ref[...], w2_ref[...], preferred_element_type=F32)
    y_ref[...] = _layer_norm_rows(ALPHA * x + h, g_ref[...], b_ref[...])


def _ffn(x2, w1_bf, w2_bf, g, b):
    T = x2.shape[0]
    tm = FFN_TM
    tok = pl.BlockSpec((tm, D_MODEL), lambda i: (i, 0))
    return pl.pallas_call(
        _ffn_kernel,
        grid=(T // tm,),
        in_specs=[tok, _full_spec(w1_bf.shape), _full_spec(w2_bf.shape),
                  _full_spec(g.shape), _full_spec(b.shape)],
        out_specs=tok,
        out_shape=jax.ShapeDtypeStruct((T, D_MODEL), F32),
        scratch_shapes=[pltpu.VMEM((tm, D_FF), BF16)],
        compiler_params=_cparams(("parallel",)),
        name="ffn",
    )(x2, w1_bf, w2_bf, g, b)


def kernel(x, even_w_in, even_w_out, hgrn_lb_logits, diff_lq1, diff_lk1, diff_lq2, diff_lk2,
           hgrn_norm_g, diff_norm_g, fox_w_in, fox_w_out, fox_b_f, fox_qnorm_g, fox_knorm_g,
           ffn_w1, ffn_w2, ln1_g, ln1_b, ln2_g, ln2_b):
    B, S, D = x.shape
    assert D == D_MODEL and S % ATT_TQ == 0 and S % PROJ_TM == 0 and S % HGRN_L == 0
    T = B * S
    x2 = x.reshape(T, D).astype(F32)

    cmat = jnp.asarray(_hgrn_cumsum_matrix(HGRN_L), BF16)
    tri = jnp.asarray(np.tril(np.ones((PROJ_TM, PROJ_TM), np.float32)), BF16)
    head_of = np.arange(MXU_DIM) // C_HEAD_DIM
    gmat = jnp.asarray((head_of[:, None] == head_of[None, :]).astype(np.float32) / C_HEAD_DIM, BF16)
    row = lambda v: v.astype(F32).reshape(1, -1)

    for l in range(DEPTH):
        j = l // 2
        if l % 2 == 0:
            aq, af, ai, ag, bq, bk, bv = _even_proj(x2, even_w_in[j].astype(BF16), B, S)
            o_a = _hgrn(aq, af, ai, ag, hgrn_lb_logits.astype(F32), row(hgrn_norm_g[j]), cmat, j, B, S)
            lamv = jnp.zeros((8, B_HEAD_DIM), F32).at[0:4].set(
                jnp.stack([diff_lq1[j], diff_lk1[j], diff_lq2[j], diff_lk2[j]]).astype(F32))
            lam_init = 0.8 - 0.6 * math.exp(-0.3 * l)
            o_b = _diff_attention(bq, bk, bv, lamv, row(diff_norm_g[j]), lam_init, B, S)
            x2 = _out_proj(_even_out_kernel, [o_a, o_b], x2, even_w_out[j].astype(BF16),
                           row(ln1_g[l]), row(ln1_b[l]), "even_out")
        else:
            w_pad = jnp.pad(fox_w_in[j], ((0, 0), (0, LANES - C_HEADS))).astype(BF16)
            bf_pad = jnp.pad(fox_b_f[j].astype(F32), (0, LANES - C_HEADS)).reshape(1, LANES)
            tile4 = lambda v: jnp.tile(v.astype(F32), MXU_DIM // C_HEAD_DIM).reshape(1, MXU_DIM)
            qt, k, vt, g = _odd_proj(x2, w_pad, bf_pad, tile4(fox_qnorm_g[j]),
                                     tile4(fox_knorm_g[j]), gmat, tri, B, S)
            o = _fox_attention(qt, k, vt, B, S)
            x2 = _out_proj(_odd_out_kernel, [o, g], x2, fox_w_out[j].astype(BF16),
                           row(ln1_g[l]), row(ln1_b[l]), "odd_out")
        x2 = _ffn(x2, ffn_w1[l].astype(BF16), ffn_w2[l].astype(BF16), row(ln2_g[l]), row(ln2_b[l]))
    return x2.reshape(B, S, D).astype(x.dtype)
```

```python
import functools
import math

import numpy as np
import jax
import jax.numpy as jnp
from jax import lax
from jax.experimental import pallas as pl
from jax.experimental.pallas import tpu as pltpu

F32 = jnp.float32
BF16 = jnp.bfloat16

D_MODEL = 1024
DEPTH = 4
A_WIDTH = 512
A_HEADS = 4
A_DK = 128
B_WIDTH = 512
B_HEADS = 4
B_HEAD_DIM = 64
DIFF_CHUNK = 64
C_HEADS = 16
C_HEAD_DIM = 64
D_FF = 2816
ALPHA = (2 * DEPTH) ** 0.25
MASK_VALUE = -1e30

LANES = 128
MXU_DIM = 256

PROJ_TM = 512
FFN_TM = 512
ATT_TQ = 2048
ATT_TK = 2048
ATT_QC = 256
ATT_KC = 512
ATT_SUB = 64
ATT_AHEAD = 2
ATT_BEHIND = 2
VT_ROWS = LANES + 16
HGRN_L = 256
LOG2E = math.log2(math.e)
VMEM_LIMIT = 56 * 1024 * 1024


def _cparams(sem):
    return pltpu.CompilerParams(dimension_semantics=sem, vmem_limit_bytes=VMEM_LIMIT)


def _full_spec(shape):
    nd = len(shape)
    return pl.BlockSpec(shape, lambda *_: (0,) * nd)


def _split3(x):
    h = x.astype(BF16)
    r = x - h.astype(F32)
    m = r.astype(BF16)
    l = (r - m.astype(F32)).astype(BF16)
    return jnp.concatenate([h, m, l], axis=1)


def _sum3(y):
    return y[:, :LANES] + y[:, LANES:2 * LANES] + y[:, 2 * LANES:]


def _layer_norm_rows(y, g, b):
    mu = jnp.mean(y, axis=-1, keepdims=True)
    d = y - mu
    var = jnp.mean(d * d, axis=-1, keepdims=True)
    return d * lax.rsqrt(var + 1e-5) * g + b


def _even_proj_kernel(x_ref, w_ref, aq, af, ai, ag, bq, bk, bv):
    xb = x_ref[...].astype(BF16)
    outs = (aq, af, ai, ag, bq, bk, bv)
    for n, o in enumerate(outs):
        acc = jnp.dot(xb, w_ref[:, n * 512:(n + 1) * 512], preferred_element_type=F32)
        if o is bq:
            qt = (acc * (B_HEAD_DIM ** -0.5 * LOG2E)).T.astype(o.dtype)
            zero = jnp.zeros((B_HEAD_DIM, qt.shape[1]), o.dtype)
            for h in range(B_HEADS):
                r = 2 * h * LANES
                o[r:r + B_HEAD_DIM, :] = qt[h * LANES:h * LANES + B_HEAD_DIM, :]
                o[r + B_HEAD_DIM:r + LANES, :] = zero
                o[r + LANES:r + LANES + B_HEAD_DIM, :] = zero
                o[r + LANES + B_HEAD_DIM:r + 2 * LANES, :] = qt[h * LANES + B_HEAD_DIM:(h + 1) * LANES, :]
        elif o is bv:
            vt = acc.T.astype(o.dtype)
            ones = jnp.ones((VT_ROWS - LANES, vt.shape[1]), o.dtype)
            for h in range(B_HEADS):
                o[h * VT_ROWS:h * VT_ROWS + LANES, :] = vt[h * LANES:(h + 1) * LANES, :]
                o[h * VT_ROWS + LANES:(h + 1) * VT_ROWS, :] = ones
        else:
            o[...] = acc.astype(o.dtype)


def _even_proj(x2, w_bf, B, S):
    T = x2.shape[0]
    tm = PROJ_TM
    nt = S // tm
    tok = lambda dt: jax.ShapeDtypeStruct((T, 512), dt)
    spec = pl.BlockSpec((tm, 512), lambda i: (i, 0))
    tspec = lambda rows: pl.BlockSpec((None, rows, tm), lambda i: (i // nt, 0, i % nt))
    return pl.pallas_call(
        _even_proj_kernel,
        grid=(T // tm,),
        in_specs=[pl.BlockSpec((tm, D_MODEL), lambda i: (i, 0)), _full_spec(w_bf.shape)],
        out_specs=[spec, spec, spec, spec, tspec(2 * B_WIDTH), spec, tspec(B_HEADS * VT_ROWS)],
        out_shape=[tok(BF16), tok(F32), tok(BF16), tok(BF16),
                   jax.ShapeDtypeStruct((B, 2 * B_WIDTH, S), BF16), tok(BF16),
                   jax.ShapeDtypeStruct((B, B_HEADS * VT_ROWS, S), BF16)],
        compiler_params=_cparams(("parallel",)),
        name="even_proj",
    )(x2, w_bf)


def _hgrn_cumsum_matrix(L):
    idx = np.arange(L)
    mats = []
    c = 1
    while c < L:
        start = (idx // c) * c
        end = start + c - 1
        right = ((idx // c) % 2) == 1
        u = idx[None, :]
        m_right = (u >= start[:, None]) & (u <= idx[:, None])
        m_left = (u > idx[:, None]) & (u <= end[:, None])
        mats.append(np.where(right[:, None], m_right, m_left))
        c *= 2
    u = idx[None, :]
    mats.append(u <= idx[:, None])
    mats.append(u > idx[:, None])
    return np.concatenate(mats, axis=0).astype(np.float32)


def _split2(x):
    h = x.astype(BF16)
    m = (x - h.astype(F32)).astype(BF16)
    return jnp.concatenate([h, m], axis=1)


def _hgrn_kernel(q_ref, f_ref, i_ref, g_ref, lbl_ref, gn_ref, cmat_ref, o_ref, state_ref, *, layer_j):
    L = HGRN_L
    n_levels = int(math.log2(L))
    nt = (((1,), (1,)), ((), ()))

    @pl.when(pl.program_id(1) == 0)
    def _():
        state_ref[...] = jnp.zeros_like(state_ref)

    lbl = lbl_ref[...]
    e = jnp.exp(lbl - jnp.max(lbl, axis=0, keepdims=True))
    soft = e / jnp.sum(e, axis=0, keepdims=True)
    lb_all = jnp.sum(soft[:layer_j + 1], axis=0, keepdims=True) - soft[0:1]

    row = lax.broadcasted_iota(jnp.int32, (L, 1), 0)
    xor = lax.broadcasted_iota(jnp.int32, (L, L), 0) ^ lax.broadcasted_iota(jnp.int32, (L, L), 1)
    cmat = cmat_ref[...]

    heads = range(A_HEADS)
    cols = [slice(h * LANES, (h + 1) * LANES) for h in heads]
    q, k, gdec = [], [], []
    for h in heads:
        lb = lb_all[:, cols[h]]
        f = lb + (1.0 - lb) * jax.nn.sigmoid(f_ref[:, cols[h]])
        k.append(1.0 - f)
        qr = q_ref[:, cols[h]].astype(F32)
        q.append(qr * jax.nn.sigmoid(qr))
        ex = jnp.dot(cmat, _split2(jnp.log(f)), preferred_element_type=F32)
        gdec.append(jnp.exp(ex[:, :LANES] + ex[:, LANES:]))

    scores = []
    for h in heads:
        s = jnp.where(xor == 0, lax.dot_general(q[h].astype(BF16), k[h].astype(BF16), nt,
                                                preferred_element_type=F32), 0.0)
        for lvl in range(n_levels):
            g_l = gdec[h][lvl * L:(lvl + 1) * L, :]
            right = ((row >> lvl) & 1).astype(F32)
            ql = (q[h] * g_l * right).astype(BF16)
            kl = (k[h] * g_l * (1.0 - right)).astype(BF16)
            s_l = lax.dot_general(ql, kl, nt, preferred_element_type=F32)
            s = s + jnp.where(xor < (2 << lvl), s_l, 0.0)
        scores.append(s.astype(BF16))

    for h in heads:
        iv = i_ref[:, cols[h]]
        g_full = gdec[h][n_levels * L:(n_levels + 1) * L, :]
        g_rest = gdec[h][(n_levels + 1) * L:(n_levels + 2) * L, :]
        state_t = state_ref[h]
        o = lax.dot_general((q[h] * g_full).astype(BF16), state_t.astype(BF16), nt,
                            preferred_element_type=F32)
        o = o + jnp.dot(scores[h], iv, preferred_element_type=F32)
        k_out = (k[h] * g_rest).astype(BF16)
        upd = jnp.dot(iv.astype(F32).T.astype(BF16), k_out, preferred_element_type=F32)
        state_ref[h] = state_t * g_full[L - 1:L, :] + upd
        ms = jnp.mean(o * o, axis=-1, keepdims=True)
        gate = g_ref[:, cols[h]].astype(F32)
        o = o * lax.rsqrt(ms + 1e-6) * gn_ref[...] * (gate * jax.nn.sigmoid(gate))
        o_ref[:, cols[h]] = o.astype(o_ref.dtype)


def _hgrn(aq, af, ai, ag, lb_logits, gn, cmat, layer_j, B, S):
    L = HGRN_L
    nb = S // L
    blk = pl.BlockSpec((L, A_WIDTH), lambda b, t: (b * nb + t, 0))
    return pl.pallas_call(
        functools.partial(_hgrn_kernel, layer_j=layer_j),
        grid=(B, nb),
        in_specs=[blk, blk, blk, blk, _full_spec(lb_logits.shape),
                  _full_spec(gn.shape), _full_spec(cmat.shape)],
        out_specs=blk,
        out_shape=jax.ShapeDtypeStruct((B * S, A_WIDTH), BF16),
        scratch_shapes=[pltpu.VMEM((A_HEADS, LANES, LANES), F32)],
        compiler_params=_cparams(("parallel", "arbitrary")),
        name="hgrn2",
    )(aq, af, ai, ag, lb_logits, gn, cmat)


def _attend(k_ref, vt_ref, qt_ref, streams, mask_fn):
    tk, kd = k_ref.shape
    tq = qt_ref.shape[1]
    klen = ATT_KC
    chains = []
    for kc in range(0, tk, klen):
        for st in streams:
            for qc in range(0, tq, ATT_QC):
                mask = "all" if mask_fn is None else mask_fn(kc, qc, klen)
                if not (isinstance(mask, str) and mask == "none"):
                    chains.append((st, qc, kc, None if isinstance(mask, str) else mask))

    def scores(st, qc, kc, mask):
        s = jnp.dot(k_ref[kc:kc + klen, :], qt_ref[st[0]:st[0] + kd, qc:qc + ATT_QC],
                    preferred_element_type=F32)
        if mask is not None:
            s = jnp.where(mask, s, MASK_VALUE)
        return s

    def softmax(st, qc, s):
        m_ref = st[1]
        cols = slice(qc, qc + ATT_QC)
        m_prev = m_ref[:, cols]
        m_run, parts, maxes = m_prev, [], []
        for r0 in range(0, s.shape[0], ATT_SUB):
            s_g = s[r0:r0 + ATT_SUB, :]
            m_run = jnp.maximum(m_run, jnp.max(s_g, axis=0, keepdims=True))
            parts.append(jnp.exp2((s_g - m_run).astype(BF16)))
            maxes.append(m_run)
        m_new = m_run
        parts = [p_g if m_g is m_new else p_g * jnp.exp2(m_g - m_new).astype(BF16)
                 for p_g, m_g in zip(parts, maxes)]
        p = parts[0] if len(parts) == 1 else jnp.concatenate(parts, axis=0)
        alpha = jnp.exp2(m_prev - m_new)
        m_ref[:, cols] = m_new
        return p, alpha

    def weighted_values(st, qc, kc, p, alpha):
        acc_ref, v0 = st[2], st[3]
        cols = slice(qc, qc + ATT_QC)
        acc_ref[:, cols] = alpha * acc_ref[:, cols] + jnp.dot(
            vt_ref[v0:v0 + acc_ref.shape[0], kc:kc + klen], p, preferred_element_type=F32)

    n = len(chains)
    ready = [scores(*c) for c in chains[:ATT_AHEAD]]
    pending = []
    for i, (st, qc, kc, _) in enumerate(chains):
        s = ready.pop(0)
        if i + ATT_AHEAD < n:
            ready.append(scores(*chains[i + ATT_AHEAD]))
        pending.append((st, qc, kc, *softmax(st, qc, s)))
        if len(pending) > ATT_BEHIND:
            weighted_values(*pending.pop(0))
    for item in pending:
        weighted_values(*item)


ATT_MULT = ATT_TQ // ATT_TK


def _tri_schedule(nq):
    qi = [i for i in range(nq) for _ in range((i + 1) * ATT_MULT)]
    ki = [j for i in range(nq) for j in range((i + 1) * ATT_MULT)]
    return jnp.asarray(qi, jnp.int32), jnp.asarray(ki, jnp.int32)


def _attend_tile(qi, ki, run, mask_for, finalize):
    rel = ki - qi * ATT_MULT

    @pl.when(rel < 0)
    def _():
        run(None)

    for r in range(ATT_MULT):
        @pl.when(rel == r)
        def _(r=r):
            run(mask_for(r * ATT_TK))
            if r == ATT_MULT - 1:
                finalize()


def _init_stats(m_ref, acc_ref):
    m_ref[...] = jnp.full_like(m_ref, MASK_VALUE)
    acc_ref[...] = jnp.zeros_like(acc_ref)


def _diff_kernel(qi_ref, ki_ref, q_ref, k_ref, v_ref, lam_ref, gn_ref, o_ref,
                 m1, a1, m2, a2, *, lam_init):
    step = pl.program_id(2)
    qi = qi_ref[step]
    ki = ki_ref[step]

    @pl.when(ki == 0)
    def _():
        _init_stats(m1, a1)
        _init_stats(m2, a2)

    def run(mask_fn):
        _attend(k_ref, v_ref, q_ref, [(0, m1, a1, 0), (LANES, m2, a2, 0)], mask_fn)

    def mask_for(key_offset):
        def chunk_mask(k0, q0, klen):
            k0 = k0 + key_offset
            if k0 + klen <= q0 + DIFF_CHUNK:
                return "all"
            if k0 >= q0 + ATT_QC:
                return "none"
            sh = DIFF_CHUNK.bit_length() - 1
            kk = (lax.broadcasted_iota(jnp.int32, (klen, ATT_QC), 0) + k0) >> sh
            qq = (lax.broadcasted_iota(jnp.int32, (klen, ATT_QC), 1) + q0) >> sh
            return kk <= qq
        return chunk_mask

    def finalize():
        lv = lam_ref[...]
        lam = (jnp.exp(jnp.sum(lv[0:1] * lv[1:2], axis=-1, keepdims=True))
               - jnp.exp(jnp.sum(lv[2:3] * lv[3:4], axis=-1, keepdims=True)) + lam_init)
        o = (a1[:LANES, :] / a1[LANES:LANES + 1, :]
             - lam * (a2[:LANES, :] / a2[LANES:LANES + 1, :])).T
        ms = jnp.mean(o * o, axis=-1, keepdims=True)
        o = o * lax.rsqrt(ms + 1e-6) * gn_ref[...] * (1.0 - lam_init)
        o_ref[...] = o.astype(o_ref.dtype)

    _attend_tile(qi, ki, run, mask_for, finalize)


def _diff_attention(bq, bk, bv, lamv, gn, lam_init, B, S):
    tq, tk = ATT_TQ, ATT_TK
    nq, nk = S // tq, S // tk
    qi, ki = _tri_schedule(nq)
    stat = pltpu.VMEM((1, tq), F32)
    acc = pltpu.VMEM((VT_ROWS, tq), F32)
    grid_spec = pltpu.PrefetchScalarGridSpec(
        num_scalar_prefetch=2,
        grid=(B, B_HEADS, int(qi.shape[0])),
        in_specs=[
            pl.BlockSpec((None, 2 * LANES, tq), lambda b, h, s, qi, ki: (b, h, qi[s])),
            pl.BlockSpec((tk, LANES), lambda b, h, s, qi, ki: (b * nk + ki[s], h)),
            pl.BlockSpec((None, VT_ROWS, tk), lambda b, h, s, qi, ki: (b, h, ki[s])),
            pl.BlockSpec(lamv.shape, lambda b, h, s, qi, ki: (0, 0)),
            pl.BlockSpec(gn.shape, lambda b, h, s, qi, ki: (0, 0)),
        ],
        out_specs=pl.BlockSpec((tq, LANES), lambda b, h, s, qi, ki: (b * nq + qi[s], h)),
        scratch_shapes=[stat, acc, stat, acc],
    )
    return pl.pallas_call(
        functools.partial(_diff_kernel, lam_init=lam_init),
        grid_spec=grid_spec,
        out_shape=jax.ShapeDtypeStruct((B * S, B_WIDTH), BF16),
        compiler_params=_cparams(("parallel", "parallel", "arbitrary")),
        name="diff_attn",
    )(qi, ki, bq, bk, bv, lamv, gn)


FOX_KD = 2 * LANES
FOX_AUG = 16
FOX_VROWS = C_HEAD_DIM + 16


def _fox_selectors():
    n_pairs = C_HEADS // 2
    selk = np.zeros((3 * LANES, n_pairs * LANES), np.float32)
    onek = np.zeros((1, n_pairs * LANES), np.float32)
    selq = np.zeros((C_HEADS * FOX_AUG, 3 * LANES), np.float32)
    oneq = np.zeros((C_HEADS * FOX_AUG, LANES), np.float32)
    for h in range(C_HEADS):
        p, odd = divmod(h, 2)
        for piece in range(3):
            selk[piece * LANES + h, p * LANES + 6 * odd + 3 + piece] = -1.0
            selq[h * FOX_AUG + 6 * odd + piece, piece * LANES + h] = 1.0
            onek[0, p * LANES + 6 * odd + piece] = 1.0
            oneq[h * FOX_AUG + 6 * odd + 3 + piece, :] = 1.0
    return selk, onek, selq, oneq


def _odd_proj_kernel(x_ref, w_ref, bf_ref, qg_ref, kg_ref, gmat_ref, tri_ref,
                     selk_ref, onek_ref, selq_ref, oneq_ref,
                     qt_o, k_o, vt_o, g_o, carry_ref):
    @pl.when(pl.program_id(1) == 0)
    def _():
        carry_ref[...] = jnp.zeros_like(carry_ref)

    xb = x_ref[...].astype(BF16)
    tm = xb.shape[0]
    ch = MXU_DIM
    hpc = ch // C_HEAD_DIM
    gmat = gmat_ref[...]

    def headnorm(acc, g):
        ms = jnp.dot((acc * acc).astype(BF16), gmat, preferred_element_type=F32)
        return acc * lax.rsqrt(ms + 1e-6) * g

    zero_half = jnp.zeros((C_HEAD_DIM, tm), BF16)
    zero_tail = jnp.zeros((FOX_KD - LANES - FOX_AUG, tm), BF16)
    ones = jnp.ones((FOX_VROWS - C_HEAD_DIM, tm), BF16)
    for j in range(D_MODEL // ch):
        acc = jnp.dot(xb, w_ref[:, j * ch:(j + 1) * ch], preferred_element_type=F32)
        qt = (headnorm(acc, qg_ref[...]) * (C_HEAD_DIM ** -0.5 * LOG2E)).T.astype(BF16)
        acc = jnp.dot(xb, w_ref[:, 2 * D_MODEL + j * ch:2 * D_MODEL + (j + 1) * ch],
                      preferred_element_type=F32)
        vt = acc.T.astype(BF16)
        for i in range(hpc):
            h = j * hpc + i
            src = slice(i * C_HEAD_DIM, (i + 1) * C_HEAD_DIM)
            base = h * FOX_KD
            own, other = (base, base + C_HEAD_DIM) if h % 2 == 0 else (base + C_HEAD_DIM, base)
            qt_o[own:own + C_HEAD_DIM, :] = qt[src, :]
            qt_o[other:other + C_HEAD_DIM, :] = zero_half
            qt_o[base + LANES + FOX_AUG:base + FOX_KD, :] = zero_tail
            vt_o[h * FOX_VROWS:h * FOX_VROWS + C_HEAD_DIM, :] = vt[src, :]
            vt_o[h * FOX_VROWS + C_HEAD_DIM:(h + 1) * FOX_VROWS, :] = ones
        acc = jnp.dot(xb, w_ref[:, D_MODEL + j * ch:D_MODEL + (j + 1) * ch], preferred_element_type=F32)
        kn = headnorm(acc, kg_ref[...]).astype(BF16)
        for i in range(ch // LANES):
            p = j * (ch // LANES) + i
            k_o[:, p * FOX_KD:p * FOX_KD + LANES] = kn[:, i * LANES:(i + 1) * LANES]
        acc = jnp.dot(xb, w_ref[:, 3 * D_MODEL + j * ch:3 * D_MODEL + (j + 1) * ch],
                      preferred_element_type=F32)
        g_o[:, j * ch:(j + 1) * ch] = acc.astype(g_o.dtype)

    fl = jnp.dot(xb, w_ref[:, 4 * D_MODEL:4 * D_MODEL + LANES], preferred_element_type=F32) + bf_ref[...]
    logf = jnp.minimum(fl, 0.0) - jnp.log(1.0 + jnp.exp(-jnp.abs(fl)))
    c = _sum3(jnp.dot(tri_ref[...], _split3(logf), preferred_element_type=F32)) + carry_ref[...]
    carry_ref[...] = c[tm - 1:tm, :]
    c3 = _split3(c * LOG2E)
    aug_k = (jnp.dot(c3, selk_ref[...], preferred_element_type=F32) + onek_ref[...]).astype(BF16)
    for p in range(C_HEADS // 2):
        k_o[:, p * FOX_KD + LANES:(p + 1) * FOX_KD] = aug_k[:, p * LANES:(p + 1) * LANES]
    aug_q = lax.dot_general(selq_ref[...], c3, (((1,), (1,)), ((), ())), preferred_element_type=F32)
    aug_q = (aug_q + jnp.concatenate([oneq_ref[...]] * (tm // LANES), axis=1)).astype(BF16)
    for h in range(C_HEADS):
        qt_o[h * FOX_KD + LANES:h * FOX_KD + LANES + FOX_AUG, :] = aug_q[h * FOX_AUG:(h + 1) * FOX_AUG, :]


def _odd_proj(x2, w_bf, bf_pad, qg, kg, gmat, tri, B, S):
    T = B * S
    tm = PROJ_TM
    nt = S // tm
    tok = lambda n: pl.BlockSpec((tm, n), lambda b, t: (b * nt + t, 0))
    tspec = lambda rows: pl.BlockSpec((None, rows, tm), lambda b, t: (b, 0, t))
    selk, onek, selq, oneq = _fox_selectors()
    consts = [jnp.asarray(selk, BF16), jnp.asarray(onek, F32), jnp.asarray(selq, BF16),
              jnp.asarray(oneq, F32)]
    k_width = (C_HEADS // 2) * FOX_KD
    return pl.pallas_call(
        _odd_proj_kernel,
        grid=(B, nt),
        in_specs=[tok(D_MODEL), _full_spec(w_bf.shape), _full_spec(bf_pad.shape), _full_spec(qg.shape),
                  _full_spec(kg.shape), _full_spec(gmat.shape), _full_spec(tri.shape)]
                 + [_full_spec(c.shape) for c in consts],
        out_specs=[tspec(C_HEADS * FOX_KD), tok(k_width), tspec(C_HEADS * FOX_VROWS), tok(D_MODEL)],
        out_shape=[jax.ShapeDtypeStruct((B, C_HEADS * FOX_KD, S), BF16),
                   jax.ShapeDtypeStruct((T, k_width), BF16),
                   jax.ShapeDtypeStruct((B, C_HEADS * FOX_VROWS, S), BF16),
                   jax.ShapeDtypeStruct((T, D_MODEL), BF16)],
        scratch_shapes=[pltpu.VMEM((1, LANES), F32)],
        compiler_params=_cparams(("parallel", "arbitrary")),
        name="odd_proj",
    )(x2, w_bf, bf_pad, qg, kg, gmat, tri, *consts)


def _fox_kernel(qi_ref, ki_ref, q_ref, k_ref, v_ref, o_ref, ma, aa, mb, ab):
    step = pl.program_id(2)
    qi = qi_ref[step]
    ki = ki_ref[step]

    @pl.when(ki == 0)
    def _():
        _init_stats(ma, aa)
        _init_stats(mb, ab)

    def run(mask_fn):
        _attend(k_ref, v_ref, q_ref, [(0, ma, aa, 0), (FOX_KD, mb, ab, FOX_VROWS)], mask_fn)

    def mask_for(key_offset):
        def chunk_mask(k0, q0, klen):
            k0 = k0 + key_offset
            if k0 + klen <= q0 + 1:
                return "all"
            if k0 >= q0 + ATT_QC:
                return "none"
            kk = lax.broadcasted_iota(jnp.int32, (klen, ATT_QC), 0) + k0
            qq = lax.broadcasted_iota(jnp.int32, (klen, ATT_QC), 1) + q0
            return kk <= qq
        return chunk_mask

    def finalize():
        d = C_HEAD_DIM
        o = jnp.concatenate([aa[:d, :] / aa[d:d + 1, :], ab[:d, :] / ab[d:d + 1, :]], axis=0)
        o_ref[...] = o.T.astype(o_ref.dtype)

    _attend_tile(qi, ki, run, mask_for, finalize)


def _fox_attention(qt, k, vt, B, S):
    tq, tk = ATT_TQ, ATT_TK
    nq, nk = S // tq, S // tk
    qi, ki = _tri_schedule(nq)
    n_pairs = C_HEADS // 2
    stat = pltpu.VMEM((1, tq), F32)
    acc = pltpu.VMEM((FOX_VROWS, tq), F32)
    grid_spec = pltpu.PrefetchScalarGridSpec(
        num_scalar_prefetch=2,
        grid=(B, n_pairs, int(qi.shape[0])),
        in_specs=[
            pl.BlockSpec((None, 2 * FOX_KD, tq), lambda b, h, s, qi, ki: (b, h, qi[s])),
            pl.BlockSpec((tk, FOX_KD), lambda b, h, s, qi, ki: (b * nk + ki[s], h)),
            pl.BlockSpec((None, 2 * FOX_VROWS, tk), lambda b, h, s, qi, ki: (b, h, ki[s])),
        ],
        out_specs=pl.BlockSpec((tq, LANES), lambda b, h, s, qi, ki: (b * nq + qi[s], h)),
        scratch_shapes=[stat, acc, stat, acc],
    )
    return pl.pallas_call(
        _fox_kernel,
        grid_spec=grid_spec,
        out_shape=jax.ShapeDtypeStruct((B * S, D_MODEL), BF16),
        compiler_params=_cparams(("parallel", "parallel", "arbitrary")),
        name="fox_attn",
    )(qi, ki, qt, k, vt)


def _even_mix(oa_ref, ob_ref, w_ref):
    h = jnp.dot(oa_ref[...], w_ref[:A_WIDTH, :], preferred_element_type=F32)
    return h + jnp.dot(ob_ref[...], w_ref[A_WIDTH:, :], preferred_element_type=F32)


def _odd_mix(o_ref, gate_ref, w_ref):
    o = o_ref[...].astype(F32) * jax.nn.sigmoid(gate_ref[...].astype(F32))
    return jnp.dot(o.astype(BF16), w_ref[...], preferred_element_type=F32)


def _tail_kernel(a0_ref, a1_ref, x_ref, wo_ref, g1_ref, b1_ref, w1_ref, w2_ref, g2_ref, b2_ref,
                 y_ref, act_ref, *, mix):
    x = _layer_norm_rows(ALPHA * x_ref[...] + mix(a0_ref, a1_ref, wo_ref), g1_ref[...], b1_ref[...])
    xb = x.astype(BF16)
    ch = MXU_DIM
    for j in range(D_FF // ch):
        gate = jnp.dot(xb, w1_ref[:, j * ch:(j + 1) * ch], preferred_element_type=F32)
        up = jnp.dot(xb, w1_ref[:, D_FF + j * ch:D_FF + (j + 1) * ch], preferred_element_type=F32)
        act_ref[:, j * ch:(j + 1) * ch] = (gate * jax.nn.sigmoid(gate) * up).astype(BF16)
    h = jnp.dot(act_ref[...], w2_ref[...], preferred_element_type=F32)
    y_ref[...] = _layer_norm_rows(ALPHA * x + h, g2_ref[...], b2_ref[...])


def _layer_tail(mix, acts, x2, wo_bf, g1, b1, w1_bf, w2_bf, g2, b2, name):
    T = x2.shape[0]
    tm = FFN_TM
    tok = lambda n: pl.BlockSpec((tm, n), lambda i: (i, 0))
    consts = [wo_bf, g1, b1, w1_bf, w2_bf, g2, b2]
    return pl.pallas_call(
        functools.partial(_tail_kernel, mix=mix),
        grid=(T // tm,),
        in_specs=[tok(a.shape[1]) for a in acts] + [tok(D_MODEL)] + [_full_spec(c.shape) for c in consts],
        out_specs=tok(D_MODEL),
        out_shape=jax.ShapeDtypeStruct((T, D_MODEL), F32),
        scratch_shapes=[pltpu.VMEM((tm, D_FF), BF16)],
        compiler_params=_cparams(("parallel",)),
        name=name,
    )(*acts, x2, *consts)


def kernel(x, even_w_in, even_w_out, hgrn_lb_logits, diff_lq1, diff_lk1, diff_lq2, diff_lk2,
           hgrn_norm_g, diff_norm_g, fox_w_in, fox_w_out, fox_b_f, fox_qnorm_g, fox_knorm_g,
           ffn_w1, ffn_w2, ln1_g, ln1_b, ln2_g, ln2_b):
    B, S, D = x.shape
    assert D == D_MODEL and S % ATT_TQ == 0 and S % PROJ_TM == 0 and S % HGRN_L == 0
    T = B * S
    x2 = x.reshape(T, D).astype(F32)

    cmat = jnp.asarray(_hgrn_cumsum_matrix(HGRN_L), BF16)
    tri = jnp.asarray(np.tril(np.ones((PROJ_TM, PROJ_TM), np.float32)), BF16)
    head_of = np.arange(MXU_DIM) // C_HEAD_DIM
    gmat = jnp.asarray((head_of[:, None] == head_of[None, :]).astype(np.float32) / C_HEAD_DIM, BF16)
    row = lambda v: v.astype(F32).reshape(1, -1)

    for l in range(DEPTH):
        j = l // 2
        if l % 2 == 0:
            aq, af, ai, ag, bq, bk, bv = _even_proj(x2, even_w_in[j].astype(BF16), B, S)
            o_a = _hgrn(aq, af, ai, ag, hgrn_lb_logits.astype(F32), row(hgrn_norm_g[j]), cmat, j, B, S)
            lamv = jnp.zeros((8, B_HEAD_DIM), F32).at[0:4].set(
                jnp.stack([diff_lq1[j], diff_lk1[j], diff_lq2[j], diff_lk2[j]]).astype(F32))
            lam_init = 0.8 - 0.6 * math.exp(-0.3 * l)
            o_b = _diff_attention(bq, bk, bv, lamv, row(diff_norm_g[j]), lam_init, B, S)
            mix, acts, w_out, name = _even_mix, [o_a, o_b], even_w_out[j], "even_tail"
        else:
            w_pad = jnp.pad(fox_w_in[j], ((0, 0), (0, LANES - C_HEADS))).astype(BF16)
            bf_pad = jnp.pad(fox_b_f[j].astype(F32), (0, LANES - C_HEADS)).reshape(1, LANES)
            tile4 = lambda v: jnp.tile(v.astype(F32), MXU_DIM // C_HEAD_DIM).reshape(1, MXU_DIM)
            qt, k, vt, g = _odd_proj(x2, w_pad, bf_pad, tile4(fox_qnorm_g[j]),
                                     tile4(fox_knorm_g[j]), gmat, tri, B, S)
            o = _fox_attention(qt, k, vt, B, S)
            mix, acts, w_out, name = _odd_mix, [o, g], fox_w_out[j], "odd_tail"
        x2 = _layer_tail(mix, acts, x2, w_out.astype(BF16), row(ln1_g[l]), row(ln1_b[l]),
                         ffn_w1[l].astype(BF16), ffn_w2[l].astype(BF16), row(ln2_g[l]), row(ln2_b[l]), name)
    return x2.reshape(B, S, D).astype(x.dtype)
```

```python
import functools
import math

import numpy as np
import jax
import jax.numpy as jnp
from jax import lax
from jax.experimental import pallas as pl
from jax.experimental.pallas import tpu as pltpu

F32 = jnp.float32
BF16 = jnp.bfloat16

D_MODEL = 1024
DEPTH = 4
A_WIDTH = 512
A_HEADS = 4
A_DK = 128
B_WIDTH = 512
B_HEADS = 4
B_HEAD_DIM = 64
DIFF_CHUNK = 64
C_HEADS = 16
C_HEAD_DIM = 64
D_FF = 2816
ALPHA = (2 * DEPTH) ** 0.25
MASK_VALUE = -1e30

LANES = 128
MXU_DIM = 256

PROJ_TM = 512
FFN_TM = 512
ATT_TQ = 2048
ATT_TK = 2048
ATT_QC = 256
ATT_KC = 512
ATT_SUB = 32
ATT_AHEAD = 2
ATT_BEHIND = 2
STAT_ROWS = 16
VT_ROWS = LANES + 16
HGRN_L = 256
LOG2E = math.log2(math.e)
VMEM_LIMIT = 56 * 1024 * 1024


def _cparams(sem):
    return pltpu.CompilerParams(dimension_semantics=sem, vmem_limit_bytes=VMEM_LIMIT)


def _full_spec(shape):
    nd = len(shape)
    return pl.BlockSpec(shape, lambda *_: (0,) * nd)


def _split3(x):
    h = x.astype(BF16)
    r = x - h.astype(F32)
    m = r.astype(BF16)
    l = (r - m.astype(F32)).astype(BF16)
    return jnp.concatenate([h, m, l], axis=1)


def _sum3(y):
    return y[:, :LANES] + y[:, LANES:2 * LANES] + y[:, 2 * LANES:]


def _layer_norm_rows(y, g, b):
    mu = jnp.mean(y, axis=-1, keepdims=True)
    d = y - mu
    var = jnp.mean(d * d, axis=-1, keepdims=True)
    return d * lax.rsqrt(var + 1e-5) * g + b


def _even_proj_kernel(x_ref, w_ref, aq, af, ai, ag, bq, bk, bv):
    xb = x_ref[...].astype(BF16)
    outs = (aq, af, ai, ag, bq, bk, bv)
    for n, o in enumerate(outs):
        acc = jnp.dot(xb, w_ref[:, n * 512:(n + 1) * 512], preferred_element_type=F32)
        if o is bq:
            qt = (acc * (B_HEAD_DIM ** -0.5 * LOG2E)).T.astype(o.dtype)
            zero = jnp.zeros((B_HEAD_DIM, qt.shape[1]), o.dtype)
            for h in range(B_HEADS):
                r = 2 * h * LANES
                o[r:r + B_HEAD_DIM, :] = qt[h * LANES:h * LANES + B_HEAD_DIM, :]
                o[r + B_HEAD_DIM:r + LANES, :] = zero
                o[r + LANES:r + LANES + B_HEAD_DIM, :] = zero
                o[r + LANES + B_HEAD_DIM:r + 2 * LANES, :] = qt[h * LANES + B_HEAD_DIM:(h + 1) * LANES, :]
        elif o is bv:
            vt = acc.T.astype(o.dtype)
            ones = jnp.ones((VT_ROWS - LANES, vt.shape[1]), o.dtype)
            for h in range(B_HEADS):
                o[h * VT_ROWS:h * VT_ROWS + LANES, :] = vt[h * LANES:(h + 1) * LANES, :]
                o[h * VT_ROWS + LANES:(h + 1) * VT_ROWS, :] = ones
        else:
            o[...] = acc.astype(o.dtype)


def _even_proj(x2, w_bf, B, S):
    T = x2.shape[0]
    tm = PROJ_TM
    nt = S // tm
    tok = lambda dt: jax.ShapeDtypeStruct((T, 512), dt)
    spec = pl.BlockSpec((tm, 512), lambda i: (i, 0))
    tspec = lambda rows: pl.BlockSpec((None, rows, tm), lambda i: (i // nt, 0, i % nt))
    return pl.pallas_call(
        _even_proj_kernel,
        grid=(T // tm,),
        in_specs=[pl.BlockSpec((tm, D_MODEL), lambda i: (i, 0)), _full_spec(w_bf.shape)],
        out_specs=[spec, spec, spec, spec, tspec(2 * B_WIDTH), spec, tspec(B_HEADS * VT_ROWS)],
        out_shape=[tok(BF16), tok(F32), tok(BF16), tok(BF16),
                   jax.ShapeDtypeStruct((B, 2 * B_WIDTH, S), BF16), tok(BF16),
                   jax.ShapeDtypeStruct((B, B_HEADS * VT_ROWS, S), BF16)],
        compiler_params=_cparams(("parallel",)),
        name="even_proj",
    )(x2, w_bf)


def _hgrn_cumsum_matrix(L):
    idx = np.arange(L)
    mats = []
    c = 1
    while c < L:
        start = (idx // c) * c
        end = start + c - 1
        right = ((idx // c) % 2) == 1
        u = idx[None, :]
        m_right = (u >= start[:, None]) & (u <= idx[:, None])
        m_left = (u > idx[:, None]) & (u <= end[:, None])
        mats.append(np.where(right[:, None], m_right, m_left))
        c *= 2
    u = idx[None, :]
    mats.append(u <= idx[:, None])
    mats.append(u > idx[:, None])
    return np.concatenate(mats, axis=0).astype(np.float32)


def _split2(x):
    h = x.astype(BF16)
    m = (x - h.astype(F32)).astype(BF16)
    return jnp.concatenate([h, m], axis=1)


def _hgrn_kernel(q_ref, f_ref, i_ref, g_ref, lbl_ref, gn_ref, cmat_ref, o_ref, state_ref, *, layer_j):
    L = HGRN_L
    n_levels = int(math.log2(L))
    nt = (((1,), (1,)), ((), ()))

    @pl.when(pl.program_id(1) == 0)
    def _():
        state_ref[...] = jnp.zeros_like(state_ref)

    lbl = lbl_ref[...]
    e = jnp.exp(lbl - jnp.max(lbl, axis=0, keepdims=True))
    soft = e / jnp.sum(e, axis=0, keepdims=True)
    lb_all = jnp.sum(soft[:layer_j + 1], axis=0, keepdims=True) - soft[0:1]

    row = lax.broadcasted_iota(jnp.int32, (L, 1), 0)
    xor = lax.broadcasted_iota(jnp.int32, (L, L), 0) ^ lax.broadcasted_iota(jnp.int32, (L, L), 1)
    cmat = cmat_ref[...]

    heads = range(A_HEADS)
    cols = [slice(h * LANES, (h + 1) * LANES) for h in heads]
    q, k, gdec = [], [], []
    for h in heads:
        lb = lb_all[:, cols[h]]
        f = lb + (1.0 - lb) * jax.nn.sigmoid(f_ref[:, cols[h]])
        k.append(1.0 - f)
        qr = q_ref[:, cols[h]].astype(F32)
        q.append(qr * jax.nn.sigmoid(qr))
        ex = jnp.dot(cmat, _split2(jnp.log(f)), preferred_element_type=F32)
        gdec.append(jnp.exp(ex[:, :LANES] + ex[:, LANES:]))

    scores = []
    for h in heads:
        s = jnp.where(xor == 0, lax.dot_general(q[h].astype(BF16), k[h].astype(BF16), nt,
                                                preferred_element_type=F32), 0.0)
        for lvl in range(n_levels):
            g_l = gdec[h][lvl * L:(lvl + 1) * L, :]
            right = ((row >> lvl) & 1).astype(F32)
            ql = (q[h] * g_l * right).astype(BF16)
            kl = (k[h] * g_l * (1.0 - right)).astype(BF16)
            s_l = lax.dot_general(ql, kl, nt, preferred_element_type=F32)
            s = s + jnp.where(xor < (2 << lvl), s_l, 0.0)
        scores.append(s.astype(BF16))

    for h in heads:
        iv = i_ref[:, cols[h]]
        g_full = gdec[h][n_levels * L:(n_levels + 1) * L, :]
        g_rest = gdec[h][(n_levels + 1) * L:(n_levels + 2) * L, :]
        state_t = state_ref[h]
        o = lax.dot_general((q[h] * g_full).astype(BF16), state_t.astype(BF16), nt,
                            preferred_element_type=F32)
        o = o + jnp.dot(scores[h], iv, preferred_element_type=F32)
        k_out = (k[h] * g_rest).astype(BF16)
        upd = jnp.dot(iv.astype(F32).T.astype(BF16), k_out, preferred_element_type=F32)
        state_ref[h] = state_t * g_full[L - 1:L, :] + upd
        ms = jnp.mean(o * o, axis=-1, keepdims=True)
        gate = g_ref[:, cols[h]].astype(F32)
        o = o * lax.rsqrt(ms + 1e-6) * gn_ref[...] * (gate * jax.nn.sigmoid(gate))
        o_ref[:, cols[h]] = o.astype(o_ref.dtype)


def _hgrn(aq, af, ai, ag, lb_logits, gn, cmat, layer_j, B, S):
    L = HGRN_L
    nb = S // L
    blk = pl.BlockSpec((L, A_WIDTH), lambda b, t: (b * nb + t, 0))
    return pl.pallas_call(
        functools.partial(_hgrn_kernel, layer_j=layer_j),
        grid=(B, nb),
        in_specs=[blk, blk, blk, blk, _full_spec(lb_logits.shape),
                  _full_spec(gn.shape), _full_spec(cmat.shape)],
        out_specs=blk,
        out_shape=jax.ShapeDtypeStruct((B * S, A_WIDTH), BF16),
        scratch_shapes=[pltpu.VMEM((A_HEADS, LANES, LANES), F32)],
        compiler_params=_cparams(("parallel", "arbitrary")),
        name="hgrn2",
    )(aq, af, ai, ag, lb_logits, gn, cmat)


def _attend(k_ref, vt_ref, qt_ref, streams, mask_fn):
    tk, kd = k_ref.shape
    tq = qt_ref.shape[1]
    klen = ATT_KC
    chains = []
    for kc in range(0, tk, klen):
        for st in streams:
            for qc in range(0, tq, ATT_QC):
                mask = "all" if mask_fn is None else mask_fn(kc, qc, klen)
                if not (isinstance(mask, str) and mask == "none"):
                    chains.append((st, qc, kc, None if isinstance(mask, str) else mask))

    def scores(st, qc, kc, mask):
        s = jnp.dot(k_ref[kc:kc + klen, :], qt_ref[st[0]:st[0] + kd, qc:qc + ATT_QC],
                    preferred_element_type=F32)
        if mask is not None:
            s = jnp.where(mask, s, MASK_VALUE)
        return s

    def softmax(st, qc, s):
        m_ref = st[1]
        cols = slice(qc, qc + ATT_QC)
        m8 = m_ref[0:8, cols]
        m_prev = m_ref[8:9, cols]
        qn = s.shape[1]
        parts, maxes = [], []
        for r0 in range(0, s.shape[0], ATT_SUB):
            s_g = s[r0:r0 + ATT_SUB, :].reshape(ATT_SUB // 8, 8, qn)
            m8 = jnp.maximum(m8, jnp.max(s_g, axis=0))
            parts.append(jnp.exp2((s_g - m8[None]).reshape(ATT_SUB, qn).astype(BF16)))
            maxes.append(m8)
        m_new = jnp.max(m8, axis=0, keepdims=True)
        fixed = []
        for p_g, m_g in zip(parts, maxes):
            c = jnp.exp2(m_g - m_new)
            c16 = jnp.concatenate([c, c], axis=0).astype(BF16)
            fixed.append((p_g.reshape(ATT_SUB // 16, 16, qn) * c16[None]).reshape(ATT_SUB, qn))
        p = fixed[0] if len(fixed) == 1 else jnp.concatenate(fixed, axis=0)
        alpha = jnp.exp2(m_prev - m_new)
        m_ref[0:8, cols] = m8
        m_ref[8:9, cols] = m_new
        return p, alpha

    def weighted_values(st, qc, kc, p, alpha):
        acc_ref, v0 = st[2], st[3]
        cols = slice(qc, qc + ATT_QC)
        acc_ref[:, cols] = alpha * acc_ref[:, cols] + jnp.dot(
            vt_ref[v0:v0 + acc_ref.shape[0], kc:kc + klen], p, preferred_element_type=F32)

    n = len(chains)
    ready = [scores(*c) for c in chains[:ATT_AHEAD]]
    pending = []
    for i, (st, qc, kc, _) in enumerate(chains):
        s = ready.pop(0)
        if i + ATT_AHEAD < n:
            ready.append(scores(*chains[i + ATT_AHEAD]))
        pending.append((st, qc, kc, *softmax(st, qc, s)))
        if len(pending) > ATT_BEHIND:
            weighted_values(*pending.pop(0))
    for item in pending:
        weighted_values(*item)


ATT_MULT = ATT_TQ // ATT_TK


def _tri_schedule(nq):
    qi = [i for i in range(nq) for _ in range((i + 1) * ATT_MULT)]
    ki = [j for i in range(nq) for j in range((i + 1) * ATT_MULT)]
    return jnp.asarray(qi, jnp.int32), jnp.asarray(ki, jnp.int32)


def _attend_tile(qi, ki, run, mask_for, finalize):
    rel = ki - qi * ATT_MULT

    @pl.when(rel < 0)
    def _():
        run(None)

    for r in range(ATT_MULT):
        @pl.when(rel == r)
        def _(r=r):
            run(mask_for(r * ATT_TK))
            if r == ATT_MULT - 1:
                finalize()


def _init_stats(m_ref, acc_ref):
    m_ref[...] = jnp.full_like(m_ref, MASK_VALUE)
    acc_ref[...] = jnp.zeros_like(acc_ref)


def _diff_kernel(qi_ref, ki_ref, q_ref, k_ref, v_ref, lam_ref, gn_ref, o_ref,
                 m1, a1, m2, a2, *, lam_init):
    step = pl.program_id(2)
    qi = qi_ref[step]
    ki = ki_ref[step]

    @pl.when(ki == 0)
    def _():
        _init_stats(m1, a1)
        _init_stats(m2, a2)

    def run(mask_fn):
        _attend(k_ref, v_ref, q_ref, [(0, m1, a1, 0), (LANES, m2, a2, 0)], mask_fn)

    def mask_for(key_offset):
        def chunk_mask(k0, q0, klen):
            k0 = k0 + key_offset
            if k0 + klen <= q0 + DIFF_CHUNK:
                return "all"
            if k0 >= q0 + ATT_QC:
                return "none"
            sh = DIFF_CHUNK.bit_length() - 1
            kk = (lax.broadcasted_iota(jnp.int32, (klen, ATT_QC), 0) + k0) >> sh
            qq = (lax.broadcasted_iota(jnp.int32, (klen, ATT_QC), 1) + q0) >> sh
            return kk <= qq
        return chunk_mask

    def finalize():
        lv = lam_ref[...]
        lam = (jnp.exp(jnp.sum(lv[0:1] * lv[1:2], axis=-1, keepdims=True))
               - jnp.exp(jnp.sum(lv[2:3] * lv[3:4], axis=-1, keepdims=True)) + lam_init)
        o = (a1[:LANES, :] / a1[LANES:LANES + 1, :]
             - lam * (a2[:LANES, :] / a2[LANES:LANES + 1, :])).T
        ms = jnp.mean(o * o, axis=-1, keepdims=True)
        o = o * lax.rsqrt(ms + 1e-6) * gn_ref[...] * (1.0 - lam_init)
        o_ref[...] = o.astype(o_ref.dtype)

    _attend_tile(qi, ki, run, mask_for, finalize)


def _diff_attention(bq, bk, bv, lamv, gn, lam_init, B, S):
    tq, tk = ATT_TQ, ATT_TK
    nq, nk = S // tq, S // tk
    qi, ki = _tri_schedule(nq)
    stat = pltpu.VMEM((STAT_ROWS, tq), F32)
    acc = pltpu.VMEM((VT_ROWS, tq), F32)
    grid_spec = pltpu.PrefetchScalarGridSpec(
        num_scalar_prefetch=2,
        grid=(B, B_HEADS, int(qi.shape[0])),
        in_specs=[
            pl.BlockSpec((None, 2 * LANES, tq), lambda b, h, s, qi, ki: (b, h, qi[s])),
            pl.BlockSpec((tk, LANES), lambda b, h, s, qi, ki: (b * nk + ki[s], h)),
            pl.BlockSpec((None, VT_ROWS, tk), lambda b, h, s, qi, ki: (b, h, ki[s])),
            pl.BlockSpec(lamv.shape, lambda b, h, s, qi, ki: (0, 0)),
            pl.BlockSpec(gn.shape, lambda b, h, s, qi, ki: (0, 0)),
        ],
        out_specs=pl.BlockSpec((tq, LANES), lambda b, h, s, qi, ki: (b * nq + qi[s], h)),
        scratch_shapes=[stat, acc, stat, acc],
    )
    return pl.pallas_call(
        functools.partial(_diff_kernel, lam_init=lam_init),
        grid_spec=grid_spec,
        out_shape=jax.ShapeDtypeStruct((B * S, B_WIDTH), BF16),
        compiler_params=_cparams(("parallel", "parallel", "arbitrary")),
        name="diff_attn",
    )(qi, ki, bq, bk, bv, lamv, gn)


FOX_KD = 2 * LANES
FOX_AUG = 16
FOX_VROWS = C_HEAD_DIM + 16


def _fox_selectors():
    n_pairs = C_HEADS // 2
    selk = np.zeros((3 * LANES, n_pairs * LANES), np.float32)
    onek = np.zeros((1, n_pairs * LANES), np.float32)
    selq = np.zeros((C_HEADS * FOX_AUG, 3 * LANES), np.float32)
    oneq = np.zeros((C_HEADS * FOX_AUG, LANES), np.float32)
    for h in range(C_HEADS):
        p, odd = divmod(h, 2)
        for piece in range(3):
            selk[piece * LANES + h, p * LANES + 6 * odd + 3 + piece] = -1.0
            selq[h * FOX_AUG + 6 * odd + piece, piece * LANES + h] = 1.0
            onek[0, p * LANES + 6 * odd + piece] = 1.0
            oneq[h * FOX_AUG + 6 * odd + 3 + piece, :] = 1.0
    return selk, onek, selq, oneq


def _odd_proj_kernel(x_ref, w_ref, bf_ref, qg_ref, kg_ref, gmat_ref, tri_ref,
                     selk_ref, onek_ref, selq_ref, oneq_ref,
                     qt_o, k_o, vt_o, g_o, carry_ref):
    @pl.when(pl.program_id(1) == 0)
    def _():
        carry_ref[...] = jnp.zeros_like(carry_ref)

    xb = x_ref[...].astype(BF16)
    tm = xb.shape[0]
    ch = MXU_DIM
    hpc = ch // C_HEAD_DIM
    gmat = gmat_ref[...]

    def headnorm(acc, g):
        ms = jnp.dot((acc * acc).astype(BF16), gmat, preferred_element_type=F32)
        return acc * lax.rsqrt(ms + 1e-6) * g

    zero_half = jnp.zeros((C_HEAD_DIM, tm), BF16)
    zero_tail = jnp.zeros((FOX_KD - LANES - FOX_AUG, tm), BF16)
    ones = jnp.ones((FOX_VROWS - C_HEAD_DIM, tm), BF16)
    for j in range(D_MODEL // ch):
        acc = jnp.dot(xb, w_ref[:, j * ch:(j + 1) * ch], preferred_element_type=F32)
        qt = (headnorm(acc, qg_ref[...]) * (C_HEAD_DIM ** -0.5 * LOG2E)).T.astype(BF16)
        acc = jnp.dot(xb, w_ref[:, 2 * D_MODEL + j * ch:2 * D_MODEL + (j + 1) * ch],
                      preferred_element_type=F32)
        vt = acc.T.astype(BF16)
        for i in range(hpc):
            h = j * hpc + i
            src = slice(i * C_HEAD_DIM, (i + 1) * C_HEAD_DIM)
            base = h * FOX_KD
            own, other = (base, base + C_HEAD_DIM) if h % 2 == 0 else (base + C_HEAD_DIM, base)
            qt_o[own:own + C_HEAD_DIM, :] = qt[src, :]
            qt_o[other:other + C_HEAD_DIM, :] = zero_half
            qt_o[base + LANES + FOX_AUG:base + FOX_KD, :] = zero_tail
            vt_o[h * FOX_VROWS:h * FOX_VROWS + C_HEAD_DIM, :] = vt[src, :]
            vt_o[h * FOX_VROWS + C_HEAD_DIM:(h + 1) * FOX_VROWS, :] = ones
        acc = jnp.dot(xb, w_ref[:, D_MODEL + j * ch:D_MODEL + (j + 1) * ch], preferred_element_type=F32)
        kn = headnorm(acc, kg_ref[...]).astype(BF16)
        for i in range(ch // LANES):
            p = j * (ch // LANES) + i
            k_o[:, p * FOX_KD:p * FOX_KD + LANES] = kn[:, i * LANES:(i + 1) * LANES]
        acc = jnp.dot(xb, w_ref[:, 3 * D_MODEL + j * ch:3 * D_MODEL + (j + 1) * ch],
                      preferred_element_type=F32)
        g_o[:, j * ch:(j + 1) * ch] = acc.astype(g_o.dtype)

    fl = jnp.dot(xb, w_ref[:, 4 * D_MODEL:4 * D_MODEL + LANES], preferred_element_type=F32) + bf_ref[...]
    logf = jnp.minimum(fl, 0.0) - jnp.log(1.0 + jnp.exp(-jnp.abs(fl)))
    c = _sum3(jnp.dot(tri_ref[...], _split3(logf), preferred_element_type=F32)) + carry_ref[...]
    carry_ref[...] = c[tm - 1:tm, :]
    c3 = _split3(c * LOG2E)
    aug_k = (jnp.dot(c3, selk_ref[...], preferred_element_type=F32) + onek_ref[...]).astype(BF16)
    for p in range(C_HEADS // 2):
        k_o[:, p * FOX_KD + LANES:(p + 1) * FOX_KD] = aug_k[:, p * LANES:(p + 1) * LANES]
    aug_q = lax.dot_general(selq_ref[...], c3, (((1,), (1,)), ((), ())), preferred_element_type=F32)
    aug_q = (aug_q + jnp.concatenate([oneq_ref[...]] * (tm // LANES), axis=1)).astype(BF16)
    for h in range(C_HEADS):
        qt_o[h * FOX_KD + LANES:h * FOX_KD + LANES + FOX_AUG, :] = aug_q[h * FOX_AUG:(h + 1) * FOX_AUG, :]


def _odd_proj(x2, w_bf, bf_pad, qg, kg, gmat, tri, B, S):
    T = B * S
    tm = PROJ_TM
    nt = S // tm
    tok = lambda n: pl.BlockSpec((tm, n), lambda b, t: (b * nt + t, 0))
    tspec = lambda rows: pl.BlockSpec((None, rows, tm), lambda b, t: (b, 0, t))
    selk, onek, selq, oneq = _fox_selectors()
    consts = [jnp.asarray(selk, BF16), jnp.asarray(onek, F32), jnp.asarray(selq, BF16),
              jnp.asarray(oneq, F32)]
    k_width = (C_HEADS // 2) * FOX_KD
    return pl.pallas_call(
        _odd_proj_kernel,
        grid=(B, nt),
        in_specs=[tok(D_MODEL), _full_spec(w_bf.shape), _full_spec(bf_pad.shape), _full_spec(qg.shape),
                  _full_spec(kg.shape), _full_spec(gmat.shape), _full_spec(tri.shape)]
                 + [_full_spec(c.shape) for c in consts],
        out_specs=[tspec(C_HEADS * FOX_KD), tok(k_width), tspec(C_HEADS * FOX_VROWS), tok(D_MODEL)],
        out_shape=[jax.ShapeDtypeStruct((B, C_HEADS * FOX_KD, S), BF16),
                   jax.ShapeDtypeStruct((T, k_width), BF16),
                   jax.ShapeDtypeStruct((B, C_HEADS * FOX_VROWS, S), BF16),
                   jax.ShapeDtypeStruct((T, D_MODEL), BF16)],
        scratch_shapes=[pltpu.VMEM((1, LANES), F32)],
        compiler_params=_cparams(("parallel", "arbitrary")),
        name="odd_proj",
    )(x2, w_bf, bf_pad, qg, kg, gmat, tri, *consts)


def _fox_kernel(qi_ref, ki_ref, q_ref, k_ref, v_ref, o_ref, ma, aa, mb, ab):
    step = pl.program_id(2)
    qi = qi_ref[step]
    ki = ki_ref[step]

    @pl.when(ki == 0)
    def _():
        _init_stats(ma, aa)
        _init_stats(mb, ab)

    def run(mask_fn):
        _attend(k_ref, v_ref, q_ref, [(0, ma, aa, 0), (FOX_KD, mb, ab, FOX_VROWS)], mask_fn)

    def mask_for(key_offset):
        def chunk_mask(k0, q0, klen):
            k0 = k0 + key_offset
            if k0 + klen <= q0 + 1:
                return "all"
            if k0 >= q0 + ATT_QC:
                return "none"
            kk = lax.broadcasted_iota(jnp.int32, (klen, ATT_QC), 0) + k0
            qq = lax.broadcasted_iota(jnp.int32, (klen, ATT_QC), 1) + q0
            return kk <= qq
        return chunk_mask

    def finalize():
        d = C_HEAD_DIM
        o = jnp.concatenate([aa[:d, :] / aa[d:d + 1, :], ab[:d, :] / ab[d:d + 1, :]], axis=0)
        o_ref[...] = o.T.astype(o_ref.dtype)

    _attend_tile(qi, ki, run, mask_for, finalize)


def _fox_attention(qt, k, vt, B, S):
    tq, tk = ATT_TQ, ATT_TK
    nq, nk = S // tq, S // tk
    qi, ki = _tri_schedule(nq)
    n_pairs = C_HEADS // 2
    stat = pltpu.VMEM((STAT_ROWS, tq), F32)
    acc = pltpu.VMEM((FOX_VROWS, tq), F32)
    grid_spec = pltpu.PrefetchScalarGridSpec(
        num_scalar_prefetch=2,
        grid=(B, n_pairs, int(qi.shape[0])),
        in_specs=[
            pl.BlockSpec((None, 2 * FOX_KD, tq), lambda b, h, s, qi, ki: (b, h, qi[s])),
            pl.BlockSpec((tk, FOX_KD), lambda b, h, s, qi, ki: (b * nk + ki[s], h)),
            pl.BlockSpec((None, 2 * FOX_VROWS, tk), lambda b, h, s, qi, ki: (b, h, ki[s])),
        ],
        out_specs=pl.BlockSpec((tq, LANES), lambda b, h, s, qi, ki: (b * nq + qi[s], h)),
        scratch_shapes=[stat, acc, stat, acc],
    )
    return pl.pallas_call(
        _fox_kernel,
        grid_spec=grid_spec,
        out_shape=jax.ShapeDtypeStruct((B * S, D_MODEL), BF16),
        compiler_params=_cparams(("parallel", "parallel", "arbitrary")),
        name="fox_attn",
    )(qi, ki, qt, k, vt)


def _even_mix(oa_ref, ob_ref, w_ref):
    h = jnp.dot(oa_ref[...], w_ref[:A_WIDTH, :], preferred_element_type=F32)
    return h + jnp.dot(ob_ref[...], w_ref[A_WIDTH:, :], preferred_element_type=F32)


def _odd_mix(o_ref, gate_ref, w_ref):
    o = o_ref[...].astype(F32) * jax.nn.sigmoid(gate_ref[...].astype(F32))
    return jnp.dot(o.astype(BF16), w_ref[...], preferred_element_type=F32)


def _tail_kernel(a0_ref, a1_ref, x_ref, wo_ref, g1_ref, b1_ref, w1_ref, w2_ref, g2_ref, b2_ref,
                 y_ref, act_ref, *, mix):
    x = _layer_norm_rows(ALPHA * x_ref[...] + mix(a0_ref, a1_ref, wo_ref), g1_ref[...], b1_ref[...])
    xb = x.astype(BF16)
    ch = MXU_DIM
    for j in range(D_FF // ch):
        gate = jnp.dot(xb, w1_ref[:, j * ch:(j + 1) * ch], preferred_element_type=F32)
        up = jnp.dot(xb, w1_ref[:, D_FF + j * ch:D_FF + (j + 1) * ch], preferred_element_type=F32)
        act_ref[:, j * ch:(j + 1) * ch] = (gate * jax.nn.sigmoid(gate) * up).astype(BF16)
    h = jnp.dot(act_ref[...], w2_ref[...], preferred_element_type=F32)
    y_ref[...] = _layer_norm_rows(ALPHA * x + h, g2_ref[...], b2_ref[...])


def _layer_tail(mix, acts, x2, wo_bf, g1, b1, w1_bf, w2_bf, g2, b2, name):
    T = x2.shape[0]
    tm = FFN_TM
    tok = lambda n: pl.BlockSpec((tm, n), lambda i: (i, 0))
    consts = [wo_bf, g1, b1, w1_bf, w2_bf, g2, b2]
    return pl.pallas_call(
        functools.partial(_tail_kernel, mix=mix),
        grid=(T // tm,),
        in_specs=[tok(a.shape[1]) for a in acts] + [tok(D_MODEL)] + [_full_spec(c.shape) for c in consts],
        out_specs=tok(D_MODEL),
        out_shape=jax.ShapeDtypeStruct((T, D_MODEL), F32),
        scratch_shapes=[pltpu.VMEM((tm, D_FF), BF16)],
        compiler_params=_cparams(("parallel",)),
        name=name,
    )(*acts, x2, *consts)


def kernel(x, even_w_in, even_w_out, hgrn_lb_logits, diff_lq1, diff_lk1, diff_lq2, diff_lk2,
           hgrn_norm_g, diff_norm_g, fox_w_in, fox_w_out, fox_b_f, fox_qnorm_g, fox_knorm_g,
           ffn_w1, ffn_w2, ln1_g, ln1_b, ln2_g, ln2_b):
    B, S, D = x.shape
    assert D == D_MODEL and S % ATT_TQ == 0 and S % PROJ_TM == 0 and S % HGRN_L == 0
    T = B * S
    x2 = x.reshape(T, D).astype(F32)

    cmat = jnp.asarray(_hgrn_cumsum_matrix(HGRN_L), BF16)
    tri = jnp.asarray(np.tril(np.ones((PROJ_TM, PROJ_TM), np.float32)), BF16)
    head_of = np.arange(MXU_DIM) // C_HEAD_DIM
    gmat = jnp.asarray((head_of[:, None] == head_of[None, :]).astype(np.float32) / C_HEAD_DIM, BF16)
    row = lambda v: v.astype(F32).reshape(1, -1)

    for l in range(DEPTH):
        j = l // 2
        if l % 2 == 0:
            aq, af, ai, ag, bq, bk, bv = _even_proj(x2, even_w_in[j].astype(BF16), B, S)
            o_a = _hgrn(aq, af, ai, ag, hgrn_lb_logits.astype(F32), row(hgrn_norm_g[j]), cmat, j, B, S)
            lamv = jnp.zeros((8, B_HEAD_DIM), F32).at[0:4].set(
                jnp.stack([diff_lq1[j], diff_lk1[j], diff_lq2[j], diff_lk2[j]]).astype(F32))
            lam_init = 0.8 - 0.6 * math.exp(-0.3 * l)
            o_b = _diff_attention(bq, bk, bv, lamv, row(diff_norm_g[j]), lam_init, B, S)
            mix, acts, w_out, name = _even_mix, [o_a, o_b], even_w_out[j], "even_tail"
        else:
            w_pad = jnp.pad(fox_w_in[j], ((0, 0), (0, LANES - C_HEADS))).astype(BF16)
            bf_pad = jnp.pad(fox_b_f[j].astype(F32), (0, LANES - C_HEADS)).reshape(1, LANES)
            tile4 = lambda v: jnp.tile(v.astype(F32), MXU_DIM // C_HEAD_DIM).reshape(1, MXU_DIM)
            qt, k, vt, g = _odd_proj(x2, w_pad, bf_pad, tile4(fox_qnorm_g[j]),
                                     tile4(fox_knorm_g[j]), gmat, tri, B, S)
            o = _fox_attention(qt, k, vt, B, S)
            mix, acts, w_out, name = _odd_mix, [o, g], fox_w_out[j], "odd_tail"
        x2 = _layer_tail(mix, acts, x2, w_out.astype(BF16), row(ln1_g[l]), row(ln1_b[l]),
                         ffn_w1[l].astype(BF16), ffn_w2[l].astype(BF16), row(ln2_g[l]), row(ln2_b[l]), name)
    return x2.reshape(B, S, D).astype(x.dtype)
```

```python
import functools
import math

import numpy as np
import jax
import jax.numpy as jnp
from jax import lax
from jax.experimental import pallas as pl
from jax.experimental.pallas import tpu as pltpu

F32 = jnp.float32
BF16 = jnp.bfloat16

D_MODEL = 1024
DEPTH = 4
A_WIDTH = 512
A_HEADS = 4
A_DK = 128
B_WIDTH = 512
B_HEADS = 4
B_HEAD_DIM = 64
DIFF_CHUNK = 64
C_HEADS = 16
C_HEAD_DIM = 64
D_FF = 2816
ALPHA = (2 * DEPTH) ** 0.25
MASK_VALUE = -1e30

LANES = 128
MXU_DIM = 256

PROJ_TM = 512
FFN_TM = 512
ATT_TQ = 2048
ATT_TK = 2048
ATT_QC = 256
ATT_KC = 512
ATT_SUB = 128
ATT_AHEAD = 2
ATT_BEHIND = 2
VT_ROWS = LANES + 16
HGRN_L = 256
LOG2E = math.log2(math.e)
VMEM_LIMIT = 56 * 1024 * 1024


def _cparams(sem):
    return pltpu.CompilerParams(dimension_semantics=sem, vmem_limit_bytes=VMEM_LIMIT)


def _full_spec(shape):
    nd = len(shape)
    return pl.BlockSpec(shape, lambda *_: (0,) * nd)


def _split3(x):
    h = x.astype(BF16)
    r = x - h.astype(F32)
    m = r.astype(BF16)
    l = (r - m.astype(F32)).astype(BF16)
    return jnp.concatenate([h, m, l], axis=1)


def _sum3(y):
    return y[:, :LANES] + y[:, LANES:2 * LANES] + y[:, 2 * LANES:]


def _layer_norm_rows(y, g, b):
    mu = jnp.mean(y, axis=-1, keepdims=True)
    d = y - mu
    var = jnp.mean(d * d, axis=-1, keepdims=True)
    return d * lax.rsqrt(var + 1e-5) * g + b


def _even_proj_kernel(x_ref, w_ref, aq, af, ai, ag, bq, bk, bv):
    xb = x_ref[...].astype(BF16)
    outs = (aq, af, ai, ag, bq, bk, bv)
    for n, o in enumerate(outs):
        acc = jnp.dot(xb, w_ref[:, n * 512:(n + 1) * 512], preferred_element_type=F32)
        if o is bq:
            qt = (acc * (B_HEAD_DIM ** -0.5 * LOG2E)).T.astype(o.dtype)
            zero = jnp.zeros((B_HEAD_DIM, qt.shape[1]), o.dtype)
            for h in range(B_HEADS):
                r = 2 * h * LANES
                o[r:r + B_HEAD_DIM, :] = qt[h * LANES:h * LANES + B_HEAD_DIM, :]
                o[r + B_HEAD_DIM:r + LANES, :] = zero
                o[r + LANES:r + LANES + B_HEAD_DIM, :] = zero
                o[r + LANES + B_HEAD_DIM:r + 2 * LANES, :] = qt[h * LANES + B_HEAD_DIM:(h + 1) * LANES, :]
        elif o is bv:
            vt = acc.T.astype(o.dtype)
            ones = jnp.ones((VT_ROWS - LANES, vt.shape[1]), o.dtype)
            for h in range(B_HEADS):
                o[h * VT_ROWS:h * VT_ROWS + LANES, :] = vt[h * LANES:(h + 1) * LANES, :]
                o[h * VT_ROWS + LANES:(h + 1) * VT_ROWS, :] = ones
        else:
            o[...] = acc.astype(o.dtype)


def _even_proj(x2, w_bf, B, S):
    T = x2.shape[0]
    tm = PROJ_TM
    nt = S // tm
    tok = lambda dt: jax.ShapeDtypeStruct((T, 512), dt)
    spec = pl.BlockSpec((tm, 512), lambda i: (i, 0))
    tspec = lambda rows: pl.BlockSpec((None, rows, tm), lambda i: (i // nt, 0, i % nt))
    return pl.pallas_call(
        _even_proj_kernel,
        grid=(T // tm,),
        in_specs=[pl.BlockSpec((tm, D_MODEL), lambda i: (i, 0)), _full_spec(w_bf.shape)],
        out_specs=[spec, spec, spec, spec, tspec(2 * B_WIDTH), spec, tspec(B_HEADS * VT_ROWS)],
        out_shape=[tok(BF16), tok(F32), tok(BF16), tok(BF16),
                   jax.ShapeDtypeStruct((B, 2 * B_WIDTH, S), BF16), tok(BF16),
                   jax.ShapeDtypeStruct((B, B_HEADS * VT_ROWS, S), BF16)],
        compiler_params=_cparams(("parallel",)),
        name="even_proj",
    )(x2, w_bf)


def _hgrn_cumsum_matrix(L):
    idx = np.arange(L)
    mats = []
    c = 1
    while c < L:
        start = (idx // c) * c
        end = start + c - 1
        right = ((idx // c) % 2) == 1
        u = idx[None, :]
        m_right = (u >= start[:, None]) & (u <= idx[:, None])
        m_left = (u > idx[:, None]) & (u <= end[:, None])
        mats.append(np.where(right[:, None], m_right, m_left))
        c *= 2
    u = idx[None, :]
    mats.append(u <= idx[:, None])
    mats.append(u > idx[:, None])
    return np.concatenate(mats, axis=0).astype(np.float32)


def _split2(x):
    h = x.astype(BF16)
    m = (x - h.astype(F32)).astype(BF16)
    return jnp.concatenate([h, m], axis=1)


def _hgrn_kernel(q_ref, f_ref, i_ref, g_ref, lbl_ref, gn_ref, cmat_ref, o_ref, state_ref, *, layer_j):
    L = HGRN_L
    n_levels = int(math.log2(L))
    nt = (((1,), (1,)), ((), ()))

    @pl.when(pl.program_id(1) == 0)
    def _():
        state_ref[...] = jnp.zeros_like(state_ref)

    lbl = lbl_ref[...]
    e = jnp.exp(lbl - jnp.max(lbl, axis=0, keepdims=True))
    soft = e / jnp.sum(e, axis=0, keepdims=True)
    lb_all = jnp.sum(soft[:layer_j + 1], axis=0, keepdims=True) - soft[0:1]

    row = lax.broadcasted_iota(jnp.int32, (L, 1), 0)
    xor = lax.broadcasted_iota(jnp.int32, (L, L), 0) ^ lax.broadcasted_iota(jnp.int32, (L, L), 1)
    cmat = cmat_ref[...]

    heads = range(A_HEADS)
    cols = [slice(h * LANES, (h + 1) * LANES) for h in heads]
    q, k, gdec = [], [], []
    for h in heads:
        lb = lb_all[:, cols[h]]
        f = lb + (1.0 - lb) * jax.nn.sigmoid(f_ref[:, cols[h]])
        k.append(1.0 - f)
        qr = q_ref[:, cols[h]].astype(F32)
        q.append(qr * jax.nn.sigmoid(qr))
        ex = jnp.dot(cmat, _split2(jnp.log(f)), preferred_element_type=F32)
        gdec.append(jnp.exp(ex[:, :LANES] + ex[:, LANES:]))

    scores = []
    for h in heads:
        s = jnp.where(xor == 0, lax.dot_general(q[h].astype(BF16), k[h].astype(BF16), nt,
                                                preferred_element_type=F32), 0.0)
        for lvl in range(n_levels):
            g_l = gdec[h][lvl * L:(lvl + 1) * L, :]
            right = ((row >> lvl) & 1).astype(F32)
            ql = (q[h] * g_l * right).astype(BF16)
            kl = (k[h] * g_l * (1.0 - right)).astype(BF16)
            s_l = lax.dot_general(ql, kl, nt, preferred_element_type=F32)
            s = s + jnp.where(xor < (2 << lvl), s_l, 0.0)
        scores.append(s.astype(BF16))

    for h in heads:
        iv = i_ref[:, cols[h]]
        g_full = gdec[h][n_levels * L:(n_levels + 1) * L, :]
        g_rest = gdec[h][(n_levels + 1) * L:(n_levels + 2) * L, :]
        state_t = state_ref[h]
        o = lax.dot_general((q[h] * g_full).astype(BF16), state_t.astype(BF16), nt,
                            preferred_element_type=F32)
        o = o + jnp.dot(scores[h], iv, preferred_element_type=F32)
        k_out = (k[h] * g_rest).astype(BF16)
        upd = jnp.dot(iv.astype(F32).T.astype(BF16), k_out, preferred_element_type=F32)
        state_ref[h] = state_t * g_full[L - 1:L, :] + upd
        ms = jnp.mean(o * o, axis=-1, keepdims=True)
        gate = g_ref[:, cols[h]].astype(F32)
        o = o * lax.rsqrt(ms + 1e-6) * gn_ref[...] * (gate * jax.nn.sigmoid(gate))
        o_ref[:, cols[h]] = o.astype(o_ref.dtype)


def _hgrn(aq, af, ai, ag, lb_logits, gn, cmat, layer_j, B, S):
    L = HGRN_L
    nb = S // L
    blk = pl.BlockSpec((L, A_WIDTH), lambda b, t: (b * nb + t, 0))
    return pl.pallas_call(
        functools.partial(_hgrn_kernel, layer_j=layer_j),
        grid=(B, nb),
        in_specs=[blk, blk, blk, blk, _full_spec(lb_logits.shape),
                  _full_spec(gn.shape), _full_spec(cmat.shape)],
        out_specs=blk,
        out_shape=jax.ShapeDtypeStruct((B * S, A_WIDTH), BF16),
        scratch_shapes=[pltpu.VMEM((A_HEADS, LANES, LANES), F32)],
        compiler_params=_cparams(("parallel", "arbitrary")),
        name="hgrn2",
    )(aq, af, ai, ag, lb_logits, gn, cmat)


def _attend(k_ref, vt_ref, qt_ref, streams, mask_fn):
    tk, kd = k_ref.shape
    tq = qt_ref.shape[1]
    klen = ATT_KC
    chains = []
    for kc in range(0, tk, klen):
        for st in streams:
            for qc in range(0, tq, ATT_QC):
                mask = "all" if mask_fn is None else mask_fn(kc, qc, klen)
                if not (isinstance(mask, str) and mask == "none"):
                    chains.append((st, qc, kc, None if isinstance(mask, str) else mask))

    def scores(st, qc, kc, mask):
        s = jnp.dot(k_ref[kc:kc + klen, :], qt_ref[st[0]:st[0] + kd, qc:qc + ATT_QC],
                    preferred_element_type=F32)
        if mask is not None:
            s = jnp.where(mask, s, MASK_VALUE)
        return s

    def softmax(st, qc, s):
        m_ref = st[1]
        cols = slice(qc, qc + ATT_QC)
        m_prev = m_ref[:, cols]
        m_run, parts, maxes = m_prev, [], []
        for r0 in range(0, s.shape[0], ATT_SUB):
            s_g = s[r0:r0 + ATT_SUB, :]
            m_run = jnp.maximum(m_run, jnp.max(s_g, axis=0, keepdims=True))
            parts.append(jnp.exp2((s_g - m_run).astype(BF16)))
            maxes.append(m_run)
        m_new = m_run
        parts = [p_g if m_g is m_new else p_g * jnp.exp2(m_g - m_new).astype(BF16)
                 for p_g, m_g in zip(parts, maxes)]
        p = parts[0] if len(parts) == 1 else jnp.concatenate(parts, axis=0)
        alpha = jnp.exp2(m_prev - m_new)
        m_ref[:, cols] = m_new
        return p, alpha

    def weighted_values(st, qc, kc, p, alpha):
        acc_ref, v0 = st[2], st[3]
        cols = slice(qc, qc + ATT_QC)
        acc_ref[:, cols] = alpha * acc_ref[:, cols] + jnp.dot(
            vt_ref[v0:v0 + acc_ref.shape[0], kc:kc + klen], p, preferred_element_type=F32)

    n = len(chains)
    ready = [scores(*c) for c in chains[:ATT_AHEAD]]
    pending = []
    for i, (st, qc, kc, _) in enumerate(chains):
        s = ready.pop(0)
        if i + ATT_AHEAD < n:
            ready.append(scores(*chains[i + ATT_AHEAD]))
        pending.append((st, qc, kc, *softmax(st, qc, s)))
        if len(pending) > ATT_BEHIND:
            weighted_values(*pending.pop(0))
    for item in pending:
        weighted_values(*item)


ATT_MULT = ATT_TQ // ATT_TK


def _tri_schedule(nq):
    qi = [i for i in range(nq) for _ in range((i + 1) * ATT_MULT)]
    ki = [j for i in range(nq) for j in range((i + 1) * ATT_MULT)]
    return jnp.asarray(qi, jnp.int32), jnp.asarray(ki, jnp.int32)


def _attend_tile(qi, ki, run, mask_for, finalize):
    rel = ki - qi * ATT_MULT

    @pl.when(rel < 0)
    def _():
        run(None)

    for r in range(ATT_MULT):
        @pl.when(rel == r)
        def _(r=r):
            run(mask_for(r * ATT_TK))
            if r == ATT_MULT - 1:
                finalize()


def _init_stats(m_ref, acc_ref):
    m_ref[...] = jnp.full_like(m_ref, MASK_VALUE)
    acc_ref[...] = jnp.zeros_like(acc_ref)


def _diff_kernel(qi_ref, ki_ref, q_ref, k_ref, v_ref, lam_ref, gn_ref, o_ref,
                 m1, a1, m2, a2, *, lam_init):
    step = pl.program_id(2)
    qi = qi_ref[step]
    ki = ki_ref[step]

    @pl.when(ki == 0)
    def _():
        _init_stats(m1, a1)
        _init_stats(m2, a2)

    def run(mask_fn):
        _attend(k_ref, v_ref, q_ref, [(0, m1, a1, 0), (LANES, m2, a2, 0)], mask_fn)

    def mask_for(key_offset):
        def chunk_mask(k0, q0, klen):
            k0 = k0 + key_offset
            if k0 + klen <= q0 + DIFF_CHUNK:
                return "all"
            if k0 >= q0 + ATT_QC:
                return "none"
            sh = DIFF_CHUNK.bit_length() - 1
            kk = (lax.broadcasted_iota(jnp.int32, (klen, ATT_QC), 0) + k0) >> sh
            qq = (lax.broadcasted_iota(jnp.int32, (klen, ATT_QC), 1) + q0) >> sh
            return kk <= qq
        return chunk_mask

    def finalize():
        lv = lam_ref[...]
        lam = (jnp.exp(jnp.sum(lv[0:1] * lv[1:2], axis=-1, keepdims=True))
               - jnp.exp(jnp.sum(lv[2:3] * lv[3:4], axis=-1, keepdims=True)) + lam_init)
        o = (a1[:LANES, :] / a1[LANES:LANES + 1, :]
             - lam * (a2[:LANES, :] / a2[LANES:LANES + 1, :])).T
        ms = jnp.mean(o * o, axis=-1, keepdims=True)
        o = o * lax.rsqrt(ms + 1e-6) * gn_ref[...] * (1.0 - lam_init)
        o_ref[...] = o.astype(o_ref.dtype)

    _attend_tile(qi, ki, run, mask_for, finalize)


def _diff_attention(bq, bk, bv, lamv, gn, lam_init, B, S):
    tq, tk = ATT_TQ, ATT_TK
    nq, nk = S // tq, S // tk
    qi, ki = _tri_schedule(nq)
    stat = pltpu.VMEM((1, tq), F32)
    acc = pltpu.VMEM((VT_ROWS, tq), F32)
    grid_spec = pltpu.PrefetchScalarGridSpec(
        num_scalar_prefetch=2,
        grid=(B, B_HEADS, int(qi.shape[0])),
        in_specs=[
            pl.BlockSpec((None, 2 * LANES, tq), lambda b, h, s, qi, ki: (b, h, qi[s])),
            pl.BlockSpec((tk, LANES), lambda b, h, s, qi, ki: (b * nk + ki[s], h)),
            pl.BlockSpec((None, VT_ROWS, tk), lambda b, h, s, qi, ki: (b, h, ki[s])),
            pl.BlockSpec(lamv.shape, lambda b, h, s, qi, ki: (0, 0)),
            pl.BlockSpec(gn.shape, lambda b, h, s, qi, ki: (0, 0)),
        ],
        out_specs=pl.BlockSpec((tq, LANES), lambda b, h, s, qi, ki: (b * nq + qi[s], h)),
        scratch_shapes=[stat, acc, stat, acc],
    )
    return pl.pallas_call(
        functools.partial(_diff_kernel, lam_init=lam_init),
        grid_spec=grid_spec,
        out_shape=jax.ShapeDtypeStruct((B * S, B_WIDTH), BF16),
        compiler_params=_cparams(("parallel", "parallel", "arbitrary")),
        name="diff_attn",
    )(qi, ki, bq, bk, bv, lamv, gn)


FOX_KD = 2 * LANES
FOX_AUG = 16
FOX_VROWS = C_HEAD_DIM + 16


def _fox_selectors():
    n_pairs = C_HEADS // 2
    selk = np.zeros((3 * LANES, n_pairs * LANES), np.float32)
    onek = np.zeros((1, n_pairs * LANES), np.float32)
    selq = np.zeros((C_HEADS * FOX_AUG, 3 * LANES), np.float32)
    oneq = np.zeros((C_HEADS * FOX_AUG, LANES), np.float32)
    for h in range(C_HEADS):
        p, odd = divmod(h, 2)
        for piece in range(3):
            selk[piece * LANES + h, p * LANES + 6 * odd + 3 + piece] = -1.0
            selq[h * FOX_AUG + 6 * odd + piece, piece * LANES + h] = 1.0
            onek[0, p * LANES + 6 * odd + piece] = 1.0
            oneq[h * FOX_AUG + 6 * odd + 3 + piece, :] = 1.0
    return selk, onek, selq, oneq


def _odd_proj_kernel(x_ref, w_ref, bf_ref, qg_ref, kg_ref, gmat_ref, tri_ref,
                     selk_ref, onek_ref, selq_ref, oneq_ref,
                     qt_o, k_o, vt_o, g_o, carry_ref):
    @pl.when(pl.program_id(1) == 0)
    def _():
        carry_ref[...] = jnp.zeros_like(carry_ref)

    xb = x_ref[...].astype(BF16)
    tm = xb.shape[0]
    ch = MXU_DIM
    hpc = ch // C_HEAD_DIM
    gmat = gmat_ref[...]

    def headnorm(acc, g):
        ms = jnp.dot((acc * acc).astype(BF16), gmat, preferred_element_type=F32)
        return acc * lax.rsqrt(ms + 1e-6) * g

    zero_half = jnp.zeros((C_HEAD_DIM, tm), BF16)
    zero_tail = jnp.zeros((FOX_KD - LANES - FOX_AUG, tm), BF16)
    ones = jnp.ones((FOX_VROWS - C_HEAD_DIM, tm), BF16)
    for j in range(D_MODEL // ch):
        acc = jnp.dot(xb, w_ref[:, j * ch:(j + 1) * ch], preferred_element_type=F32)
        qt = (headnorm(acc, qg_ref[...]) * (C_HEAD_DIM ** -0.5 * LOG2E)).T.astype(BF16)
        acc = jnp.dot(xb, w_ref[:, 2 * D_MODEL + j * ch:2 * D_MODEL + (j + 1) * ch],
                      preferred_element_type=F32)
        vt = acc.T.astype(BF16)
        for i in range(hpc):
            h = j * hpc + i
            src = slice(i * C_HEAD_DIM, (i + 1) * C_HEAD_DIM)
            base = h * FOX_KD
            own, other = (base, base + C_HEAD_DIM) if h % 2 == 0 else (base + C_HEAD_DIM, base)
            qt_o[own:own + C_HEAD_DIM, :] = qt[src, :]
            qt_o[other:other + C_HEAD_DIM, :] = zero_half
            qt_o[base + LANES + FOX_AUG:base + FOX_KD, :] = zero_tail
            vt_o[h * FOX_VROWS:h * FOX_VROWS + C_HEAD_DIM, :] = vt[src, :]
            vt_o[h * FOX_VROWS + C_HEAD_DIM:(h + 1) * FOX_VROWS, :] = ones
        acc = jnp.dot(xb, w_ref[:, D_MODEL + j * ch:D_MODEL + (j + 1) * ch], preferred_element_type=F32)
        kn = headnorm(acc, kg_ref[...]).astype(BF16)
        for i in range(ch // LANES):
            p = j * (ch // LANES) + i
            k_o[:, p * FOX_KD:p * FOX_KD + LANES] = kn[:, i * LANES:(i + 1) * LANES]
        acc = jnp.dot(xb, w_ref[:, 3 * D_MODEL + j * ch:3 * D_MODEL + (j + 1) * ch],
                      preferred_element_type=F32)
        g_o[:, j * ch:(j + 1) * ch] = acc.astype(g_o.dtype)

    fl = jnp.dot(xb, w_ref[:, 4 * D_MODEL:4 * D_MODEL + LANES], preferred_element_type=F32) + bf_ref[...]
    logf = jnp.minimum(fl, 0.0) - jnp.log(1.0 + jnp.exp(-jnp.abs(fl)))
    c = _sum3(jnp.dot(tri_ref[...], _split3(logf), preferred_element_type=F32)) + carry_ref[...]
    carry_ref[...] = c[tm - 1:tm, :]
    c3 = _split3(c * LOG2E)
    aug_k = (jnp.dot(c3, selk_ref[...], preferred_element_type=F32) + onek_ref[...]).astype(BF16)
    for p in range(C_HEADS // 2):
        k_o[:, p * FOX_KD + LANES:(p + 1) * FOX_KD] = aug_k[:, p * LANES:(p + 1) * LANES]
    aug_q = lax.dot_general(selq_ref[...], c3, (((1,), (1,)), ((), ())), preferred_element_type=F32)
    aug_q = (aug_q + jnp.concatenate([oneq_ref[...]] * (tm // LANES), axis=1)).astype(BF16)
    for h in range(C_HEADS):
        qt_o[h * FOX_KD + LANES:h * FOX_KD + LANES + FOX_AUG, :] = aug_q[h * FOX_AUG:(h + 1) * FOX_AUG, :]


def _odd_proj(x2, w_bf, bf_pad, qg, kg, gmat, tri, B, S):
    T = B * S
    tm = PROJ_TM
    nt = S // tm
    tok = lambda n: pl.BlockSpec((tm, n), lambda b, t: (b * nt + t, 0))
    tspec = lambda rows: pl.BlockSpec((None, rows, tm), lambda b, t: (b, 0, t))
    selk, onek, selq, oneq = _fox_selectors()
    consts = [jnp.asarray(selk, BF16), jnp.asarray(onek, F32), jnp.asarray(selq, BF16),
              jnp.asarray(oneq, F32)]
    k_width = (C_HEADS // 2) * FOX_KD
    return pl.pallas_call(
        _odd_proj_kernel,
        grid=(B, nt),
        in_specs=[tok(D_MODEL), _full_spec(w_bf.shape), _full_spec(bf_pad.shape), _full_spec(qg.shape),
                  _full_spec(kg.shape), _full_spec(gmat.shape), _full_spec(tri.shape)]
                 + [_full_spec(c.shape) for c in consts],
        out_specs=[tspec(C_HEADS * FOX_KD), tok(k_width), tspec(C_HEADS * FOX_VROWS), tok(D_MODEL)],
        out_shape=[jax.ShapeDtypeStruct((B, C_HEADS * FOX_KD, S), BF16),
                   jax.ShapeDtypeStruct((T, k_width), BF16),
                   jax.ShapeDtypeStruct((B, C_HEADS * FOX_VROWS, S), BF16),
                   jax.ShapeDtypeStruct((T, D_MODEL), BF16)],
        scratch_shapes=[pltpu.VMEM((1, LANES), F32)],
        compiler_params=_cparams(("parallel", "arbitrary")),
        name="odd_proj",
    )(x2, w_bf, bf_pad, qg, kg, gmat, tri, *consts)


def _fox_kernel(qi_ref, ki_ref, q_ref, k_ref, v_ref, o_ref, ma, aa, mb, ab):
    step = pl.program_id(2)
    qi = qi_ref[step]
    ki = ki_ref[step]

    @pl.when(ki == 0)
    def _():
        _init_stats(ma, aa)
        _init_stats(mb, ab)

    def run(mask_fn):
        _attend(k_ref, v_ref, q_ref, [(0, ma, aa, 0), (FOX_KD, mb, ab, FOX_VROWS)], mask_fn)

    def mask_for(key_offset):
        def chunk_mask(k0, q0, klen):
            k0 = k0 + key_offset
            if k0 + klen <= q0 + 1:
                return "all"
            if k0 >= q0 + ATT_QC:
                return "none"
            kk = lax.broadcasted_iota(jnp.int32, (klen, ATT_QC), 0) + k0
            qq = lax.broadcasted_iota(jnp.int32, (klen, ATT_QC), 1) + q0
            return kk <= qq
        return chunk_mask

    def finalize():
        d = C_HEAD_DIM
        o = jnp.concatenate([aa[:d, :] / aa[d:d + 1, :], ab[:d, :] / ab[d:d + 1, :]], axis=0)
        o_ref[...] = o.T.astype(o_ref.dtype)

    _attend_tile(qi, ki, run, mask_for, finalize)


def _fox_attention(qt, k, vt, B, S):
    tq, tk = ATT_TQ, ATT_TK
    nq, nk = S // tq, S // tk
    qi, ki = _tri_schedule(nq)
    n_pairs = C_HEADS // 2
    stat = pltpu.VMEM((1, tq), F32)
    acc = pltpu.VMEM((FOX_VROWS, tq), F32)
    grid_spec = pltpu.PrefetchScalarGridSpec(
        num_scalar_prefetch=2,
        grid=(B, n_pairs, int(qi.shape[0])),
        in_specs=[
            pl.BlockSpec((None, 2 * FOX_KD, tq), lambda b, h, s, qi, ki: (b, h, qi[s])),
            pl.BlockSpec((tk, FOX_KD), lambda b, h, s, qi, ki: (b * nk + ki[s], h)),
            pl.BlockSpec((None, 2 * FOX_VROWS, tk), lambda b, h, s, qi, ki: (b, h, ki[s])),
        ],
        out_specs=pl.BlockSpec((tq, LANES), lambda b, h, s, qi, ki: (b * nq + qi[s], h)),
        scratch_shapes=[stat, acc, stat, acc],
    )
    return pl.pallas_call(
        _fox_kernel,
        grid_spec=grid_spec,
        out_shape=jax.ShapeDtypeStruct((B * S, D_MODEL), BF16),
        compiler_params=_cparams(("parallel", "parallel", "arbitrary")),
        name="fox_attn",
    )(qi, ki, qt, k, vt)


def _even_mix(oa_ref, ob_ref, w_ref):
    h = jnp.dot(oa_ref[...], w_ref[:A_WIDTH, :], preferred_element_type=F32)
    return h + jnp.dot(ob_ref[...], w_ref[A_WIDTH:, :], preferred_element_type=F32)


def _odd_mix(o_ref, gate_ref, w_ref):
    o = o_ref[...].astype(F32) * jax.nn.sigmoid(gate_ref[...].astype(F32))
    return jnp.dot(o.astype(BF16), w_ref[...], preferred_element_type=F32)


def _tail_kernel(a0_ref, a1_ref, x_ref, wo_ref, g1_ref, b1_ref, w1_ref, w2_ref, g2_ref, b2_ref,
                 y_ref, act_ref, *, mix):
    x = _layer_norm_rows(ALPHA * x_ref[...] + mix(a0_ref, a1_ref, wo_ref), g1_ref[...], b1_ref[...])
    xb = x.astype(BF16)
    ch = MXU_DIM
    for j in range(D_FF // ch):
        gate = jnp.dot(xb, w1_ref[:, j * ch:(j + 1) * ch], preferred_element_type=F32)
        up = jnp.dot(xb, w1_ref[:, D_FF + j * ch:D_FF + (j + 1) * ch], preferred_element_type=F32)
        act_ref[:, j * ch:(j + 1) * ch] = (gate * jax.nn.sigmoid(gate) * up).astype(BF16)
    h = jnp.dot(act_ref[...], w2_ref[...], preferred_element_type=F32)
    y_ref[...] = _layer_norm_rows(ALPHA * x + h, g2_ref[...], b2_ref[...])


def _layer_tail(mix, acts, x2, wo_bf, g1, b1, w1_bf, w2_bf, g2, b2, name):
    T = x2.shape[0]
    tm = FFN_TM
    tok = lambda n: pl.BlockSpec((tm, n), lambda i: (i, 0))
    consts = [wo_bf, g1, b1, w1_bf, w2_bf, g2, b2]
    return pl.pallas_call(
        functools.partial(_tail_kernel, mix=mix),
        grid=(T // tm,),
        in_specs=[tok(a.shape[1]) for a in acts] + [tok(D_MODEL)] + [_full_spec(c.shape) for c in consts],
        out_specs=tok(D_MODEL),
        out_shape=jax.ShapeDtypeStruct((T, D_MODEL), F32),
        scratch_shapes=[pltpu.VMEM((tm, D_FF), BF16)],
        compiler_params=_cparams(("parallel",)),
        name=name,
    )(*acts, x2, *consts)


def kernel(x, even_w_in, even_w_out, hgrn_lb_logits, diff_lq1, diff_lk1, diff_lq2, diff_lk2,
           hgrn_norm_g, diff_norm_g, fox_w_in, fox_w_out, fox_b_f, fox_qnorm_g, fox_knorm_g,
           ffn_w1, ffn_w2, ln1_g, ln1_b, ln2_g, ln2_b):
    B, S, D = x.shape
    assert D == D_MODEL and S % ATT_TQ == 0 and S % PROJ_TM == 0 and S % HGRN_L == 0
    T = B * S
    x2 = x.reshape(T, D).astype(F32)

    cmat = jnp.asarray(_hgrn_cumsum_matrix(HGRN_L), BF16)
    tri = jnp.asarray(np.tril(np.ones((PROJ_TM, PROJ_TM), np.float32)), BF16)
    head_of = np.arange(MXU_DIM) // C_HEAD_DIM
    gmat = jnp.asarray((head_of[:, None] == head_of[None, :]).astype(np.float32) / C_HEAD_DIM, BF16)
    row = lambda v: v.astype(F32).reshape(1, -1)

    for l in range(DEPTH):
        j = l // 2
        if l % 2 == 0:
            aq, af, ai, ag, bq, bk, bv = _even_proj(x2, even_w_in[j].astype(BF16), B, S)
            o_a = _hgrn(aq, af, ai, ag, hgrn_lb_logits.astype(F32), row(hgrn_norm_g[j]), cmat, j, B, S)
            lamv = jnp.zeros((8, B_HEAD_DIM), F32).at[0:4].set(
                jnp.stack([diff_lq1[j], diff_lk1[j], diff_lq2[j], diff_lk2[j]]).astype(F32))
            lam_init = 0.8 - 0.6 * math.exp(-0.3 * l)
            o_b = _diff_attention(bq, bk, bv, lamv, row(diff_norm_g[j]), lam_init, B, S)
            mix, acts, w_out, name = _even_mix, [o_a, o_b], even_w_out[j], "even_tail"
        else:
            w_pad = jnp.pad(fox_w_in[j], ((0, 0), (0, LANES - C_HEADS))).astype(BF16)
            bf_pad = jnp.pad(fox_b_f[j].astype(F32), (0, LANES - C_HEADS)).reshape(1, LANES)
            tile4 = lambda v: jnp.tile(v.astype(F32), MXU_DIM // C_HEAD_DIM).reshape(1, MXU_DIM)
            qt, k, vt, g = _odd_proj(x2, w_pad, bf_pad, tile4(fox_qnorm_g[j]),
                                     tile4(fox_knorm_g[j]), gmat, tri, B, S)
            o = _fox_attention(qt, k, vt, B, S)
            mix, acts, w_out, name = _odd_mix, [o, g], fox_w_out[j], "odd_tail"
        x2 = _layer_tail(mix, acts, x2, w_out.astype(BF16), row(ln1_g[l]), row(ln1_b[l]),
                         ffn_w1[l].astype(BF16), ffn_w2[l].astype(BF16), row(ln2_g[l]), row(ln2_b[l]), name)
    return x2.reshape(B, S, D).astype(x.dtype)
```

```python
import functools
import math

import numpy as np
import jax
import jax.numpy as jnp
from jax import lax
from jax.experimental import pallas as pl
from jax.experimental.pallas import tpu as pltpu

F32 = jnp.float32
BF16 = jnp.bfloat16

D_MODEL = 1024
DEPTH = 4
A_WIDTH = 512
A_HEADS = 4
A_DK = 128
B_WIDTH = 512
B_HEADS = 4
B_HEAD_DIM = 64
DIFF_CHUNK = 64
C_HEADS = 16
C_HEAD_DIM = 64
D_FF = 2816
ALPHA = (2 * DEPTH) ** 0.25
MASK_VALUE = -1e30

LANES = 128
MXU_DIM = 256

PROJ_TM = 512
FFN_TM = 512
ATT_TQ = 2048
ATT_TK = 2048
ATT_QC = 256
ATT_KC = 512
ATT_SUB = 64
ATT_AHEAD = 2
ATT_BEHIND = 2
VT_ROWS = LANES + 16
HGRN_L = 256
LOG2E = math.log2(math.e)
VMEM_LIMIT = 56 * 1024 * 1024


def _cparams(sem):
    return pltpu.CompilerParams(dimension_semantics=sem, vmem_limit_bytes=VMEM_LIMIT)


def _full_spec(shape):
    nd = len(shape)
    return pl.BlockSpec(shape, lambda *_: (0,) * nd)


def _split3(x):
    h = x.astype(BF16)
    r = x - h.astype(F32)
    m = r.astype(BF16)
    l = (r - m.astype(F32)).astype(BF16)
    return jnp.concatenate([h, m, l], axis=1)


def _sum3(y):
    return y[:, :LANES] + y[:, LANES:2 * LANES] + y[:, 2 * LANES:]


def _layer_norm_rows(y, g, b):
    mu = jnp.mean(y, axis=-1, keepdims=True)
    d = y - mu
    var = jnp.mean(d * d, axis=-1, keepdims=True)
    return d * lax.rsqrt(var + 1e-5) * g + b


def _even_proj_kernel(x_ref, w_ref, aq, af, ai, ag, bq, bk, bv):
    xb = x_ref[...].astype(BF16)
    outs = (aq, af, ai, ag, bq, bk, bv)
    for n, o in enumerate(outs):
        acc = jnp.dot(xb, w_ref[:, n * 512:(n + 1) * 512], preferred_element_type=F32)
        if o is bq:
            qt = (acc * (B_HEAD_DIM ** -0.5 * LOG2E)).T.astype(o.dtype)
            zero = jnp.zeros((B_HEAD_DIM, qt.shape[1]), o.dtype)
            for h in range(B_HEADS):
                r = 2 * h * LANES
                o[r:r + B_HEAD_DIM, :] = qt[h * LANES:h * LANES + B_HEAD_DIM, :]
                o[r + B_HEAD_DIM:r + LANES, :] = zero
                o[r + LANES:r + LANES + B_HEAD_DIM, :] = zero
                o[r + LANES + B_HEAD_DIM:r + 2 * LANES, :] = qt[h * LANES + B_HEAD_DIM:(h + 1) * LANES, :]
        elif o is bv:
            vt = acc.T.astype(o.dtype)
            ones = jnp.ones((VT_ROWS - LANES, vt.shape[1]), o.dtype)
            for h in range(B_HEADS):
                o[h * VT_ROWS:h * VT_ROWS + LANES, :] = vt[h * LANES:(h + 1) * LANES, :]
                o[h * VT_ROWS + LANES:(h + 1) * VT_ROWS, :] = ones
        else:
            o[...] = acc.astype(o.dtype)


def _even_proj(x2, w_bf, B, S):
    T = x2.shape[0]
    tm = PROJ_TM
    nt = S // tm
    tok = lambda dt: jax.ShapeDtypeStruct((T, 512), dt)
    spec = pl.BlockSpec((tm, 512), lambda i: (i, 0))
    tspec = lambda rows: pl.BlockSpec((None, rows, tm), lambda i: (i // nt, 0, i % nt))
    return pl.pallas_call(
        _even_proj_kernel,
        grid=(T // tm,),
        in_specs=[pl.BlockSpec((tm, D_MODEL), lambda i: (i, 0)), _full_spec(w_bf.shape)],
        out_specs=[spec, spec, spec, spec, tspec(2 * B_WIDTH), spec, tspec(B_HEADS * VT_ROWS)],
        out_shape=[tok(BF16), tok(F32), tok(BF16), tok(BF16),
                   jax.ShapeDtypeStruct((B, 2 * B_WIDTH, S), BF16), tok(BF16),
                   jax.ShapeDtypeStruct((B, B_HEADS * VT_ROWS, S), BF16)],
        compiler_params=_cparams(("parallel",)),
        name="even_proj",
    )(x2, w_bf)


def _hgrn_cumsum_matrix(L):
    idx = np.arange(L)
    mats = []
    c = 1
    while c < L:
        start = (idx // c) * c
        end = start + c - 1
        right = ((idx // c) % 2) == 1
        u = idx[None, :]
        m_right = (u >= start[:, None]) & (u <= idx[:, None])
        m_left = (u > idx[:, None]) & (u <= end[:, None])
        mats.append(np.where(right[:, None], m_right, m_left))
        c *= 2
    u = idx[None, :]
    mats.append(u <= idx[:, None])
    mats.append(u > idx[:, None])
    return np.concatenate(mats, axis=0).astype(np.float32)


def _split2(x):
    h = x.astype(BF16)
    m = (x - h.astype(F32)).astype(BF16)
    return jnp.concatenate([h, m], axis=1)


def _hgrn_kernel(q_ref, f_ref, i_ref, g_ref, lbl_ref, gn_ref, cmat_ref, o_ref, state_ref, *, layer_j):
    L = HGRN_L
    n_levels = int(math.log2(L))
    nt = (((1,), (1,)), ((), ()))

    @pl.when(pl.program_id(1) == 0)
    def _():
        state_ref[...] = jnp.zeros_like(state_ref)

    lbl = lbl_ref[...]
    e = jnp.exp(lbl - jnp.max(lbl, axis=0, keepdims=True))
    soft = e / jnp.sum(e, axis=0, keepdims=True)
    lb_all = jnp.sum(soft[:layer_j + 1], axis=0, keepdims=True) - soft[0:1]

    row = lax.broadcasted_iota(jnp.int32, (L, 1), 0)
    xor = lax.broadcasted_iota(jnp.int32, (L, L), 0) ^ lax.broadcasted_iota(jnp.int32, (L, L), 1)
    cmat = cmat_ref[...]

    heads = range(A_HEADS)
    cols = [slice(h * LANES, (h + 1) * LANES) for h in heads]
    q, k, gdec = [], [], []
    for h in heads:
        lb = lb_all[:, cols[h]]
        f = lb + (1.0 - lb) * jax.nn.sigmoid(f_ref[:, cols[h]])
        k.append(1.0 - f)
        qr = q_ref[:, cols[h]].astype(F32)
        q.append(qr * jax.nn.sigmoid(qr))
        ex = jnp.dot(cmat, _split2(jnp.log(f)), preferred_element_type=F32)
        gdec.append(jnp.exp(ex[:, :LANES] + ex[:, LANES:]))

    scores = []
    for h in heads:
        s = jnp.where(xor == 0, lax.dot_general(q[h].astype(BF16), k[h].astype(BF16), nt,
                                                preferred_element_type=F32), 0.0)
        for lvl in range(n_levels):
            g_l = gdec[h][lvl * L:(lvl + 1) * L, :]
            right = ((row >> lvl) & 1).astype(F32)
            ql = (q[h] * g_l * right).astype(BF16)
            kl = (k[h] * g_l * (1.0 - right)).astype(BF16)
            s_l = lax.dot_general(ql, kl, nt, preferred_element_type=F32)
            s = s + jnp.where(xor < (2 << lvl), s_l, 0.0)
        scores.append(s.astype(BF16))

    for h in heads:
        iv = i_ref[:, cols[h]]
        g_full = gdec[h][n_levels * L:(n_levels + 1) * L, :]
        g_rest = gdec[h][(n_levels + 1) * L:(n_levels + 2) * L, :]
        state_t = state_ref[h]
        o = lax.dot_general((q[h] * g_full).astype(BF16), state_t.astype(BF16), nt,
                            preferred_element_type=F32)
        o = o + jnp.dot(scores[h], iv, preferred_element_type=F32)
        k_out = (k[h] * g_rest).astype(BF16)
        upd = jnp.dot(iv.astype(F32).T.astype(BF16), k_out, preferred_element_type=F32)
        state_ref[h] = state_t * g_full[L - 1:L, :] + upd
        ms = jnp.mean(o * o, axis=-1, keepdims=True)
        gate = g_ref[:, cols[h]].astype(F32)
        o = o * lax.rsqrt(ms + 1e-6) * gn_ref[...] * (gate * jax.nn.sigmoid(gate))
        o_ref[:, cols[h]] = o.astype(o_ref.dtype)


def _hgrn(aq, af, ai, ag, lb_logits, gn, cmat, layer_j, B, S):
    L = HGRN_L
    nb = S // L
    blk = pl.BlockSpec((L, A_WIDTH), lambda b, t: (b * nb + t, 0))
    return pl.pallas_call(
        functools.partial(_hgrn_kernel, layer_j=layer_j),
        grid=(B, nb),
        in_specs=[blk, blk, blk, blk, _full_spec(lb_logits.shape),
                  _full_spec(gn.shape), _full_spec(cmat.shape)],
        out_specs=blk,
        out_shape=jax.ShapeDtypeStruct((B * S, A_WIDTH), BF16),
        scratch_shapes=[pltpu.VMEM((A_HEADS, LANES, LANES), F32)],
        compiler_params=_cparams(("parallel", "arbitrary")),
        name="hgrn2",
    )(aq, af, ai, ag, lb_logits, gn, cmat)


def _attend(k_ref, vt_ref, qt_ref, streams, mask_fn):
    tk = k_ref.shape[0]
    kd = LANES
    tq = qt_ref.shape[1]
    klen = ATT_KC
    chains = []
    for kc in range(0, tk, klen):
        for st in streams:
            for qc in range(0, tq, ATT_QC):
                mask = "all" if mask_fn is None else mask_fn(kc, qc, klen)
                if not (isinstance(mask, str) and mask == "none"):
                    chains.append((st, qc, kc, None if isinstance(mask, str) else mask))

    def scores(st, qc, kc, mask):
        s = jnp.dot(k_ref[kc:kc + klen, st[4]:st[4] + kd], qt_ref[st[0]:st[0] + kd, qc:qc + ATT_QC],
                    preferred_element_type=F32)
        if mask is not None:
            s = jnp.where(mask, s, MASK_VALUE)
        return s

    def softmax(st, qc, s):
        m_ref = st[1]
        cols = slice(qc, qc + ATT_QC)
        m_prev = m_ref[:, cols]
        m_run, parts, maxes = m_prev, [], []
        for r0 in range(0, s.shape[0], ATT_SUB):
            s_g = s[r0:r0 + ATT_SUB, :]
            m_run = jnp.maximum(m_run, jnp.max(s_g, axis=0, keepdims=True))
            parts.append(jnp.exp2((s_g - m_run).astype(BF16)))
            maxes.append(m_run)
        m_new = m_run
        parts = [p_g if m_g is m_new else p_g * jnp.exp2(m_g - m_new).astype(BF16)
                 for p_g, m_g in zip(parts, maxes)]
        p = parts[0] if len(parts) == 1 else jnp.concatenate(parts, axis=0)
        alpha = jnp.exp2(m_prev - m_new)
        m_ref[:, cols] = m_new
        return p, alpha

    def weighted_values(st, qc, kc, p, alpha):
        acc_ref, v0 = st[2], st[3]
        cols = slice(qc, qc + ATT_QC)
        acc_ref[:, cols] = alpha * acc_ref[:, cols] + jnp.dot(
            vt_ref[v0:v0 + acc_ref.shape[0], kc:kc + klen], p, preferred_element_type=F32)

    n = len(chains)
    ready = [scores(*c) for c in chains[:ATT_AHEAD]]
    pending = []
    for i, (st, qc, kc, _) in enumerate(chains):
        s = ready.pop(0)
        if i + ATT_AHEAD < n:
            ready.append(scores(*chains[i + ATT_AHEAD]))
        pending.append((st, qc, kc, *softmax(st, qc, s)))
        if len(pending) > ATT_BEHIND:
            weighted_values(*pending.pop(0))
    for item in pending:
        weighted_values(*item)


ATT_MULT = ATT_TQ // ATT_TK


def _tri_schedule(nq):
    qi = [i for i in range(nq) for _ in range((i + 1) * ATT_MULT)]
    ki = [j for i in range(nq) for j in range((i + 1) * ATT_MULT)]
    return jnp.asarray(qi, jnp.int32), jnp.asarray(ki, jnp.int32)


def _attend_tile(qi, ki, run, mask_for, finalize):
    rel = ki - qi * ATT_MULT

    @pl.when(rel < 0)
    def _():
        run(None)

    for r in range(ATT_MULT):
        @pl.when(rel == r)
        def _(r=r):
            run(mask_for(r * ATT_TK))
            if r == ATT_MULT - 1:
                finalize()


def _init_stats(m_ref, acc_ref):
    m_ref[...] = jnp.full_like(m_ref, MASK_VALUE)
    acc_ref[...] = jnp.zeros_like(acc_ref)


def _diff_kernel(qi_ref, ki_ref, q_ref, k_ref, v_ref, lam_ref, gn_ref, o_ref,
                 m1, a1, m2, a2, *, lam_init):
    step = pl.program_id(2)
    qi = qi_ref[step]
    ki = ki_ref[step]

    @pl.when(ki == 0)
    def _():
        _init_stats(m1, a1)
        _init_stats(m2, a2)

    def run(mask_fn):
        _attend(k_ref, v_ref, q_ref, [(0, m1, a1, 0, 0), (LANES, m2, a2, 0, 0)], mask_fn)

    def mask_for(key_offset):
        def chunk_mask(k0, q0, klen):
            k0 = k0 + key_offset
            if k0 + klen <= q0 + DIFF_CHUNK:
                return "all"
            if k0 >= q0 + ATT_QC:
                return "none"
            sh = DIFF_CHUNK.bit_length() - 1
            kk = (lax.broadcasted_iota(jnp.int32, (klen, ATT_QC), 0) + k0) >> sh
            qq = (lax.broadcasted_iota(jnp.int32, (klen, ATT_QC), 1) + q0) >> sh
            return kk <= qq
        return chunk_mask

    def finalize():
        lv = lam_ref[...]
        lam = (jnp.exp(jnp.sum(lv[0:1] * lv[1:2], axis=-1, keepdims=True))
               - jnp.exp(jnp.sum(lv[2:3] * lv[3:4], axis=-1, keepdims=True)) + lam_init)
        o = (a1[:LANES, :] / a1[LANES:LANES + 1, :]
             - lam * (a2[:LANES, :] / a2[LANES:LANES + 1, :])).T
        ms = jnp.mean(o * o, axis=-1, keepdims=True)
        o = o * lax.rsqrt(ms + 1e-6) * gn_ref[...] * (1.0 - lam_init)
        o_ref[...] = o.astype(o_ref.dtype)

    _attend_tile(qi, ki, run, mask_for, finalize)


def _diff_attention(bq, bk, bv, lamv, gn, lam_init, B, S):
    tq, tk = ATT_TQ, ATT_TK
    nq, nk = S // tq, S // tk
    qi, ki = _tri_schedule(nq)
    stat = pltpu.VMEM((1, tq), F32)
    acc = pltpu.VMEM((VT_ROWS, tq), F32)
    grid_spec = pltpu.PrefetchScalarGridSpec(
        num_scalar_prefetch=2,
        grid=(B, B_HEADS, int(qi.shape[0])),
        in_specs=[
            pl.BlockSpec((None, 2 * LANES, tq), lambda b, h, s, qi, ki: (b, h, qi[s])),
            pl.BlockSpec((tk, LANES), lambda b, h, s, qi, ki: (b * nk + ki[s], h)),
            pl.BlockSpec((None, VT_ROWS, tk), lambda b, h, s, qi, ki: (b, h, ki[s])),
            pl.BlockSpec(lamv.shape, lambda b, h, s, qi, ki: (0, 0)),
            pl.BlockSpec(gn.shape, lambda b, h, s, qi, ki: (0, 0)),
        ],
        out_specs=pl.BlockSpec((tq, LANES), lambda b, h, s, qi, ki: (b * nq + qi[s], h)),
        scratch_shapes=[stat, acc, stat, acc],
    )
    return pl.pallas_call(
        functools.partial(_diff_kernel, lam_init=lam_init),
        grid_spec=grid_spec,
        out_shape=jax.ShapeDtypeStruct((B * S, B_WIDTH), BF16),
        compiler_params=_cparams(("parallel", "parallel", "arbitrary")),
        name="diff_attn",
    )(qi, ki, bq, bk, bv, lamv, gn)


FOX_AUG = 16
FOX_VROWS = C_HEAD_DIM + 16
FOX_PAD = LANES - C_HEAD_DIM - FOX_AUG


def _fox_slots(h):
    base = (h // 2) * 2 * LANES
    if h % 2 == 0:
        return base, base + C_HEAD_DIM, base + C_HEAD_DIM + FOX_AUG
    return base + 2 * LANES - C_HEAD_DIM, base + LANES, base + LANES + FOX_AUG


def _fox_selectors():
    n_pairs = C_HEADS // 2
    selk = np.zeros((3 * LANES, n_pairs * LANES), np.float32)
    onek = np.zeros((1, n_pairs * LANES), np.float32)
    selq = np.zeros((C_HEADS * FOX_AUG, 3 * LANES), np.float32)
    oneq = np.zeros((C_HEADS * FOX_AUG, LANES), np.float32)
    for h in range(C_HEADS):
        p, odd = divmod(h, 2)
        lane0 = p * LANES + (0 if odd else C_HEAD_DIM)
        for piece in range(3):
            onek[0, lane0 + piece] = 1.0
            selk[piece * LANES + h, lane0 + 3 + piece] = -1.0
            selq[h * FOX_AUG + piece, piece * LANES + h] = 1.0
            oneq[h * FOX_AUG + 3 + piece, :] = 1.0
    return selk, onek, selq, oneq


def _odd_proj_kernel(x_ref, w_ref, bf_ref, qg_ref, kg_ref, gmat_ref, tri_ref,
                     selk_ref, onek_ref, selq_ref, oneq_ref,
                     qt_o, k_o, vt_o, g_o, carry_ref):
    @pl.when(pl.program_id(1) == 0)
    def _():
        carry_ref[...] = jnp.zeros_like(carry_ref)

    xb = x_ref[...].astype(BF16)
    tm = xb.shape[0]
    ch = MXU_DIM
    hpc = ch // C_HEAD_DIM
    gmat = gmat_ref[...]

    def headnorm(acc, g):
        ms = jnp.dot((acc * acc).astype(BF16), gmat, preferred_element_type=F32)
        return acc * lax.rsqrt(ms + 1e-6) * g

    zero_pad = jnp.zeros((FOX_PAD, tm), BF16)
    ones = jnp.ones((FOX_VROWS - C_HEAD_DIM, tm), BF16)
    for j in range(D_MODEL // ch):
        acc = jnp.dot(xb, w_ref[:, j * ch:(j + 1) * ch], preferred_element_type=F32)
        qt = (headnorm(acc, qg_ref[...]) * (C_HEAD_DIM ** -0.5 * LOG2E)).T.astype(BF16)
        acc = jnp.dot(xb, w_ref[:, 2 * D_MODEL + j * ch:2 * D_MODEL + (j + 1) * ch],
                      preferred_element_type=F32)
        vt = acc.T.astype(BF16)
        for i in range(hpc):
            h = j * hpc + i
            src = slice(i * C_HEAD_DIM, (i + 1) * C_HEAD_DIM)
            val0, _, zero0 = _fox_slots(h)
            qt_o[val0:val0 + C_HEAD_DIM, :] = qt[src, :]
            qt_o[zero0:zero0 + FOX_PAD, :] = zero_pad
            vt_o[h * FOX_VROWS:h * FOX_VROWS + C_HEAD_DIM, :] = vt[src, :]
            vt_o[h * FOX_VROWS + C_HEAD_DIM:(h + 1) * FOX_VROWS, :] = ones
        acc = jnp.dot(xb, w_ref[:, D_MODEL + j * ch:D_MODEL + (j + 1) * ch], preferred_element_type=F32)
        kn = headnorm(acc, kg_ref[...]).astype(BF16)
        for i in range(hpc):
            val0 = _fox_slots(j * hpc + i)[0]
            k_o[:, val0:val0 + C_HEAD_DIM] = kn[:, i * C_HEAD_DIM:(i + 1) * C_HEAD_DIM]
        acc = jnp.dot(xb, w_ref[:, 3 * D_MODEL + j * ch:3 * D_MODEL + (j + 1) * ch],
                      preferred_element_type=F32)
        g_o[:, j * ch:(j + 1) * ch] = acc.astype(g_o.dtype)

    fl = jnp.dot(xb, w_ref[:, 4 * D_MODEL:4 * D_MODEL + LANES], preferred_element_type=F32) + bf_ref[...]
    logf = jnp.minimum(fl, 0.0) - jnp.log(1.0 + jnp.exp(-jnp.abs(fl)))
    c = _sum3(jnp.dot(tri_ref[...], _split3(logf), preferred_element_type=F32)) + carry_ref[...]
    carry_ref[...] = c[tm - 1:tm, :]
    c3 = _split3(c * LOG2E)
    aug_k = (jnp.dot(c3, selk_ref[...], preferred_element_type=F32) + onek_ref[...]).astype(BF16)
    half = LANES // 2
    for p in range(C_HEADS // 2):
        k_o[:, 2 * p * LANES + half:2 * p * LANES + LANES] = aug_k[:, p * LANES + half:(p + 1) * LANES]
        k_o[:, (2 * p + 1) * LANES:(2 * p + 1) * LANES + half] = aug_k[:, p * LANES:p * LANES + half]
    aug_q = lax.dot_general(selq_ref[...], c3, (((1,), (1,)), ((), ())), preferred_element_type=F32)
    aug_q = (aug_q + jnp.concatenate([oneq_ref[...]] * (tm // LANES), axis=1)).astype(BF16)
    for h in range(C_HEADS):
        aug0 = _fox_slots(h)[1]
        qt_o[aug0:aug0 + FOX_AUG, :] = aug_q[h * FOX_AUG:(h + 1) * FOX_AUG, :]


def _odd_proj(x2, w_bf, bf_pad, qg, kg, gmat, tri, B, S):
    T = B * S
    tm = PROJ_TM
    nt = S // tm
    tok = lambda n: pl.BlockSpec((tm, n), lambda b, t: (b * nt + t, 0))
    tspec = lambda rows: pl.BlockSpec((None, rows, tm), lambda b, t: (b, 0, t))
    selk, onek, selq, oneq = _fox_selectors()
    consts = [jnp.asarray(selk, BF16), jnp.asarray(onek, F32), jnp.asarray(selq, BF16),
              jnp.asarray(oneq, F32)]
    k_width = C_HEADS * LANES
    return pl.pallas_call(
        _odd_proj_kernel,
        grid=(B, nt),
        in_specs=[tok(D_MODEL), _full_spec(w_bf.shape), _full_spec(bf_pad.shape), _full_spec(qg.shape),
                  _full_spec(kg.shape), _full_spec(gmat.shape), _full_spec(tri.shape)]
                 + [_full_spec(c.shape) for c in consts],
        out_specs=[tspec(k_width), tok(k_width), tspec(C_HEADS * FOX_VROWS), tok(D_MODEL)],
        out_shape=[jax.ShapeDtypeStruct((B, k_width, S), BF16),
                   jax.ShapeDtypeStruct((T, k_width), BF16),
                   jax.ShapeDtypeStruct((B, C_HEADS * FOX_VROWS, S), BF16),
                   jax.ShapeDtypeStruct((T, D_MODEL), BF16)],
        scratch_shapes=[pltpu.VMEM((1, LANES), F32)],
        compiler_params=_cparams(("parallel", "arbitrary")),
        name="odd_proj",
    )(x2, w_bf, bf_pad, qg, kg, gmat, tri, *consts)


def _fox_kernel(qi_ref, ki_ref, q_ref, k_ref, v_ref, o_ref, ma, aa, mb, ab):
    step = pl.program_id(2)
    qi = qi_ref[step]
    ki = ki_ref[step]

    @pl.when(ki == 0)
    def _():
        _init_stats(ma, aa)
        _init_stats(mb, ab)

    def run(mask_fn):
        _attend(k_ref, v_ref, q_ref, [(0, ma, aa, 0, 0), (LANES, mb, ab, FOX_VROWS, LANES)], mask_fn)

    def mask_for(key_offset):
        def chunk_mask(k0, q0, klen):
            k0 = k0 + key_offset
            if k0 + klen <= q0 + 1:
                return "all"
            if k0 >= q0 + ATT_QC:
                return "none"
            kk = lax.broadcasted_iota(jnp.int32, (klen, ATT_QC), 0) + k0
            qq = lax.broadcasted_iota(jnp.int32, (klen, ATT_QC), 1) + q0
            return kk <= qq
        return chunk_mask

    def finalize():
        d = C_HEAD_DIM
        o = jnp.concatenate([aa[:d, :] / aa[d:d + 1, :], ab[:d, :] / ab[d:d + 1, :]], axis=0)
        o_ref[...] = o.T.astype(o_ref.dtype)

    _attend_tile(qi, ki, run, mask_for, finalize)


def _fox_attention(qt, k, vt, B, S):
    tq, tk = ATT_TQ, ATT_TK
    nq, nk = S // tq, S // tk
    qi, ki = _tri_schedule(nq)
    n_pairs = C_HEADS // 2
    stat = pltpu.VMEM((1, tq), F32)
    acc = pltpu.VMEM((FOX_VROWS, tq), F32)
    grid_spec = pltpu.PrefetchScalarGridSpec(
        num_scalar_prefetch=2,
        grid=(B, n_pairs, int(qi.shape[0])),
        in_specs=[
            pl.BlockSpec((None, 2 * LANES, tq), lambda b, h, s, qi, ki: (b, h, qi[s])),
            pl.BlockSpec((tk, 2 * LANES), lambda b, h, s, qi, ki: (b * nk + ki[s], h)),
            pl.BlockSpec((None, 2 * FOX_VROWS, tk), lambda b, h, s, qi, ki: (b, h, ki[s])),
        ],
        out_specs=pl.BlockSpec((tq, LANES), lambda b, h, s, qi, ki: (b * nq + qi[s], h)),
        scratch_shapes=[stat, acc, stat, acc],
    )
    return pl.pallas_call(
        _fox_kernel,
        grid_spec=grid_spec,
        out_shape=jax.ShapeDtypeStruct((B * S, D_MODEL), BF16),
        compiler_params=_cparams(("parallel", "parallel", "arbitrary")),
        name="fox_attn",
    )(qi, ki, qt, k, vt)


def _even_mix(oa_ref, ob_ref, w_ref):
    h = jnp.dot(oa_ref[...], w_ref[:A_WIDTH, :], preferred_element_type=F32)
    return h + jnp.dot(ob_ref[...], w_ref[A_WIDTH:, :], preferred_element_type=F32)


def _odd_mix(o_ref, gate_ref, w_ref):
    o = o_ref[...].astype(F32) * jax.nn.sigmoid(gate_ref[...].astype(F32))
    return jnp.dot(o.astype(BF16), w_ref[...], preferred_element_type=F32)


def _tail_kernel(a0_ref, a1_ref, x_ref, wo_ref, g1_ref, b1_ref, w1_ref, w2_ref, g2_ref, b2_ref,
                 y_ref, act_ref, *, mix):
    x = _layer_norm_rows(ALPHA * x_ref[...] + mix(a0_ref, a1_ref, wo_ref), g1_ref[...], b1_ref[...])
    xb = x.astype(BF16)
    ch = MXU_DIM
    for j in range(D_FF // ch):
        gate = jnp.dot(xb, w1_ref[:, j * ch:(j + 1) * ch], preferred_element_type=F32)
        up = jnp.dot(xb, w1_ref[:, D_FF + j * ch:D_FF + (j + 1) * ch], preferred_element_type=F32)
        act_ref[:, j * ch:(j + 1) * ch] = (gate * jax.nn.sigmoid(gate) * up).astype(BF16)
    h = jnp.dot(act_ref[...], w2_ref[...], preferred_element_type=F32)
    y_ref[...] = _layer_norm_rows(ALPHA * x + h, g2_ref[...], b2_ref[...])


def _layer_tail(mix, acts, x2, wo_bf, g1, b1, w1_bf, w2_bf, g2, b2, name):
    T = x2.shape[0]
    tm = FFN_TM
    tok = lambda n: pl.BlockSpec((tm, n), lambda i: (i, 0))
    consts = [wo_bf, g1, b1, w1_bf, w2_bf, g2, b2]
    return pl.pallas_call(
        functools.partial(_tail_kernel, mix=mix),
        grid=(T // tm,),
        in_specs=[tok(a.shape[1]) for a in acts] + [tok(D_MODEL)] + [_full_spec(c.shape) for c in consts],
        out_specs=tok(D_MODEL),
        out_shape=jax.ShapeDtypeStruct((T, D_MODEL), F32),
        scratch_shapes=[pltpu.VMEM((tm, D_FF), BF16)],
        compiler_params=_cparams(("parallel",)),
        name=name,
    )(*acts, x2, *consts)


def kernel(x, even_w_in, even_w_out, hgrn_lb_logits, diff_lq1, diff_lk1, diff_lq2, diff_lk2,
           hgrn_norm_g, diff_norm_g, fox_w_in, fox_w_out, fox_b_f, fox_qnorm_g, fox_knorm_g,
           ffn_w1, ffn_w2, ln1_g, ln1_b, ln2_g, ln2_b):
    B, S, D = x.shape
    assert D == D_MODEL and S % ATT_TQ == 0 and S % PROJ_TM == 0 and S % HGRN_L == 0
    T = B * S
    x2 = x.reshape(T, D).astype(F32)

    cmat = jnp.asarray(_hgrn_cumsum_matrix(HGRN_L), BF16)
    tri = jnp.asarray(np.tril(np.ones((PROJ_TM, PROJ_TM), np.float32)), BF16)
    head_of = np.arange(MXU_DIM) // C_HEAD_DIM
    gmat = jnp.asarray((head_of[:, None] == head_of[None, :]).astype(np.float32) / C_HEAD_DIM, BF16)
    row = lambda v: v.astype(F32).reshape(1, -1)

    for l in range(DEPTH):
        j = l // 2
        if l % 2 == 0:
            aq, af, ai, ag, bq, bk, bv = _even_proj(x2, even_w_in[j].astype(BF16), B, S)
            o_a = _hgrn(aq, af, ai, ag, hgrn_lb_logits.astype(F32), row(hgrn_norm_g[j]), cmat, j, B, S)
            lamv = jnp.zeros((8, B_HEAD_DIM), F32).at[0:4].set(
                jnp.stack([diff_lq1[j], diff_lk1[j], diff_lq2[j], diff_lk2[j]]).astype(F32))
            lam_init = 0.8 - 0.6 * math.exp(-0.3 * l)
            o_b = _diff_attention(bq, bk, bv, lamv, row(diff_norm_g[j]), lam_init, B, S)
            mix, acts, w_out, name = _even_mix, [o_a, o_b], even_w_out[j], "even_tail"
        else:
            w_pad = jnp.pad(fox_w_in[j], ((0, 0), (0, LANES - C_HEADS))).astype(BF16)
            bf_pad = jnp.pad(fox_b_f[j].astype(F32), (0, LANES - C_HEADS)).reshape(1, LANES)
            tile4 = lambda v: jnp.tile(v.astype(F32), MXU_DIM // C_HEAD_DIM).reshape(1, MXU_DIM)
            qt, k, vt, g = _odd_proj(x2, w_pad, bf_pad, tile4(fox_qnorm_g[j]),
                                     tile4(fox_knorm_g[j]), gmat, tri, B, S)
            o = _fox_attention(qt, k, vt, B, S)
            mix, acts, w_out, name = _odd_mix, [o, g], fox_w_out[j], "odd_tail"
        x2 = _layer_tail(mix, acts, x2, w_out.astype(BF16), row(ln1_g[l]), row(ln1_b[l]),
                         ffn_w1[l].astype(BF16), ffn_w2[l].astype(BF16), row(ln2_g[l]), row(ln2_b[l]), name)
    return x2.reshape(B, S, D).astype(x.dtype)
```

```python
import functools
import math

import numpy as np
import jax
import jax.numpy as jnp
from jax import lax
from jax.experimental import pallas as pl
from jax.experimental.pallas import tpu as pltpu

F32 = jnp.float32
BF16 = jnp.bfloat16

D_MODEL = 1024
DEPTH = 4
A_WIDTH = 512
A_HEADS = 4
A_DK = 128
B_WIDTH = 512
B_HEADS = 4
B_HEAD_DIM = 64
DIFF_CHUNK = 64
C_HEADS = 16
C_HEAD_DIM = 64
D_FF = 2816
ALPHA = (2 * DEPTH) ** 0.25
MASK_VALUE = -1e30

LANES = 128
MXU_DIM = 256

PROJ_TM = 512
FFN_TM = 512
ATT_TQ = 2048
ATT_TK = 2048
ATT_QC = 256
ATT_KC = 512
ATT_SUB = 64
ATT_AHEAD = 2
ATT_BEHIND = 2
VT_ROWS = LANES + 16
HGRN_L = 256
LOG2E = math.log2(math.e)
VMEM_LIMIT = 56 * 1024 * 1024


def _cparams(sem):
    return pltpu.CompilerParams(dimension_semantics=sem, vmem_limit_bytes=VMEM_LIMIT)


def _full_spec(shape):
    nd = len(shape)
    return pl.BlockSpec(shape, lambda *_: (0,) * nd)


def _split3(x):
    h = x.astype(BF16)
    r = x - h.astype(F32)
    m = r.astype(BF16)
    l = (r - m.astype(F32)).astype(BF16)
    return jnp.concatenate([h, m, l], axis=1)


def _sum3(y):
    return y[:, :LANES] + y[:, LANES:2 * LANES] + y[:, 2 * LANES:]


def _layer_norm_rows(y, g, b):
    mu = jnp.mean(y, axis=-1, keepdims=True)
    d = y - mu
    var = jnp.mean(d * d, axis=-1, keepdims=True)
    return d * lax.rsqrt(var + 1e-5) * g + b


def _even_proj_kernel(x_ref, w_ref, aq, af, ai, ag, bq, bk, bv):
    xb = x_ref[...].astype(BF16)
    outs = (aq, af, ai, ag, bq, bk, bv)
    for n, o in enumerate(outs):
        acc = jnp.dot(xb, w_ref[:, n * 512:(n + 1) * 512], preferred_element_type=F32)
        if o is bq:
            qt = (acc * (B_HEAD_DIM ** -0.5 * LOG2E)).T.astype(o.dtype)
            zero = jnp.zeros((B_HEAD_DIM, qt.shape[1]), o.dtype)
            for h in range(B_HEADS):
                r = 2 * h * LANES
                o[r:r + B_HEAD_DIM, :] = qt[h * LANES:h * LANES + B_HEAD_DIM, :]
                o[r + B_HEAD_DIM:r + LANES, :] = zero
                o[r + LANES:r + LANES + B_HEAD_DIM, :] = zero
                o[r + LANES + B_HEAD_DIM:r + 2 * LANES, :] = qt[h * LANES + B_HEAD_DIM:(h + 1) * LANES, :]
        elif o is bv:
            vt = acc.T.astype(o.dtype)
            ones = jnp.ones((VT_ROWS - LANES, vt.shape[1]), o.dtype)
            for h in range(B_HEADS):
                o[h * VT_ROWS:h * VT_ROWS + LANES, :] = vt[h * LANES:(h + 1) * LANES, :]
                o[h * VT_ROWS + LANES:(h + 1) * VT_ROWS, :] = ones
        else:
            o[...] = acc.astype(o.dtype)


def _even_proj(x2, w_bf, B, S):
    T = x2.shape[0]
    tm = PROJ_TM
    nt = S // tm
    tok = lambda dt: jax.ShapeDtypeStruct((T, 512), dt)
    spec = pl.BlockSpec((tm, 512), lambda i: (i, 0))
    tspec = lambda rows: pl.BlockSpec((None, rows, tm), lambda i: (i // nt, 0, i % nt))
    return pl.pallas_call(
        _even_proj_kernel,
        grid=(T // tm,),
        in_specs=[pl.BlockSpec((tm, D_MODEL), lambda i: (i, 0)), _full_spec(w_bf.shape)],
        out_specs=[spec, spec, spec, spec, tspec(2 * B_WIDTH), spec, tspec(B_HEADS * VT_ROWS)],
        out_shape=[tok(BF16), tok(F32), tok(BF16), tok(BF16),
                   jax.ShapeDtypeStruct((B, 2 * B_WIDTH, S), BF16), tok(BF16),
                   jax.ShapeDtypeStruct((B, B_HEADS * VT_ROWS, S), BF16)],
        compiler_params=_cparams(("parallel",)),
        name="even_proj",
    )(x2, w_bf)


def _hgrn_cumsum_matrix(L):
    idx = np.arange(L)
    mats = []
    c = 1
    while c < L:
        start = (idx // c) * c
        end = start + c - 1
        right = ((idx // c) % 2) == 1
        u = idx[None, :]
        m_right = (u >= start[:, None]) & (u <= idx[:, None])
        m_left = (u > idx[:, None]) & (u <= end[:, None])
        mats.append(np.where(right[:, None], m_right, m_left))
        c *= 2
    u = idx[None, :]
    mats.append(u <= idx[:, None])
    mats.append(u > idx[:, None])
    return np.concatenate(mats, axis=0).astype(np.float32)


def _split2(x):
    h = x.astype(BF16)
    m = (x - h.astype(F32)).astype(BF16)
    return jnp.concatenate([h, m], axis=1)


def _hgrn_kernel(q_ref, f_ref, i_ref, g_ref, lbl_ref, gn_ref, cmat_ref, o_ref, state_ref, *, layer_j):
    L = HGRN_L
    n_levels = int(math.log2(L))
    nt = (((1,), (1,)), ((), ()))

    @pl.when(pl.program_id(1) == 0)
    def _():
        state_ref[...] = jnp.zeros_like(state_ref)

    lbl = lbl_ref[...]
    e = jnp.exp(lbl - jnp.max(lbl, axis=0, keepdims=True))
    soft = e / jnp.sum(e, axis=0, keepdims=True)
    lb_all = jnp.sum(soft[:layer_j + 1], axis=0, keepdims=True) - soft[0:1]

    row = lax.broadcasted_iota(jnp.int32, (L, 1), 0)
    xor = lax.broadcasted_iota(jnp.int32, (L, L), 0) ^ lax.broadcasted_iota(jnp.int32, (L, L), 1)
    cmat = cmat_ref[...]

    heads = range(A_HEADS)
    cols = [slice(h * LANES, (h + 1) * LANES) for h in heads]
    q, k, gdec = [], [], []
    for h in heads:
        lb = lb_all[:, cols[h]]
        f = lb + (1.0 - lb) * jax.nn.sigmoid(f_ref[:, cols[h]])
        k.append(1.0 - f)
        qr = q_ref[:, cols[h]].astype(F32)
        q.append(qr * jax.nn.sigmoid(qr))
        ex = jnp.dot(cmat, _split2(jnp.log(f)), preferred_element_type=F32)
        gdec.append(jnp.exp(ex[:, :LANES] + ex[:, LANES:]))

    scores = []
    for h in heads:
        s = jnp.where(xor == 0, lax.dot_general(q[h].astype(BF16), k[h].astype(BF16), nt,
                                                preferred_element_type=F32), 0.0)
        for lvl in range(n_levels):
            g_l = gdec[h][lvl * L:(lvl + 1) * L, :]
            right = ((row >> lvl) & 1).astype(F32)
            ql = (q[h] * g_l * right).astype(BF16)
            kl = (k[h] * g_l * (1.0 - right)).astype(BF16)
            s_l = lax.dot_general(ql, kl, nt, preferred_element_type=F32)
            s = s + jnp.where(xor < (2 << lvl), s_l, 0.0)
        scores.append(s.astype(BF16))

    for h in heads:
        iv = i_ref[:, cols[h]]
        g_full = gdec[h][n_levels * L:(n_levels + 1) * L, :]
        g_rest = gdec[h][(n_levels + 1) * L:(n_levels + 2) * L, :]
        state_t = state_ref[h]
        o = lax.dot_general((q[h] * g_full).astype(BF16), state_t.astype(BF16), nt,
                            preferred_element_type=F32)
        o = o + jnp.dot(scores[h], iv, preferred_element_type=F32)
        k_out = (k[h] * g_rest).astype(BF16)
        upd = jnp.dot(iv.astype(F32).T.astype(BF16), k_out, preferred_element_type=F32)
        state_ref[h] = state_t * g_full[L - 1:L, :] + upd
        ms = jnp.mean(o * o, axis=-1, keepdims=True)
        gate = g_ref[:, cols[h]].astype(F32)
        o = o * lax.rsqrt(ms + 1e-6) * gn_ref[...] * (gate * jax.nn.sigmoid(gate))
        o_ref[:, cols[h]] = o.astype(o_ref.dtype)


def _hgrn(aq, af, ai, ag, lb_logits, gn, cmat, layer_j, B, S):
    L = HGRN_L
    nb = S // L
    blk = pl.BlockSpec((L, A_WIDTH), lambda b, t: (b * nb + t, 0))
    return pl.pallas_call(
        functools.partial(_hgrn_kernel, layer_j=layer_j),
        grid=(B, nb),
        in_specs=[blk, blk, blk, blk, _full_spec(lb_logits.shape),
                  _full_spec(gn.shape), _full_spec(cmat.shape)],
        out_specs=blk,
        out_shape=jax.ShapeDtypeStruct((B * S, A_WIDTH), BF16),
        scratch_shapes=[pltpu.VMEM((A_HEADS, LANES, LANES), F32)],
        compiler_params=_cparams(("parallel", "arbitrary")),
        name="hgrn2",
    )(aq, af, ai, ag, lb_logits, gn, cmat)


def _attend(k_ref, vt_ref, qt_ref, streams, mask_fn):
    tk = k_ref.shape[0]
    kd = LANES
    tq = qt_ref.shape[1]
    def verdict(mask, word):
        return isinstance(mask, str) and mask == word

    chains = []
    for kc in range(0, tk, ATT_KC):
        for st in streams:
            for qc in range(0, tq, ATT_QC):
                n_keys, mask = ATT_KC, None
                if mask_fn is not None:
                    half = ATT_KC // 2
                    if verdict(mask_fn(kc + half, qc, half), "none"):
                        n_keys = half
                    mask = mask_fn(kc, qc, n_keys)
                    if verdict(mask, "none"):
                        continue
                    if verdict(mask, "all"):
                        mask = None
                chains.append((st, qc, kc, n_keys, mask))

    def scores(st, qc, kc, n_keys, mask):
        s = jnp.dot(k_ref[kc:kc + n_keys, st[4]:st[4] + kd], qt_ref[st[0]:st[0] + kd, qc:qc + ATT_QC],
                    preferred_element_type=F32)
        if mask is not None:
            s = jnp.where(mask, s, MASK_VALUE)
        return s

    def softmax(st, qc, s):
        m_ref = st[1]
        cols = slice(qc, qc + ATT_QC)
        m_prev = m_ref[:, cols]
        m_run, parts, maxes = m_prev, [], []
        for r0 in range(0, s.shape[0], ATT_SUB):
            s_g = s[r0:r0 + ATT_SUB, :]
            m_run = jnp.maximum(m_run, jnp.max(s_g, axis=0, keepdims=True))
            parts.append(jnp.exp2((s_g - m_run).astype(BF16)))
            maxes.append(m_run)
        m_new = m_run
        parts = [p_g if m_g is m_new else p_g * jnp.exp2(m_g - m_new).astype(BF16)
                 for p_g, m_g in zip(parts, maxes)]
        p = parts[0] if len(parts) == 1 else jnp.concatenate(parts, axis=0)
        alpha = jnp.exp2(m_prev - m_new)
        m_ref[:, cols] = m_new
        return p, alpha

    def weighted_values(st, qc, kc, p, alpha):
        acc_ref, v0 = st[2], st[3]
        cols = slice(qc, qc + ATT_QC)
        acc_ref[:, cols] = alpha * acc_ref[:, cols] + jnp.dot(
            vt_ref[v0:v0 + acc_ref.shape[0], kc:kc + p.shape[0]], p, preferred_element_type=F32)

    n = len(chains)
    ready = [scores(*c) for c in chains[:ATT_AHEAD]]
    pending = []
    for i, (st, qc, kc, _, _) in enumerate(chains):
        s = ready.pop(0)
        if i + ATT_AHEAD < n:
            ready.append(scores(*chains[i + ATT_AHEAD]))
        pending.append((st, qc, kc, *softmax(st, qc, s)))
        if len(pending) > ATT_BEHIND:
            weighted_values(*pending.pop(0))
    for item in pending:
        weighted_values(*item)


ATT_MULT = ATT_TQ // ATT_TK


def _tri_schedule(nq):
    qi = [i for i in range(nq) for _ in range((i + 1) * ATT_MULT)]
    ki = [j for i in range(nq) for j in range((i + 1) * ATT_MULT)]
    return jnp.asarray(qi, jnp.int32), jnp.asarray(ki, jnp.int32)


def _attend_tile(qi, ki, run, mask_for, finalize):
    rel = ki - qi * ATT_MULT

    @pl.when(rel < 0)
    def _():
        run(None)

    for r in range(ATT_MULT):
        @pl.when(rel == r)
        def _(r=r):
            run(mask_for(r * ATT_TK))
            if r == ATT_MULT - 1:
                finalize()


def _init_stats(m_ref, acc_ref):
    m_ref[...] = jnp.full_like(m_ref, MASK_VALUE)
    acc_ref[...] = jnp.zeros_like(acc_ref)


def _diff_kernel(qi_ref, ki_ref, q_ref, k_ref, v_ref, lam_ref, gn_ref, o_ref,
                 m1, a1, m2, a2, *, lam_init):
    step = pl.program_id(2)
    qi = qi_ref[step]
    ki = ki_ref[step]

    @pl.when(ki == 0)
    def _():
        _init_stats(m1, a1)
        _init_stats(m2, a2)

    def run(mask_fn):
        _attend(k_ref, v_ref, q_ref, [(0, m1, a1, 0, 0), (LANES, m2, a2, 0, 0)], mask_fn)

    def mask_for(key_offset):
        def chunk_mask(k0, q0, klen):
            k0 = k0 + key_offset
            if k0 + klen <= q0 + DIFF_CHUNK:
                return "all"
            if k0 >= q0 + ATT_QC:
                return "none"
            sh = DIFF_CHUNK.bit_length() - 1
            kk = (lax.broadcasted_iota(jnp.int32, (klen, ATT_QC), 0) + k0) >> sh
            qq = (lax.broadcasted_iota(jnp.int32, (klen, ATT_QC), 1) + q0) >> sh
            return kk <= qq
        return chunk_mask

    def finalize():
        lv = lam_ref[...]
        lam = (jnp.exp(jnp.sum(lv[0:1] * lv[1:2], axis=-1, keepdims=True))
               - jnp.exp(jnp.sum(lv[2:3] * lv[3:4], axis=-1, keepdims=True)) + lam_init)
        o = (a1[:LANES, :] / a1[LANES:LANES + 1, :]
             - lam * (a2[:LANES, :] / a2[LANES:LANES + 1, :])).T
        ms = jnp.mean(o * o, axis=-1, keepdims=True)
        o = o * lax.rsqrt(ms + 1e-6) * gn_ref[...] * (1.0 - lam_init)
        o_ref[...] = o.astype(o_ref.dtype)

    _attend_tile(qi, ki, run, mask_for, finalize)


def _diff_attention(bq, bk, bv, lamv, gn, lam_init, B, S):
    tq, tk = ATT_TQ, ATT_TK
    nq, nk = S // tq, S // tk
    qi, ki = _tri_schedule(nq)
    stat = pltpu.VMEM((1, tq), F32)
    acc = pltpu.VMEM((VT_ROWS, tq), F32)
    grid_spec = pltpu.PrefetchScalarGridSpec(
        num_scalar_prefetch=2,
        grid=(B, B_HEADS, int(qi.shape[0])),
        in_specs=[
            pl.BlockSpec((None, 2 * LANES, tq), lambda b, h, s, qi, ki: (b, h, qi[s])),
            pl.BlockSpec((tk, LANES), lambda b, h, s, qi, ki: (b * nk + ki[s], h)),
            pl.BlockSpec((None, VT_ROWS, tk), lambda b, h, s, qi, ki: (b, h, ki[s])),
            pl.BlockSpec(lamv.shape, lambda b, h, s, qi, ki: (0, 0)),
            pl.BlockSpec(gn.shape, lambda b, h, s, qi, ki: (0, 0)),
        ],
        out_specs=pl.BlockSpec((tq, LANES), lambda b, h, s, qi, ki: (b * nq + qi[s], h)),
        scratch_shapes=[stat, acc, stat, acc],
    )
    return pl.pallas_call(
        functools.partial(_diff_kernel, lam_init=lam_init),
        grid_spec=grid_spec,
        out_shape=jax.ShapeDtypeStruct((B * S, B_WIDTH), BF16),
        compiler_params=_cparams(("parallel", "parallel", "arbitrary")),
        name="diff_attn",
    )(qi, ki, bq, bk, bv, lamv, gn)


FOX_AUG = 16
FOX_VROWS = C_HEAD_DIM + 16
FOX_PAD = LANES - C_HEAD_DIM - FOX_AUG


def _fox_slots(h):
    base = (h // 2) * 2 * LANES
    if h % 2 == 0:
        return base, base + C_HEAD_DIM, base + C_HEAD_DIM + FOX_AUG
    return base + 2 * LANES - C_HEAD_DIM, base + LANES, base + LANES + FOX_AUG


def _fox_selectors():
    n_pairs = C_HEADS // 2
    selk = np.zeros((3 * LANES, n_pairs * LANES), np.float32)
    onek = np.zeros((1, n_pairs * LANES), np.float32)
    selq = np.zeros((C_HEADS * FOX_AUG, 3 * LANES), np.float32)
    oneq = np.zeros((C_HEADS * FOX_AUG, LANES), np.float32)
    for h in range(C_HEADS):
        p, odd = divmod(h, 2)
        lane0 = p * LANES + (0 if odd else C_HEAD_DIM)
        for piece in range(3):
            onek[0, lane0 + piece] = 1.0
            selk[piece * LANES + h, lane0 + 3 + piece] = -1.0
            selq[h * FOX_AUG + piece, piece * LANES + h] = 1.0
            oneq[h * FOX_AUG + 3 + piece, :] = 1.0
    return selk, onek, selq, oneq


def _odd_proj_kernel(x_ref, w_ref, bf_ref, qg_ref, kg_ref, gmat_ref, tri_ref,
                     selk_ref, onek_ref, selq_ref, oneq_ref,
                     qt_o, k_o, vt_o, g_o, carry_ref):
    @pl.when(pl.program_id(1) == 0)
    def _():
        carry_ref[...] = jnp.zeros_like(carry_ref)

    xb = x_ref[...].astype(BF16)
    tm = xb.shape[0]
    ch = MXU_DIM
    hpc = ch // C_HEAD_DIM
    gmat = gmat_ref[...]

    def headnorm(acc, g):
        ms = jnp.dot((acc * acc).astype(BF16), gmat, preferred_element_type=F32)
        return acc * lax.rsqrt(ms + 1e-6) * g

    zero_pad = jnp.zeros((FOX_PAD, tm), BF16)
    ones = jnp.ones((FOX_VROWS - C_HEAD_DIM, tm), BF16)
    for j in range(D_MODEL // ch):
        acc = jnp.dot(xb, w_ref[:, j * ch:(j + 1) * ch], preferred_element_type=F32)
        qt = (headnorm(acc, qg_ref[...]) * (C_HEAD_DIM ** -0.5 * LOG2E)).T.astype(BF16)
        acc = jnp.dot(xb, w_ref[:, 2 * D_MODEL + j * ch:2 * D_MODEL + (j + 1) * ch],
                      preferred_element_type=F32)
        vt = acc.T.astype(BF16)
        for i in range(hpc):
            h = j * hpc + i
            src = slice(i * C_HEAD_DIM, (i + 1) * C_HEAD_DIM)
            val0, _, zero0 = _fox_slots(h)
            qt_o[val0:val0 + C_HEAD_DIM, :] = qt[src, :]
            qt_o[zero0:zero0 + FOX_PAD, :] = zero_pad
            vt_o[h * FOX_VROWS:h * FOX_VROWS + C_HEAD_DIM, :] = vt[src, :]
            vt_o[h * FOX_VROWS + C_HEAD_DIM:(h + 1) * FOX_VROWS, :] = ones
        acc = jnp.dot(xb, w_ref[:, D_MODEL + j * ch:D_MODEL + (j + 1) * ch], preferred_element_type=F32)
        kn = headnorm(acc, kg_ref[...]).astype(BF16)
        for i in range(hpc):
            val0 = _fox_slots(j * hpc + i)[0]
            k_o[:, val0:val0 + C_HEAD_DIM] = kn[:, i * C_HEAD_DIM:(i + 1) * C_HEAD_DIM]
        acc = jnp.dot(xb, w_ref[:, 3 * D_MODEL + j * ch:3 * D_MODEL + (j + 1) * ch],
                      preferred_element_type=F32)
        g_o[:, j * ch:(j + 1) * ch] = acc.astype(g_o.dtype)

    fl = jnp.dot(xb, w_ref[:, 4 * D_MODEL:4 * D_MODEL + LANES], preferred_element_type=F32) + bf_ref[...]
    logf = jnp.minimum(fl, 0.0) - jnp.log(1.0 + jnp.exp(-jnp.abs(fl)))
    c = _sum3(jnp.dot(tri_ref[...], _split3(logf), preferred_element_type=F32)) + carry_ref[...]
    carry_ref[...] = c[tm - 1:tm, :]
    c3 = _split3(c * LOG2E)
    aug_k = (jnp.dot(c3, selk_ref[...], preferred_element_type=F32) + onek_ref[...]).astype(BF16)
    half = LANES // 2
    for p in range(C_HEADS // 2):
        k_o[:, 2 * p * LANES + half:2 * p * LANES + LANES] = aug_k[:, p * LANES + half:(p + 1) * LANES]
        k_o[:, (2 * p + 1) * LANES:(2 * p + 1) * LANES + half] = aug_k[:, p * LANES:p * LANES + half]
    aug_q = lax.dot_general(selq_ref[...], c3, (((1,), (1,)), ((), ())), preferred_element_type=F32)
    aug_q = (aug_q + jnp.concatenate([oneq_ref[...]] * (tm // LANES), axis=1)).astype(BF16)
    for h in range(C_HEADS):
        aug0 = _fox_slots(h)[1]
        qt_o[aug0:aug0 + FOX_AUG, :] = aug_q[h * FOX_AUG:(h + 1) * FOX_AUG, :]


def _odd_proj(x2, w_bf, bf_pad, qg, kg, gmat, tri, B, S):
    T = B * S
    tm = PROJ_TM
    nt = S // tm
    tok = lambda n: pl.BlockSpec((tm, n), lambda b, t: (b * nt + t, 0))
    tspec = lambda rows: pl.BlockSpec((None, rows, tm), lambda b, t: (b, 0, t))
    selk, onek, selq, oneq = _fox_selectors()
    consts = [jnp.asarray(selk, BF16), jnp.asarray(onek, F32), jnp.asarray(selq, BF16),
              jnp.asarray(oneq, F32)]
    k_width = C_HEADS * LANES
    return pl.pallas_call(
        _odd_proj_kernel,
        grid=(B, nt),
        in_specs=[tok(D_MODEL), _full_spec(w_bf.shape), _full_spec(bf_pad.shape), _full_spec(qg.shape),
                  _full_spec(kg.shape), _full_spec(gmat.shape), _full_spec(tri.shape)]
                 + [_full_spec(c.shape) for c in consts],
        out_specs=[tspec(k_width), tok(k_width), tspec(C_HEADS * FOX_VROWS), tok(D_MODEL)],
        out_shape=[jax.ShapeDtypeStruct((B, k_width, S), BF16),
                   jax.ShapeDtypeStruct((T, k_width), BF16),
                   jax.ShapeDtypeStruct((B, C_HEADS * FOX_VROWS, S), BF16),
                   jax.ShapeDtypeStruct((T, D_MODEL), BF16)],
        scratch_shapes=[pltpu.VMEM((1, LANES), F32)],
        compiler_params=_cparams(("parallel", "arbitrary")),
        name="odd_proj",
    )(x2, w_bf, bf_pad, qg, kg, gmat, tri, *consts)


def _fox_kernel(qi_ref, ki_ref, q_ref, k_ref, v_ref, o_ref, ma, aa, mb, ab):
    step = pl.program_id(2)
    qi = qi_ref[step]
    ki = ki_ref[step]

    @pl.when(ki == 0)
    def _():
        _init_stats(ma, aa)
        _init_stats(mb, ab)

    def run(mask_fn):
        _attend(k_ref, v_ref, q_ref, [(0, ma, aa, 0, 0), (LANES, mb, ab, FOX_VROWS, LANES)], mask_fn)

    def mask_for(key_offset):
        def chunk_mask(k0, q0, klen):
            k0 = k0 + key_offset
            if k0 + klen <= q0 + 1:
                return "all"
            if k0 >= q0 + ATT_QC:
                return "none"
            kk = lax.broadcasted_iota(jnp.int32, (klen, ATT_QC), 0) + k0
            qq = lax.broadcasted_iota(jnp.int32, (klen, ATT_QC), 1) + q0
            return kk <= qq
        return chunk_mask

    def finalize():
        d = C_HEAD_DIM
        o = jnp.concatenate([aa[:d, :] / aa[d:d + 1, :], ab[:d, :] / ab[d:d + 1, :]], axis=0)
        o_ref[...] = o.T.astype(o_ref.dtype)

    _attend_tile(qi, ki, run, mask_for, finalize)


def _fox_attention(qt, k, vt, B, S):
    tq, tk = ATT_TQ, ATT_TK
    nq, nk = S // tq, S // tk
    qi, ki = _tri_schedule(nq)
    n_pairs = C_HEADS // 2
    stat = pltpu.VMEM((1, tq), F32)
    acc = pltpu.VMEM((FOX_VROWS, tq), F32)
    grid_spec = pltpu.PrefetchScalarGridSpec(
        num_scalar_prefetch=2,
        grid=(B, n_pairs, int(qi.shape[0])),
        in_specs=[
            pl.BlockSpec((None, 2 * LANES, tq), lambda b, h, s, qi, ki: (b, h, qi[s])),
            pl.BlockSpec((tk, 2 * LANES), lambda b, h, s, qi, ki: (b * nk + ki[s], h)),
            pl.BlockSpec((None, 2 * FOX_VROWS, tk), lambda b, h, s, qi, ki: (b, h, ki[s])),
        ],
        out_specs=pl.BlockSpec((tq, LANES), lambda b, h, s, qi, ki: (b * nq + qi[s], h)),
        scratch_shapes=[stat, acc, stat, acc],
    )
    return pl.pallas_call(
        _fox_kernel,
        grid_spec=grid_spec,
        out_shape=jax.ShapeDtypeStruct((B * S, D_MODEL), BF16),
        compiler_params=_cparams(("parallel", "parallel", "arbitrary")),
        name="fox_attn",
    )(qi, ki, qt, k, vt)


def _even_mix(oa_ref, ob_ref, w_ref):
    h = jnp.dot(oa_ref[...], w_ref[:A_WIDTH, :], preferred_element_type=F32)
    return h + jnp.dot(ob_ref[...], w_ref[A_WIDTH:, :], preferred_element_type=F32)


def _odd_mix(o_ref, gate_ref, w_ref):
    o = o_ref[...].astype(F32) * jax.nn.sigmoid(gate_ref[...].astype(F32))
    return jnp.dot(o.astype(BF16), w_ref[...], preferred_element_type=F32)


def _tail_kernel(a0_ref, a1_ref, x_ref, wo_ref, g1_ref, b1_ref, w1_ref, w2_ref, g2_ref, b2_ref,
                 y_ref, act_ref, *, mix):
    x = _layer_norm_rows(ALPHA * x_ref[...] + mix(a0_ref, a1_ref, wo_ref), g1_ref[...], b1_ref[...])
    xb = x.astype(BF16)
    ch = MXU_DIM
    for j in range(D_FF // ch):
        gate = jnp.dot(xb, w1_ref[:, j * ch:(j + 1) * ch], preferred_element_type=F32)
        up = jnp.dot(xb, w1_ref[:, D_FF + j * ch:D_FF + (j + 1) * ch], preferred_element_type=F32)
        act_ref[:, j * ch:(j + 1) * ch] = (gate * jax.nn.sigmoid(gate) * up).astype(BF16)
    h = jnp.dot(act_ref[...], w2_ref[...], preferred_element_type=F32)
    y_ref[...] = _layer_norm_rows(ALPHA * x + h, g2_ref[...], b2_ref[...])


def _layer_tail(mix, acts, x2, wo_bf, g1, b1, w1_bf, w2_bf, g2, b2, name):
    T = x2.shape[0]
    tm = FFN_TM
    tok = lambda n: pl.BlockSpec((tm, n), lambda i: (i, 0))
    consts = [wo_bf, g1, b1, w1_bf, w2_bf, g2, b2]
    return pl.pallas_call(
        functools.partial(_tail_kernel, mix=mix),
        grid=(T // tm,),
        in_specs=[tok(a.shape[1]) for a in acts] + [tok(D_MODEL)] + [_full_spec(c.shape) for c in consts],
        out_specs=tok(D_MODEL),
        out_shape=jax.ShapeDtypeStruct((T, D_MODEL), F32),
        scratch_shapes=[pltpu.VMEM((tm, D_FF), BF16)],
        compiler_params=_cparams(("parallel",)),
        name=name,
    )(*acts, x2, *consts)


def kernel(x, even_w_in, even_w_out, hgrn_lb_logits, diff_lq1, diff_lk1, diff_lq2, diff_lk2,
           hgrn_norm_g, diff_norm_g, fox_w_in, fox_w_out, fox_b_f, fox_qnorm_g, fox_knorm_g,
           ffn_w1, ffn_w2, ln1_g, ln1_b, ln2_g, ln2_b):
    B, S, D = x.shape
    assert D == D_MODEL and S % ATT_TQ == 0 and S % PROJ_TM == 0 and S % HGRN_L == 0
    T = B * S
    x2 = x.reshape(T, D).astype(F32)

    cmat = jnp.asarray(_hgrn_cumsum_matrix(HGRN_L), BF16)
    tri = jnp.asarray(np.tril(np.ones((PROJ_TM, PROJ_TM), np.float32)), BF16)
    head_of = np.arange(MXU_DIM) // C_HEAD_DIM
    gmat = jnp.asarray((head_of[:, None] == head_of[None, :]).astype(np.float32) / C_HEAD_DIM, BF16)
    row = lambda v: v.astype(F32).reshape(1, -1)

    for l in range(DEPTH):
        j = l // 2
        if l % 2 == 0:
            aq, af, ai, ag, bq, bk, bv = _even_proj(x2, even_w_in[j].astype(BF16), B, S)
            o_a = _hgrn(aq, af, ai, ag, hgrn_lb_logits.astype(F32), row(hgrn_norm_g[j]), cmat, j, B, S)
            lamv = jnp.zeros((8, B_HEAD_DIM), F32).at[0:4].set(
                jnp.stack([diff_lq1[j], diff_lk1[j], diff_lq2[j], diff_lk2[j]]).astype(F32))
            lam_init = 0.8 - 0.6 * math.exp(-0.3 * l)
            o_b = _diff_attention(bq, bk, bv, lamv, row(diff_norm_g[j]), lam_init, B, S)
            mix, acts, w_out, name = _even_mix, [o_a, o_b], even_w_out[j], "even_tail"
        else:
            w_pad = jnp.pad(fox_w_in[j], ((0, 0), (0, LANES - C_HEADS))).astype(BF16)
            bf_pad = jnp.pad(fox_b_f[j].astype(F32), (0, LANES - C_HEADS)).reshape(1, LANES)
            tile4 = lambda v: jnp.tile(v.astype(F32), MXU_DIM // C_HEAD_DIM).reshape(1, MXU_DIM)
            qt, k, vt, g = _odd_proj(x2, w_pad, bf_pad, tile4(fox_qnorm_g[j]),
                                     tile4(fox_knorm_g[j]), gmat, tri, B, S)
            o = _fox_attention(qt, k, vt, B, S)
            mix, acts, w_out, name = _odd_mix, [o, g], fox_w_out[j], "odd_tail"
        x2 = _layer_tail(mix, acts, x2, w_out.astype(BF16), row(ln1_g[l]), row(ln1_b[l]),
                         ffn_w1[l].astype(BF16), ffn_w2[l].astype(BF16), row(ln2_g[l]), row(ln2_b[l]), name)
    return x2.reshape(B, S, D).astype(x.dtype)
```

```python
import functools
import math

import numpy as np
import jax
import jax.numpy as jnp
from jax import lax
from jax.experimental import pallas as pl
from jax.experimental.pallas import tpu as pltpu

F32 = jnp.float32
BF16 = jnp.bfloat16

D_MODEL = 1024
DEPTH = 4
A_WIDTH = 512
A_HEADS = 4
A_DK = 128
B_WIDTH = 512
B_HEADS = 4
B_HEAD_DIM = 64
DIFF_CHUNK = 64
C_HEADS = 16
C_HEAD_DIM = 64
D_FF = 2816
ALPHA = (2 * DEPTH) ** 0.25
MASK_VALUE = -1e30

LANES = 128
MXU_DIM = 256

PROJ_TM = 512
FFN_TM = 512
ATT_TQ = 2048
ATT_TK = 2048
ATT_QC = 256
ATT_KC = 512
ATT_SUB = 64
ATT_AHEAD = 2
ATT_BEHIND = 2
VT_ROWS = LANES + 16
HGRN_L = 256
LOG2E = math.log2(math.e)
VMEM_LIMIT = 56 * 1024 * 1024


def _cparams(sem):
    return pltpu.CompilerParams(dimension_semantics=sem, vmem_limit_bytes=VMEM_LIMIT)


def _full_spec(shape):
    nd = len(shape)
    return pl.BlockSpec(shape, lambda *_: (0,) * nd)


def _split3(x):
    h = x.astype(BF16)
    r = x - h.astype(F32)
    m = r.astype(BF16)
    l = (r - m.astype(F32)).astype(BF16)
    return jnp.concatenate([h, m, l], axis=1)


def _sum3(y):
    return y[:, :LANES] + y[:, LANES:2 * LANES] + y[:, 2 * LANES:]


def _layer_norm_rows(y, g, b):
    mu = jnp.mean(y, axis=-1, keepdims=True)
    d = y - mu
    var = jnp.mean(d * d, axis=-1, keepdims=True)
    return d * lax.rsqrt(var + 1e-5) * g + b


def _even_proj_kernel(x_ref, w_ref, aq, af, ai, ag, bq, bk, bv):
    xb = x_ref[...].astype(BF16)
    outs = (aq, af, ai, ag, bq, bk, bv)
    for n, o in enumerate(outs):
        acc = jnp.dot(xb, w_ref[:, n * 512:(n + 1) * 512], preferred_element_type=F32)
        if o is bq:
            qt = (acc * (B_HEAD_DIM ** -0.5 * LOG2E)).T.astype(o.dtype)
            zero = jnp.zeros((B_HEAD_DIM, qt.shape[1]), o.dtype)
            for h in range(B_HEADS):
                r = 2 * h * LANES
                o[r:r + B_HEAD_DIM, :] = qt[h * LANES:h * LANES + B_HEAD_DIM, :]
                o[r + B_HEAD_DIM:r + LANES, :] = zero
                o[r + LANES:r + LANES + B_HEAD_DIM, :] = zero
                o[r + LANES + B_HEAD_DIM:r + 2 * LANES, :] = qt[h * LANES + B_HEAD_DIM:(h + 1) * LANES, :]
        elif o is bv:
            vt = acc.T.astype(o.dtype)
            ones = jnp.ones((VT_ROWS - LANES, vt.shape[1]), o.dtype)
            for h in range(B_HEADS):
                o[h * VT_ROWS:h * VT_ROWS + LANES, :] = vt[h * LANES:(h + 1) * LANES, :]
                o[h * VT_ROWS + LANES:(h + 1) * VT_ROWS, :] = ones
        else:
            o[...] = acc.astype(o.dtype)


def _even_proj(x2, w_bf, B, S):
    T = x2.shape[0]
    tm = PROJ_TM
    nt = S // tm
    tok = lambda dt: jax.ShapeDtypeStruct((T, 512), dt)
    spec = pl.BlockSpec((tm, 512), lambda i: (i, 0))
    tspec = lambda rows: pl.BlockSpec((None, rows, tm), lambda i: (i // nt, 0, i % nt))
    return pl.pallas_call(
        _even_proj_kernel,
        grid=(T // tm,),
        in_specs=[pl.BlockSpec((tm, D_MODEL), lambda i: (i, 0)), _full_spec(w_bf.shape)],
        out_specs=[spec, spec, spec, spec, tspec(2 * B_WIDTH), spec, tspec(B_HEADS * VT_ROWS)],
        out_shape=[tok(BF16), tok(F32), tok(BF16), tok(BF16),
                   jax.ShapeDtypeStruct((B, 2 * B_WIDTH, S), BF16), tok(BF16),
                   jax.ShapeDtypeStruct((B, B_HEADS * VT_ROWS, S), BF16)],
        compiler_params=_cparams(("parallel",)),
        name="even_proj",
    )(x2, w_bf)


def _hgrn_cumsum_matrix(L):
    idx = np.arange(L)
    mats = []
    c = 1
    while c < L:
        start = (idx // c) * c
        end = start + c - 1
        right = ((idx // c) % 2) == 1
        u = idx[None, :]
        m_right = (u >= start[:, None]) & (u <= idx[:, None])
        m_left = (u > idx[:, None]) & (u <= end[:, None])
        mats.append(np.where(right[:, None], m_right, m_left))
        c *= 2
    u = idx[None, :]
    mats.append(u <= idx[:, None])
    mats.append(u > idx[:, None])
    return np.concatenate(mats, axis=0).astype(np.float32)


def _split2(x):
    h = x.astype(BF16)
    m = (x - h.astype(F32)).astype(BF16)
    return jnp.concatenate([h, m], axis=1)


def _hgrn_kernel(q_ref, f_ref, i_ref, g_ref, lbl_ref, gn_ref, cmat_ref, o_ref, state_ref, *, layer_j):
    L = HGRN_L
    n_levels = int(math.log2(L))
    nt = (((1,), (1,)), ((), ()))

    @pl.when(pl.program_id(1) == 0)
    def _():
        state_ref[...] = jnp.zeros_like(state_ref)

    lbl = lbl_ref[...]
    e = jnp.exp(lbl - jnp.max(lbl, axis=0, keepdims=True))
    soft = e / jnp.sum(e, axis=0, keepdims=True)
    lb_all = jnp.sum(soft[:layer_j + 1], axis=0, keepdims=True) - soft[0:1]

    row = lax.broadcasted_iota(jnp.int32, (L, 1), 0)
    xor = lax.broadcasted_iota(jnp.int32, (L, L), 0) ^ lax.broadcasted_iota(jnp.int32, (L, L), 1)
    cmat = cmat_ref[...]

    heads = range(A_HEADS)
    cols = [slice(h * LANES, (h + 1) * LANES) for h in heads]
    q, k, gdec = [], [], []
    for h in heads:
        lb = lb_all[:, cols[h]]
        f = lb + (1.0 - lb) * jax.nn.sigmoid(f_ref[:, cols[h]])
        k.append(1.0 - f)
        qr = q_ref[:, cols[h]].astype(F32)
        q.append(qr * jax.nn.sigmoid(qr))
        ex = jnp.dot(cmat, _split2(jnp.log(f)), preferred_element_type=F32)
        gdec.append(jnp.exp(ex[:, :LANES] + ex[:, LANES:]))

    scores = []
    for h in heads:
        s = jnp.where(xor == 0, lax.dot_general(q[h].astype(BF16), k[h].astype(BF16), nt,
                                                preferred_element_type=F32), 0.0)
        for lvl in range(n_levels):
            g_l = gdec[h][lvl * L:(lvl + 1) * L, :]
            right = ((row >> lvl) & 1).astype(F32)
            ql = (q[h] * g_l * right).astype(BF16)
            kl = (k[h] * g_l * (1.0 - right)).astype(BF16)
            s_l = lax.dot_general(ql, kl, nt, preferred_element_type=F32)
            s = s + jnp.where(xor < (2 << lvl), s_l, 0.0)
        scores.append(s.astype(BF16))

    for h in heads:
        iv = i_ref[:, cols[h]]
        g_full = gdec[h][n_levels * L:(n_levels + 1) * L, :]
        g_rest = gdec[h][(n_levels + 1) * L:(n_levels + 2) * L, :]
        state_t = state_ref[h]
        o = lax.dot_general((q[h] * g_full).astype(BF16), state_t.astype(BF16), nt,
                            preferred_element_type=F32)
        o = o + jnp.dot(scores[h], iv, preferred_element_type=F32)
        k_out = (k[h] * g_rest).astype(BF16)
        upd = jnp.dot(iv.astype(F32).T.astype(BF16), k_out, preferred_element_type=F32)
        state_ref[h] = state_t * g_full[L - 1:L, :] + upd
        ms = jnp.mean(o * o, axis=-1, keepdims=True)
        gate = g_ref[:, cols[h]].astype(F32)
        o = o * lax.rsqrt(ms + 1e-6) * gn_ref[...] * (gate * jax.nn.sigmoid(gate))
        o_ref[:, cols[h]] = o.astype(o_ref.dtype)


def _hgrn(aq, af, ai, ag, lb_logits, gn, cmat, layer_j, B, S):
    L = HGRN_L
    nb = S // L
    blk = pl.BlockSpec((L, A_WIDTH), lambda b, t: (b * nb + t, 0))
    return pl.pallas_call(
        functools.partial(_hgrn_kernel, layer_j=layer_j),
        grid=(B, nb),
        in_specs=[blk, blk, blk, blk, _full_spec(lb_logits.shape),
                  _full_spec(gn.shape), _full_spec(cmat.shape)],
        out_specs=blk,
        out_shape=jax.ShapeDtypeStruct((B * S, A_WIDTH), BF16),
        scratch_shapes=[pltpu.VMEM((A_HEADS, LANES, LANES), F32)],
        compiler_params=_cparams(("parallel", "arbitrary")),
        name="hgrn2",
    )(aq, af, ai, ag, lb_logits, gn, cmat)


def _attend(k_ref, vt_ref, qt_ref, streams, mask_fn):
    tk = k_ref.shape[0]
    kd = LANES
    tq = qt_ref.shape[1]
    def verdict(mask, word):
        return isinstance(mask, str) and mask == word

    chains = []
    for kc in range(0, tk, ATT_KC):
        for st in streams:
            for qc in range(0, tq, ATT_QC):
                n_keys, mask = ATT_KC, None
                if mask_fn is not None:
                    half = ATT_KC // 2
                    if verdict(mask_fn(kc + half, qc, half), "none"):
                        n_keys = half
                    mask = mask_fn(kc, qc, n_keys)
                    if verdict(mask, "none"):
                        continue
                    if verdict(mask, "all"):
                        mask = None
                chains.append((st, qc, kc, n_keys, mask))

    def scores(st, qc, kc, n_keys, mask):
        s = jnp.dot(k_ref[kc:kc + n_keys, st[4]:st[4] + kd], qt_ref[st[0]:st[0] + kd, qc:qc + ATT_QC],
                    preferred_element_type=F32).astype(BF16)
        if mask is not None:
            s = jnp.where(mask, s, jnp.asarray(MASK_VALUE, BF16))
        return s

    def softmax(st, qc, s):
        m_ref = st[1]
        cols = slice(qc, qc + ATT_QC)
        m_prev = m_ref[:, cols]
        m_run, parts, maxes = m_prev, [], []
        for r0 in range(0, s.shape[0], ATT_SUB):
            s_g = s[r0:r0 + ATT_SUB, :]
            m_run = jnp.maximum(m_run, jnp.max(s_g, axis=0, keepdims=True).astype(F32))
            parts.append(jnp.exp2(s_g - m_run.astype(BF16)))
            maxes.append(m_run)
        m_new = m_run
        parts = [p_g if m_g is m_new else p_g * jnp.exp2(m_g - m_new).astype(BF16)
                 for p_g, m_g in zip(parts, maxes)]
        p = parts[0] if len(parts) == 1 else jnp.concatenate(parts, axis=0)
        alpha = jnp.exp2(m_prev - m_new)
        m_ref[:, cols] = m_new
        return p, alpha

    def weighted_values(st, qc, kc, p, alpha):
        acc_ref, v0 = st[2], st[3]
        cols = slice(qc, qc + ATT_QC)
        acc_ref[:, cols] = alpha * acc_ref[:, cols] + jnp.dot(
            vt_ref[v0:v0 + acc_ref.shape[0], kc:kc + p.shape[0]], p, preferred_element_type=F32)

    n = len(chains)
    ready = [scores(*c) for c in chains[:ATT_AHEAD]]
    pending = []
    for i, (st, qc, kc, _, _) in enumerate(chains):
        s = ready.pop(0)
        if i + ATT_AHEAD < n:
            ready.append(scores(*chains[i + ATT_AHEAD]))
        pending.append((st, qc, kc, *softmax(st, qc, s)))
        if len(pending) > ATT_BEHIND:
            weighted_values(*pending.pop(0))
    for item in pending:
        weighted_values(*item)


ATT_MULT = ATT_TQ // ATT_TK


def _tri_schedule(nq):
    qi = [i for i in range(nq) for _ in range((i + 1) * ATT_MULT)]
    ki = [j for i in range(nq) for j in range((i + 1) * ATT_MULT)]
    return jnp.asarray(qi, jnp.int32), jnp.asarray(ki, jnp.int32)


def _attend_tile(qi, ki, run, mask_for, finalize):
    rel = ki - qi * ATT_MULT

    @pl.when(rel < 0)
    def _():
        run(None)

    for r in range(ATT_MULT):
        @pl.when(rel == r)
        def _(r=r):
            run(mask_for(r * ATT_TK))
            if r == ATT_MULT - 1:
                finalize()


def _init_stats(m_ref, acc_ref):
    m_ref[...] = jnp.full_like(m_ref, MASK_VALUE)
    acc_ref[...] = jnp.zeros_like(acc_ref)


def _diff_kernel(qi_ref, ki_ref, q_ref, k_ref, v_ref, lam_ref, gn_ref, o_ref,
                 m1, a1, m2, a2, *, lam_init):
    step = pl.program_id(2)
    qi = qi_ref[step]
    ki = ki_ref[step]

    @pl.when(ki == 0)
    def _():
        _init_stats(m1, a1)
        _init_stats(m2, a2)

    def run(mask_fn):
        _attend(k_ref, v_ref, q_ref, [(0, m1, a1, 0, 0), (LANES, m2, a2, 0, 0)], mask_fn)

    def mask_for(key_offset):
        def chunk_mask(k0, q0, klen):
            k0 = k0 + key_offset
            if k0 + klen <= q0 + DIFF_CHUNK:
                return "all"
            if k0 >= q0 + ATT_QC:
                return "none"
            sh = DIFF_CHUNK.bit_length() - 1
            kk = (lax.broadcasted_iota(jnp.int32, (klen, ATT_QC), 0) + k0) >> sh
            qq = (lax.broadcasted_iota(jnp.int32, (klen, ATT_QC), 1) + q0) >> sh
            return kk <= qq
        return chunk_mask

    def finalize():
        lv = lam_ref[...]
        lam = (jnp.exp(jnp.sum(lv[0:1] * lv[1:2], axis=-1, keepdims=True))
               - jnp.exp(jnp.sum(lv[2:3] * lv[3:4], axis=-1, keepdims=True)) + lam_init)
        o = (a1[:LANES, :] / a1[LANES:LANES + 1, :]
             - lam * (a2[:LANES, :] / a2[LANES:LANES + 1, :])).T
        ms = jnp.mean(o * o, axis=-1, keepdims=True)
        o = o * lax.rsqrt(ms + 1e-6) * gn_ref[...] * (1.0 - lam_init)
        o_ref[...] = o.astype(o_ref.dtype)

    _attend_tile(qi, ki, run, mask_for, finalize)


def _diff_attention(bq, bk, bv, lamv, gn, lam_init, B, S):
    tq, tk = ATT_TQ, ATT_TK
    nq, nk = S // tq, S // tk
    qi, ki = _tri_schedule(nq)
    stat = pltpu.VMEM((1, tq), F32)
    acc = pltpu.VMEM((VT_ROWS, tq), F32)
    grid_spec = pltpu.PrefetchScalarGridSpec(
        num_scalar_prefetch=2,
        grid=(B, B_HEADS, int(qi.shape[0])),
        in_specs=[
            pl.BlockSpec((None, 2 * LANES, tq), lambda b, h, s, qi, ki: (b, h, qi[s])),
            pl.BlockSpec((tk, LANES), lambda b, h, s, qi, ki: (b * nk + ki[s], h)),
            pl.BlockSpec((None, VT_ROWS, tk), lambda b, h, s, qi, ki: (b, h, ki[s])),
            pl.BlockSpec(lamv.shape, lambda b, h, s, qi, ki: (0, 0)),
            pl.BlockSpec(gn.shape, lambda b, h, s, qi, ki: (0, 0)),
        ],
        out_specs=pl.BlockSpec((tq, LANES), lambda b, h, s, qi, ki: (b * nq + qi[s], h)),
        scratch_shapes=[stat, acc, stat, acc],
    )
    return pl.pallas_call(
        functools.partial(_diff_kernel, lam_init=lam_init),
        grid_spec=grid_spec,
        out_shape=jax.ShapeDtypeStruct((B * S, B_WIDTH), BF16),
        compiler_params=_cparams(("parallel", "parallel", "arbitrary")),
        name="diff_attn",
    )(qi, ki, bq, bk, bv, lamv, gn)


FOX_AUG = 16
FOX_VROWS = C_HEAD_DIM + 16
FOX_PAD = LANES - C_HEAD_DIM - FOX_AUG


def _fox_slots(h):
    base = (h // 2) * 2 * LANES
    if h % 2 == 0:
        return base, base + C_HEAD_DIM, base + C_HEAD_DIM + FOX_AUG
    return base + 2 * LANES - C_HEAD_DIM, base + LANES, base + LANES + FOX_AUG


def _fox_selectors():
    n_pairs = C_HEADS // 2
    selk = np.zeros((3 * LANES, n_pairs * LANES), np.float32)
    onek = np.zeros((1, n_pairs * LANES), np.float32)
    selq = np.zeros((C_HEADS * FOX_AUG, 3 * LANES), np.float32)
    oneq = np.zeros((C_HEADS * FOX_AUG, LANES), np.float32)
    for h in range(C_HEADS):
        p, odd = divmod(h, 2)
        lane0 = p * LANES + (0 if odd else C_HEAD_DIM)
        for piece in range(3):
            onek[0, lane0 + piece] = 1.0
            selk[piece * LANES + h, lane0 + 3 + piece] = -1.0
            selq[h * FOX_AUG + piece, piece * LANES + h] = 1.0
            oneq[h * FOX_AUG + 3 + piece, :] = 1.0
    return selk, onek, selq, oneq


def _odd_proj_kernel(x_ref, w_ref, bf_ref, qg_ref, kg_ref, gmat_ref, tri_ref,
                     selk_ref, onek_ref, selq_ref, oneq_ref,
                     qt_o, k_o, vt_o, g_o, carry_ref):
    @pl.when(pl.program_id(1) == 0)
    def _():
        carry_ref[...] = jnp.zeros_like(carry_ref)

    xb = x_ref[...].astype(BF16)
    tm = xb.shape[0]
    ch = MXU_DIM
    hpc = ch // C_HEAD_DIM
    gmat = gmat_ref[...]

    def headnorm(acc, g):
        ms = jnp.dot((acc * acc).astype(BF16), gmat, preferred_element_type=F32)
        return acc * lax.rsqrt(ms + 1e-6) * g

    zero_pad = jnp.zeros((FOX_PAD, tm), BF16)
    ones = jnp.ones((FOX_VROWS - C_HEAD_DIM, tm), BF16)
    for j in range(D_MODEL // ch):
        acc = jnp.dot(xb, w_ref[:, j * ch:(j + 1) * ch], preferred_element_type=F32)
        qt = (headnorm(acc, qg_ref[...]) * (C_HEAD_DIM ** -0.5 * LOG2E)).T.astype(BF16)
        acc = jnp.dot(xb, w_ref[:, 2 * D_MODEL + j * ch:2 * D_MODEL + (j + 1) * ch],
                      preferred_element_type=F32)
        vt = acc.T.astype(BF16)
        for i in range(hpc):
            h = j * hpc + i
            src = slice(i * C_HEAD_DIM, (i + 1) * C_HEAD_DIM)
            val0, _, zero0 = _fox_slots(h)
            qt_o[val0:val0 + C_HEAD_DIM, :] = qt[src, :]
            qt_o[zero0:zero0 + FOX_PAD, :] = zero_pad
            vt_o[h * FOX_VROWS:h * FOX_VROWS + C_HEAD_DIM, :] = vt[src, :]
            vt_o[h * FOX_VROWS + C_HEAD_DIM:(h + 1) * FOX_VROWS, :] = ones
        acc = jnp.dot(xb, w_ref[:, D_MODEL + j * ch:D_MODEL + (j + 1) * ch], preferred_element_type=F32)
        kn = headnorm(acc, kg_ref[...]).astype(BF16)
        for i in range(hpc):
            val0 = _fox_slots(j * hpc + i)[0]
            k_o[:, val0:val0 + C_HEAD_DIM] = kn[:, i * C_HEAD_DIM:(i + 1) * C_HEAD_DIM]
        acc = jnp.dot(xb, w_ref[:, 3 * D_MODEL + j * ch:3 * D_MODEL + (j + 1) * ch],
                      preferred_element_type=F32)
        g_o[:, j * ch:(j + 1) * ch] = acc.astype(g_o.dtype)

    fl = jnp.dot(xb, w_ref[:, 4 * D_MODEL:4 * D_MODEL + LANES], preferred_element_type=F32) + bf_ref[...]
    logf = jnp.minimum(fl, 0.0) - jnp.log(1.0 + jnp.exp(-jnp.abs(fl)))
    c = _sum3(jnp.dot(tri_ref[...], _split3(logf), preferred_element_type=F32)) + carry_ref[...]
    carry_ref[...] = c[tm - 1:tm, :]
    c3 = _split3(c * LOG2E)
    aug_k = (jnp.dot(c3, selk_ref[...], preferred_element_type=F32) + onek_ref[...]).astype(BF16)
    half = LANES // 2
    for p in range(C_HEADS // 2):
        k_o[:, 2 * p * LANES + half:2 * p * LANES + LANES] = aug_k[:, p * LANES + half:(p + 1) * LANES]
        k_o[:, (2 * p + 1) * LANES:(2 * p + 1) * LANES + half] = aug_k[:, p * LANES:p * LANES + half]
    aug_q = lax.dot_general(selq_ref[...], c3, (((1,), (1,)), ((), ())), preferred_element_type=F32)
    aug_q = (aug_q + jnp.concatenate([oneq_ref[...]] * (tm // LANES), axis=1)).astype(BF16)
    for h in range(C_HEADS):
        aug0 = _fox_slots(h)[1]
        qt_o[aug0:aug0 + FOX_AUG, :] = aug_q[h * FOX_AUG:(h + 1) * FOX_AUG, :]


def _odd_proj(x2, w_bf, bf_pad, qg, kg, gmat, tri, B, S):
    T = B * S
    tm = PROJ_TM
    nt = S // tm
    tok = lambda n: pl.BlockSpec((tm, n), lambda b, t: (b * nt + t, 0))
    tspec = lambda rows: pl.BlockSpec((None, rows, tm), lambda b, t: (b, 0, t))
    selk, onek, selq, oneq = _fox_selectors()
    consts = [jnp.asarray(selk, BF16), jnp.asarray(onek, F32), jnp.asarray(selq, BF16),
              jnp.asarray(oneq, F32)]
    k_width = C_HEADS * LANES
    return pl.pallas_call(
        _odd_proj_kernel,
        grid=(B, nt),
        in_specs=[tok(D_MODEL), _full_spec(w_bf.shape), _full_spec(bf_pad.shape), _full_spec(qg.shape),
                  _full_spec(kg.shape), _full_spec(gmat.shape), _full_spec(tri.shape)]
                 + [_full_spec(c.shape) for c in consts],
        out_specs=[tspec(k_width), tok(k_width), tspec(C_HEADS * FOX_VROWS), tok(D_MODEL)],
        out_shape=[jax.ShapeDtypeStruct((B, k_width, S), BF16),
                   jax.ShapeDtypeStruct((T, k_width), BF16),
                   jax.ShapeDtypeStruct((B, C_HEADS * FOX_VROWS, S), BF16),
                   jax.ShapeDtypeStruct((T, D_MODEL), BF16)],
        scratch_shapes=[pltpu.VMEM((1, LANES), F32)],
        compiler_params=_cparams(("parallel", "arbitrary")),
        name="odd_proj",
    )(x2, w_bf, bf_pad, qg, kg, gmat, tri, *consts)


def _fox_kernel(qi_ref, ki_ref, q_ref, k_ref, v_ref, o_ref, ma, aa, mb, ab):
    step = pl.program_id(2)
    qi = qi_ref[step]
    ki = ki_ref[step]

    @pl.when(ki == 0)
    def _():
        _init_stats(ma, aa)
        _init_stats(mb, ab)

    def run(mask_fn):
        _attend(k_ref, v_ref, q_ref, [(0, ma, aa, 0, 0), (LANES, mb, ab, FOX_VROWS, LANES)], mask_fn)

    def mask_for(key_offset):
        def chunk_mask(k0, q0, klen):
            k0 = k0 + key_offset
            if k0 + klen <= q0 + 1:
                return "all"
            if k0 >= q0 + ATT_QC:
                return "none"
            kk = lax.broadcasted_iota(jnp.int32, (klen, ATT_QC), 0) + k0
            qq = lax.broadcasted_iota(jnp.int32, (klen, ATT_QC), 1) + q0
            return kk <= qq
        return chunk_mask

    def finalize():
        d = C_HEAD_DIM
        o = jnp.concatenate([aa[:d, :] / aa[d:d + 1, :], ab[:d, :] / ab[d:d + 1, :]], axis=0)
        o_ref[...] = o.T.astype(o_ref.dtype)

    _attend_tile(qi, ki, run, mask_for, finalize)


def _fox_attention(qt, k, vt, B, S):
    tq, tk = ATT_TQ, ATT_TK
    nq, nk = S // tq, S // tk
    qi, ki = _tri_schedule(nq)
    n_pairs = C_HEADS // 2
    stat = pltpu.VMEM((1, tq), F32)
    acc = pltpu.VMEM((FOX_VROWS, tq), F32)
    grid_spec = pltpu.PrefetchScalarGridSpec(
        num_scalar_prefetch=2,
        grid=(B, n_pairs, int(qi.shape[0])),
        in_specs=[
            pl.BlockSpec((None, 2 * LANES, tq), lambda b, h, s, qi, ki: (b, h, qi[s])),
            pl.BlockSpec((tk, 2 * LANES), lambda b, h, s, qi, ki: (b * nk + ki[s], h)),
            pl.BlockSpec((None, 2 * FOX_VROWS, tk), lambda b, h, s, qi, ki: (b, h, ki[s])),
        ],
        out_specs=pl.BlockSpec((tq, LANES), lambda b, h, s, qi, ki: (b * nq + qi[s], h)),
        scratch_shapes=[stat, acc, stat, acc],
    )
    return pl.pallas_call(
        _fox_kernel,
        grid_spec=grid_spec,
        out_shape=jax.ShapeDtypeStruct((B * S, D_MODEL), BF16),
        compiler_params=_cparams(("parallel", "parallel", "arbitrary")),
        name="fox_attn",
    )(qi, ki, qt, k, vt)


def _even_mix(oa_ref, ob_ref, w_ref):
    h = jnp.dot(oa_ref[...], w_ref[:A_WIDTH, :], preferred_element_type=F32)
    return h + jnp.dot(ob_ref[...], w_ref[A_WIDTH:, :], preferred_element_type=F32)


def _odd_mix(o_ref, gate_ref, w_ref):
    o = o_ref[...].astype(F32) * jax.nn.sigmoid(gate_ref[...].astype(F32))
    return jnp.dot(o.astype(BF16), w_ref[...], preferred_element_type=F32)


def _tail_kernel(a0_ref, a1_ref, x_ref, wo_ref, g1_ref, b1_ref, w1_ref, w2_ref, g2_ref, b2_ref,
                 y_ref, act_ref, *, mix):
    x = _layer_norm_rows(ALPHA * x_ref[...] + mix(a0_ref, a1_ref, wo_ref), g1_ref[...], b1_ref[...])
    xb = x.astype(BF16)
    ch = MXU_DIM
    for j in range(D_FF // ch):
        gate = jnp.dot(xb, w1_ref[:, j * ch:(j + 1) * ch], preferred_element_type=F32)
        up = jnp.dot(xb, w1_ref[:, D_FF + j * ch:D_FF + (j + 1) * ch], preferred_element_type=F32)
        act_ref[:, j * ch:(j + 1) * ch] = (gate * jax.nn.sigmoid(gate) * up).astype(BF16)
    h = jnp.dot(act_ref[...], w2_ref[...], preferred_element_type=F32)
    y_ref[...] = _layer_norm_rows(ALPHA * x + h, g2_ref[...], b2_ref[...])


def _layer_tail(mix, acts, x2, wo_bf, g1, b1, w1_bf, w2_bf, g2, b2, name):
    T = x2.shape[0]
    tm = FFN_TM
    tok = lambda n: pl.BlockSpec((tm, n), lambda i: (i, 0))
    consts = [wo_bf, g1, b1, w1_bf, w2_bf, g2, b2]
    return pl.pallas_call(
        functools.partial(_tail_kernel, mix=mix),
        grid=(T // tm,),
        in_specs=[tok(a.shape[1]) for a in acts] + [tok(D_MODEL)] + [_full_spec(c.shape) for c in consts],
        out_specs=tok(D_MODEL),
        out_shape=jax.ShapeDtypeStruct((T, D_MODEL), F32),
        scratch_shapes=[pltpu.VMEM((tm, D_FF), BF16)],
        compiler_params=_cparams(("parallel",)),
        name=name,
    )(*acts, x2, *consts)


def kernel(x, even_w_in, even_w_out, hgrn_lb_logits, diff_lq1, diff_lk1, diff_lq2, diff_lk2,
           hgrn_norm_g, diff_norm_g, fox_w_in, fox_w_out, fox_b_f, fox_qnorm_g, fox_knorm_g,
           ffn_w1, ffn_w2, ln1_g, ln1_b, ln2_g, ln2_b):
    B, S, D = x.shape
    assert D == D_MODEL and S % ATT_TQ == 0 and S % PROJ_TM == 0 and S % HGRN_L == 0
    T = B * S
    x2 = x.reshape(T, D).astype(F32)

    cmat = jnp.asarray(_hgrn_cumsum_matrix(HGRN_L), BF16)
    tri = jnp.asarray(np.tril(np.ones((PROJ_TM, PROJ_TM), np.float32)), BF16)
    head_of = np.arange(MXU_DIM) // C_HEAD_DIM
    gmat = jnp.asarray((head_of[:, None] == head_of[None, :]).astype(np.float32) / C_HEAD_DIM, BF16)
    row = lambda v: v.astype(F32).reshape(1, -1)

    for l in range(DEPTH):
        j = l // 2
        if l % 2 == 0:
            aq, af, ai, ag, bq, bk, bv = _even_proj(x2, even_w_in[j].astype(BF16), B, S)
            o_a = _hgrn(aq, af, ai, ag, hgrn_lb_logits.astype(F32), row(hgrn_norm_g[j]), cmat, j, B, S)
            lamv = jnp.zeros((8, B_HEAD_DIM), F32).at[0:4].set(
                jnp.stack([diff_lq1[j], diff_lk1[j], diff_lq2[j], diff_lk2[j]]).astype(F32))
            lam_init = 0.8 - 0.6 * math.exp(-0.3 * l)
            o_b = _diff_attention(bq, bk, bv, lamv, row(diff_norm_g[j]), lam_init, B, S)
            mix, acts, w_out, name = _even_mix, [o_a, o_b], even_w_out[j], "even_tail"
        else:
            w_pad = jnp.pad(fox_w_in[j], ((0, 0), (0, LANES - C_HEADS))).astype(BF16)
            bf_pad = jnp.pad(fox_b_f[j].astype(F32), (0, LANES - C_HEADS)).reshape(1, LANES)
            tile4 = lambda v: jnp.tile(v.astype(F32), MXU_DIM // C_HEAD_DIM).reshape(1, MXU_DIM)
            qt, k, vt, g = _odd_proj(x2, w_pad, bf_pad, tile4(fox_qnorm_g[j]),
                                     tile4(fox_knorm_g[j]), gmat, tri, B, S)
            o = _fox_attention(qt, k, vt, B, S)
            mix, acts, w_out, name = _odd_mix, [o, g], fox_w_out[j], "odd_tail"
        x2 = _layer_tail(mix, acts, x2, w_out.astype(BF16), row(ln1_g[l]), row(ln1_b[l]),
                         ffn_w1[l].astype(BF16), ffn_w2[l].astype(BF16), row(ln2_g[l]), row(ln2_b[l]), name)
    return x2.reshape(B, S, D).astype(x.dtype)
```

```python
import functools
import math

import numpy as np
import jax
import jax.numpy as jnp
from jax import lax
from jax.experimental import pallas as pl
from jax.experimental.pallas import tpu as pltpu

F32 = jnp.float32
BF16 = jnp.bfloat16

D_MODEL = 1024
DEPTH = 4
A_WIDTH = 512
A_HEADS = 4
A_DK = 128
B_WIDTH = 512
B_HEADS = 4
B_HEAD_DIM = 64
DIFF_CHUNK = 64
C_HEADS = 16
C_HEAD_DIM = 64
D_FF = 2816
ALPHA = (2 * DEPTH) ** 0.25
MASK_VALUE = -1e30

LANES = 128
MXU_DIM = 256

PROJ_TM = 512
FFN_TM = 512
ATT_TQ = 4096
ATT_TK = 2048
ATT_QC = 256
ATT_KC = 512
ATT_SUB = 64
ATT_AHEAD = 2
ATT_BEHIND = 2
VT_ROWS = LANES + 16
HGRN_L = 256
LOG2E = math.log2(math.e)
VMEM_LIMIT = 56 * 1024 * 1024


def _cparams(sem):
    return pltpu.CompilerParams(dimension_semantics=sem, vmem_limit_bytes=VMEM_LIMIT)


def _full_spec(shape):
    nd = len(shape)
    return pl.BlockSpec(shape, lambda *_: (0,) * nd)


def _split3(x):
    h = x.astype(BF16)
    r = x - h.astype(F32)
    m = r.astype(BF16)
    l = (r - m.astype(F32)).astype(BF16)
    return jnp.concatenate([h, m, l], axis=1)


def _sum3(y):
    return y[:, :LANES] + y[:, LANES:2 * LANES] + y[:, 2 * LANES:]


def _layer_norm_rows(y, g, b):
    mu = jnp.mean(y, axis=-1, keepdims=True)
    d = y - mu
    var = jnp.mean(d * d, axis=-1, keepdims=True)
    return d * lax.rsqrt(var + 1e-5) * g + b


def _even_proj_kernel(x_ref, w_ref, aq, af, ai, ag, bq, bk, bv):
    xb = x_ref[...].astype(BF16)
    outs = (aq, af, ai, ag, bq, bk, bv)
    for n, o in enumerate(outs):
        acc = jnp.dot(xb, w_ref[:, n * 512:(n + 1) * 512], preferred_element_type=F32)
        if o is bq:
            qt = (acc * (B_HEAD_DIM ** -0.5 * LOG2E)).T.astype(o.dtype)
            zero = jnp.zeros((B_HEAD_DIM, qt.shape[1]), o.dtype)
            for h in range(B_HEADS):
                r = 2 * h * LANES
                o[r:r + B_HEAD_DIM, :] = qt[h * LANES:h * LANES + B_HEAD_DIM, :]
                o[r + B_HEAD_DIM:r + LANES, :] = zero
                o[r + LANES:r + LANES + B_HEAD_DIM, :] = zero
                o[r + LANES + B_HEAD_DIM:r + 2 * LANES, :] = qt[h * LANES + B_HEAD_DIM:(h + 1) * LANES, :]
        elif o is bv:
            vt = acc.T.astype(o.dtype)
            ones = jnp.ones((VT_ROWS - LANES, vt.shape[1]), o.dtype)
            for h in range(B_HEADS):
                o[h * VT_ROWS:h * VT_ROWS + LANES, :] = vt[h * LANES:(h + 1) * LANES, :]
                o[h * VT_ROWS + LANES:(h + 1) * VT_ROWS, :] = ones
        else:
            o[...] = acc.astype(o.dtype)


def _even_proj(x2, w_bf, B, S):
    T = x2.shape[0]
    tm = PROJ_TM
    nt = S // tm
    tok = lambda dt: jax.ShapeDtypeStruct((T, 512), dt)
    spec = pl.BlockSpec((tm, 512), lambda i: (i, 0))
    tspec = lambda rows: pl.BlockSpec((None, rows, tm), lambda i: (i // nt, 0, i % nt))
    return pl.pallas_call(
        _even_proj_kernel,
        grid=(T // tm,),
        in_specs=[pl.BlockSpec((tm, D_MODEL), lambda i: (i, 0)), _full_spec(w_bf.shape)],
        out_specs=[spec, spec, spec, spec, tspec(2 * B_WIDTH), spec, tspec(B_HEADS * VT_ROWS)],
        out_shape=[tok(BF16), tok(F32), tok(BF16), tok(BF16),
                   jax.ShapeDtypeStruct((B, 2 * B_WIDTH, S), BF16), tok(BF16),
                   jax.ShapeDtypeStruct((B, B_HEADS * VT_ROWS, S), BF16)],
        compiler_params=_cparams(("parallel",)),
        name="even_proj",
    )(x2, w_bf)


def _hgrn_cumsum_matrix(L):
    idx = np.arange(L)
    mats = []
    c = 1
    while c < L:
        start = (idx // c) * c
        end = start + c - 1
        right = ((idx // c) % 2) == 1
        u = idx[None, :]
        m_right = (u >= start[:, None]) & (u <= idx[:, None])
        m_left = (u > idx[:, None]) & (u <= end[:, None])
        mats.append(np.where(right[:, None], m_right, m_left))
        c *= 2
    u = idx[None, :]
    mats.append(u <= idx[:, None])
    mats.append(u > idx[:, None])
    return np.concatenate(mats, axis=0).astype(np.float32)


def _split2(x):
    h = x.astype(BF16)
    m = (x - h.astype(F32)).astype(BF16)
    return jnp.concatenate([h, m], axis=1)


def _hgrn_kernel(q_ref, f_ref, i_ref, g_ref, lbl_ref, gn_ref, cmat_ref, o_ref, state_ref, *, layer_j):
    L = HGRN_L
    n_levels = int(math.log2(L))
    nt = (((1,), (1,)), ((), ()))

    @pl.when(pl.program_id(1) == 0)
    def _():
        state_ref[...] = jnp.zeros_like(state_ref)

    lbl = lbl_ref[...]
    e = jnp.exp(lbl - jnp.max(lbl, axis=0, keepdims=True))
    soft = e / jnp.sum(e, axis=0, keepdims=True)
    lb_all = jnp.sum(soft[:layer_j + 1], axis=0, keepdims=True) - soft[0:1]

    row = lax.broadcasted_iota(jnp.int32, (L, 1), 0)
    xor = lax.broadcasted_iota(jnp.int32, (L, L), 0) ^ lax.broadcasted_iota(jnp.int32, (L, L), 1)
    cmat = cmat_ref[...]

    heads = range(A_HEADS)
    cols = [slice(h * LANES, (h + 1) * LANES) for h in heads]
    q, k, gdec = [], [], []
    for h in heads:
        lb = lb_all[:, cols[h]]
        f = lb + (1.0 - lb) * jax.nn.sigmoid(f_ref[:, cols[h]])
        k.append(1.0 - f)
        qr = q_ref[:, cols[h]].astype(F32)
        q.append(qr * jax.nn.sigmoid(qr))
        ex = jnp.dot(cmat, _split2(jnp.log(f)), preferred_element_type=F32)
        gdec.append(jnp.exp(ex[:, :LANES] + ex[:, LANES:]))

    scores = []
    for h in heads:
        s = jnp.where(xor == 0, lax.dot_general(q[h].astype(BF16), k[h].astype(BF16), nt,
                                                preferred_element_type=F32), 0.0)
        for lvl in range(n_levels):
            g_l = gdec[h][lvl * L:(lvl + 1) * L, :]
            right = ((row >> lvl) & 1).astype(F32)
            ql = (q[h] * g_l * right).astype(BF16)
            kl = (k[h] * g_l * (1.0 - right)).astype(BF16)
            s_l = lax.dot_general(ql, kl, nt, preferred_element_type=F32)
            s = s + jnp.where(xor < (2 << lvl), s_l, 0.0)
        scores.append(s.astype(BF16))

    for h in heads:
        iv = i_ref[:, cols[h]]
        g_full = gdec[h][n_levels * L:(n_levels + 1) * L, :]
        g_rest = gdec[h][(n_levels + 1) * L:(n_levels + 2) * L, :]
        state_t = state_ref[h]
        o = lax.dot_general((q[h] * g_full).astype(BF16), state_t.astype(BF16), nt,
                            preferred_element_type=F32)
        o = o + jnp.dot(scores[h], iv, preferred_element_type=F32)
        k_out = (k[h] * g_rest).astype(BF16)
        upd = jnp.dot(iv.astype(F32).T.astype(BF16), k_out, preferred_element_type=F32)
        state_ref[h] = state_t * g_full[L - 1:L, :] + upd
        ms = jnp.mean(o * o, axis=-1, keepdims=True)
        gate = g_ref[:, cols[h]].astype(F32)
        o = o * lax.rsqrt(ms + 1e-6) * gn_ref[...] * (gate * jax.nn.sigmoid(gate))
        o_ref[:, cols[h]] = o.astype(o_ref.dtype)


def _hgrn(aq, af, ai, ag, lb_logits, gn, cmat, layer_j, B, S):
    L = HGRN_L
    nb = S // L
    blk = pl.BlockSpec((L, A_WIDTH), lambda b, t: (b * nb + t, 0))
    return pl.pallas_call(
        functools.partial(_hgrn_kernel, layer_j=layer_j),
        grid=(B, nb),
        in_specs=[blk, blk, blk, blk, _full_spec(lb_logits.shape),
                  _full_spec(gn.shape), _full_spec(cmat.shape)],
        out_specs=blk,
        out_shape=jax.ShapeDtypeStruct((B * S, A_WIDTH), BF16),
        scratch_shapes=[pltpu.VMEM((A_HEADS, LANES, LANES), F32)],
        compiler_params=_cparams(("parallel", "arbitrary")),
        name="hgrn2",
    )(aq, af, ai, ag, lb_logits, gn, cmat)


def _attend(k_ref, vt_ref, qt_ref, streams, mask_fn):
    tk = k_ref.shape[0]
    kd = LANES
    tq = qt_ref.shape[1]
    def verdict(mask, word):
        return isinstance(mask, str) and mask == word

    chains = []
    for kc in range(0, tk, ATT_KC):
        for st in streams:
            for qc in range(0, tq, ATT_QC):
                n_keys, mask = ATT_KC, None
                if mask_fn is not None:
                    half = ATT_KC // 2
                    if verdict(mask_fn(kc + half, qc, half), "none"):
                        n_keys = half
                    mask = mask_fn(kc, qc, n_keys)
                    if verdict(mask, "none"):
                        continue
                    if verdict(mask, "all"):
                        mask = None
                chains.append((st, qc, kc, n_keys, mask))

    def scores(st, qc, kc, n_keys, mask):
        s = jnp.dot(k_ref[kc:kc + n_keys, st[4]:st[4] + kd], qt_ref[st[0]:st[0] + kd, qc:qc + ATT_QC],
                    preferred_element_type=F32).astype(BF16)
        if mask is not None:
            s = jnp.where(mask, s, jnp.asarray(MASK_VALUE, BF16))
        return s

    def softmax(st, qc, s):
        m_ref = st[1]
        cols = slice(qc, qc + ATT_QC)
        m_prev = m_ref[:, cols]
        m_run, parts, maxes = m_prev, [], []
        for r0 in range(0, s.shape[0], ATT_SUB):
            s_g = s[r0:r0 + ATT_SUB, :]
            m_run = jnp.maximum(m_run, jnp.max(s_g, axis=0, keepdims=True).astype(F32))
            parts.append(jnp.exp2(s_g - m_run.astype(BF16)))
            maxes.append(m_run)
        m_new = m_run
        parts = [p_g if m_g is m_new else p_g * jnp.exp2(m_g - m_new).astype(BF16)
                 for p_g, m_g in zip(parts, maxes)]
        p = parts[0] if len(parts) == 1 else jnp.concatenate(parts, axis=0)
        alpha = jnp.exp2(m_prev - m_new)
        m_ref[:, cols] = m_new
        return p, alpha

    def weighted_values(st, qc, kc, p, alpha):
        acc_ref, v0 = st[2], st[3]
        cols = slice(qc, qc + ATT_QC)
        acc_ref[:, cols] = alpha * acc_ref[:, cols] + jnp.dot(
            vt_ref[v0:v0 + acc_ref.shape[0], kc:kc + p.shape[0]], p, preferred_element_type=F32)

    n = len(chains)
    ready = [scores(*c) for c in chains[:ATT_AHEAD]]
    pending = []
    for i, (st, qc, kc, _, _) in enumerate(chains):
        s = ready.pop(0)
        if i + ATT_AHEAD < n:
            ready.append(scores(*chains[i + ATT_AHEAD]))
        pending.append((st, qc, kc, *softmax(st, qc, s)))
        if len(pending) > ATT_BEHIND:
            weighted_values(*pending.pop(0))
    for item in pending:
        weighted_values(*item)


ATT_MULT = ATT_TQ // ATT_TK


def _tri_schedule(nq):
    qi = [i for i in range(nq) for _ in range((i + 1) * ATT_MULT)]
    ki = [j for i in range(nq) for j in range((i + 1) * ATT_MULT)]
    return jnp.asarray(qi, jnp.int32), jnp.asarray(ki, jnp.int32)


def _attend_tile(qi, ki, run, mask_for, finalize):
    rel = ki - qi * ATT_MULT

    @pl.when(rel < 0)
    def _():
        run(None)

    for r in range(ATT_MULT):
        @pl.when(rel == r)
        def _(r=r):
            run(mask_for(r * ATT_TK))
            if r == ATT_MULT - 1:
                finalize()


def _init_stats(m_ref, acc_ref):
    m_ref[...] = jnp.full_like(m_ref, MASK_VALUE)
    acc_ref[...] = jnp.zeros_like(acc_ref)


def _diff_kernel(qi_ref, ki_ref, q_ref, k_ref, v_ref, lam_ref, gn_ref, o_ref,
                 m1, a1, m2, a2, *, lam_init):
    step = pl.program_id(2)
    qi = qi_ref[step]
    ki = ki_ref[step]

    @pl.when(ki == 0)
    def _():
        _init_stats(m1, a1)
        _init_stats(m2, a2)

    def run(mask_fn):
        _attend(k_ref, v_ref, q_ref, [(0, m1, a1, 0, 0), (LANES, m2, a2, 0, 0)], mask_fn)

    def mask_for(key_offset):
        def chunk_mask(k0, q0, klen):
            k0 = k0 + key_offset
            if k0 + klen <= q0 + DIFF_CHUNK:
                return "all"
            if k0 >= q0 + ATT_QC:
                return "none"
            sh = DIFF_CHUNK.bit_length() - 1
            kk = (lax.broadcasted_iota(jnp.int32, (klen, ATT_QC), 0) + k0) >> sh
            qq = (lax.broadcasted_iota(jnp.int32, (klen, ATT_QC), 1) + q0) >> sh
            return kk <= qq
        return chunk_mask

    def finalize():
        lv = lam_ref[...]
        lam = (jnp.exp(jnp.sum(lv[0:1] * lv[1:2], axis=-1, keepdims=True))
               - jnp.exp(jnp.sum(lv[2:3] * lv[3:4], axis=-1, keepdims=True)) + lam_init)
        o = (a1[:LANES, :] / a1[LANES:LANES + 1, :]
             - lam * (a2[:LANES, :] / a2[LANES:LANES + 1, :])).T
        ms = jnp.mean(o * o, axis=-1, keepdims=True)
        o = o * lax.rsqrt(ms + 1e-6) * gn_ref[...] * (1.0 - lam_init)
        o_ref[...] = o.astype(o_ref.dtype)

    _attend_tile(qi, ki, run, mask_for, finalize)


def _diff_attention(bq, bk, bv, lamv, gn, lam_init, B, S):
    tq, tk = ATT_TQ, ATT_TK
    nq, nk = S // tq, S // tk
    qi, ki = _tri_schedule(nq)
    stat = pltpu.VMEM((1, tq), F32)
    acc = pltpu.VMEM((VT_ROWS, tq), F32)
    grid_spec = pltpu.PrefetchScalarGridSpec(
        num_scalar_prefetch=2,
        grid=(B, B_HEADS, int(qi.shape[0])),
        in_specs=[
            pl.BlockSpec((None, 2 * LANES, tq), lambda b, h, s, qi, ki: (b, h, qi[s])),
            pl.BlockSpec((tk, LANES), lambda b, h, s, qi, ki: (b * nk + ki[s], h)),
            pl.BlockSpec((None, VT_ROWS, tk), lambda b, h, s, qi, ki: (b, h, ki[s])),
            pl.BlockSpec(lamv.shape, lambda b, h, s, qi, ki: (0, 0)),
            pl.BlockSpec(gn.shape, lambda b, h, s, qi, ki: (0, 0)),
        ],
        out_specs=pl.BlockSpec((tq, LANES), lambda b, h, s, qi, ki: (b * nq + qi[s], h)),
        scratch_shapes=[stat, acc, stat, acc],
    )
    return pl.pallas_call(
        functools.partial(_diff_kernel, lam_init=lam_init),
        grid_spec=grid_spec,
        out_shape=jax.ShapeDtypeStruct((B * S, B_WIDTH), BF16),
        compiler_params=_cparams(("parallel", "parallel", "arbitrary")),
        name="diff_attn",
    )(qi, ki, bq, bk, bv, lamv, gn)


FOX_AUG = 16
FOX_VROWS = C_HEAD_DIM + 16
FOX_PAD = LANES - C_HEAD_DIM - FOX_AUG


def _fox_slots(h):
    base = (h // 2) * 2 * LANES
    if h % 2 == 0:
        return base, base + C_HEAD_DIM, base + C_HEAD_DIM + FOX_AUG
    return base + 2 * LANES - C_HEAD_DIM, base + LANES, base + LANES + FOX_AUG


def _fox_selectors():
    n_pairs = C_HEADS // 2
    selk = np.zeros((3 * LANES, n_pairs * LANES), np.float32)
    onek = np.zeros((1, n_pairs * LANES), np.float32)
    selq = np.zeros((C_HEADS * FOX_AUG, 3 * LANES), np.float32)
    oneq = np.zeros((C_HEADS * FOX_AUG, LANES), np.float32)
    for h in range(C_HEADS):
        p, odd = divmod(h, 2)
        lane0 = p * LANES + (0 if odd else C_HEAD_DIM)
        for piece in range(3):
            onek[0, lane0 + piece] = 1.0
            selk[piece * LANES + h, lane0 + 3 + piece] = -1.0
            selq[h * FOX_AUG + piece, piece * LANES + h] = 1.0
            oneq[h * FOX_AUG + 3 + piece, :] = 1.0
    return selk, onek, selq, oneq


def _odd_proj_kernel(x_ref, w_ref, bf_ref, qg_ref, kg_ref, gmat_ref, tri_ref,
                     selk_ref, onek_ref, selq_ref, oneq_ref,
                     qt_o, k_o, vt_o, g_o, carry_ref):
    @pl.when(pl.program_id(1) == 0)
    def _():
        carry_ref[...] = jnp.zeros_like(carry_ref)

    xb = x_ref[...].astype(BF16)
    tm = xb.shape[0]
    ch = MXU_DIM
    hpc = ch // C_HEAD_DIM
    gmat = gmat_ref[...]

    def headnorm(acc, g):
        ms = jnp.dot((acc * acc).astype(BF16), gmat, preferred_element_type=F32)
        return acc * lax.rsqrt(ms + 1e-6) * g

    zero_pad = jnp.zeros((FOX_PAD, tm), BF16)
    ones = jnp.ones((FOX_VROWS - C_HEAD_DIM, tm), BF16)
    for j in range(D_MODEL // ch):
        acc = jnp.dot(xb, w_ref[:, j * ch:(j + 1) * ch], preferred_element_type=F32)
        qt = (headnorm(acc, qg_ref[...]) * (C_HEAD_DIM ** -0.5 * LOG2E)).T.astype(BF16)
        acc = jnp.dot(xb, w_ref[:, 2 * D_MODEL + j * ch:2 * D_MODEL + (j + 1) * ch],
                      preferred_element_type=F32)
        vt = acc.T.astype(BF16)
        for i in range(hpc):
            h = j * hpc + i
            src = slice(i * C_HEAD_DIM, (i + 1) * C_HEAD_DIM)
            val0, _, zero0 = _fox_slots(h)
            qt_o[val0:val0 + C_HEAD_DIM, :] = qt[src, :]
            qt_o[zero0:zero0 + FOX_PAD, :] = zero_pad
            vt_o[h * FOX_VROWS:h * FOX_VROWS + C_HEAD_DIM, :] = vt[src, :]
            vt_o[h * FOX_VROWS + C_HEAD_DIM:(h + 1) * FOX_VROWS, :] = ones
        acc = jnp.dot(xb, w_ref[:, D_MODEL + j * ch:D_MODEL + (j + 1) * ch], preferred_element_type=F32)
        kn = headnorm(acc, kg_ref[...]).astype(BF16)
        for i in range(hpc):
            val0 = _fox_slots(j * hpc + i)[0]
            k_o[:, val0:val0 + C_HEAD_DIM] = kn[:, i * C_HEAD_DIM:(i + 1) * C_HEAD_DIM]
        acc = jnp.dot(xb, w_ref[:, 3 * D_MODEL + j * ch:3 * D_MODEL + (j + 1) * ch],
                      preferred_element_type=F32)
        g_o[:, j * ch:(j + 1) * ch] = acc.astype(g_o.dtype)

    fl = jnp.dot(xb, w_ref[:, 4 * D_MODEL:4 * D_MODEL + LANES], preferred_element_type=F32) + bf_ref[...]
    logf = jnp.minimum(fl, 0.0) - jnp.log(1.0 + jnp.exp(-jnp.abs(fl)))
    c = _sum3(jnp.dot(tri_ref[...], _split3(logf), preferred_element_type=F32)) + carry_ref[...]
    carry_ref[...] = c[tm - 1:tm, :]
    c3 = _split3(c * LOG2E)
    aug_k = (jnp.dot(c3, selk_ref[...], preferred_element_type=F32) + onek_ref[...]).astype(BF16)
    half = LANES // 2
    for p in range(C_HEADS // 2):
        k_o[:, 2 * p * LANES + half:2 * p * LANES + LANES] = aug_k[:, p * LANES + half:(p + 1) * LANES]
        k_o[:, (2 * p + 1) * LANES:(2 * p + 1) * LANES + half] = aug_k[:, p * LANES:p * LANES + half]
    aug_q = lax.dot_general(selq_ref[...], c3, (((1,), (1,)), ((), ())), preferred_element_type=F32)
    aug_q = (aug_q + jnp.concatenate([oneq_ref[...]] * (tm // LANES), axis=1)).astype(BF16)
    for h in range(C_HEADS):
        aug0 = _fox_slots(h)[1]
        qt_o[aug0:aug0 + FOX_AUG, :] = aug_q[h * FOX_AUG:(h + 1) * FOX_AUG, :]


def _odd_proj(x2, w_bf, bf_pad, qg, kg, gmat, tri, B, S):
    T = B * S
    tm = PROJ_TM
    nt = S // tm
    tok = lambda n: pl.BlockSpec((tm, n), lambda b, t: (b * nt + t, 0))
    tspec = lambda rows: pl.BlockSpec((None, rows, tm), lambda b, t: (b, 0, t))
    selk, onek, selq, oneq = _fox_selectors()
    consts = [jnp.asarray(selk, BF16), jnp.asarray(onek, F32), jnp.asarray(selq, BF16),
              jnp.asarray(oneq, F32)]
    k_width = C_HEADS * LANES
    return pl.pallas_call(
        _odd_proj_kernel,
        grid=(B, nt),
        in_specs=[tok(D_MODEL), _full_spec(w_bf.shape), _full_spec(bf_pad.shape), _full_spec(qg.shape),
                  _full_spec(kg.shape), _full_spec(gmat.shape), _full_spec(tri.shape)]
                 + [_full_spec(c.shape) for c in consts],
        out_specs=[tspec(k_width), tok(k_width), tspec(C_HEADS * FOX_VROWS), tok(D_MODEL)],
        out_shape=[jax.ShapeDtypeStruct((B, k_width, S), BF16),
                   jax.ShapeDtypeStruct((T, k_width), BF16),
                   jax.ShapeDtypeStruct((B, C_HEADS * FOX_VROWS, S), BF16),
                   jax.ShapeDtypeStruct((T, D_MODEL), BF16)],
        scratch_shapes=[pltpu.VMEM((1, LANES), F32)],
        compiler_params=_cparams(("parallel", "arbitrary")),
        name="odd_proj",
    )(x2, w_bf, bf_pad, qg, kg, gmat, tri, *consts)


def _fox_kernel(qi_ref, ki_ref, q_ref, k_ref, v_ref, o_ref, ma, aa, mb, ab):
    step = pl.program_id(2)
    qi = qi_ref[step]
    ki = ki_ref[step]

    @pl.when(ki == 0)
    def _():
        _init_stats(ma, aa)
        _init_stats(mb, ab)

    def run(mask_fn):
        _attend(k_ref, v_ref, q_ref, [(0, ma, aa, 0, 0), (LANES, mb, ab, FOX_VROWS, LANES)], mask_fn)

    def mask_for(key_offset):
        def chunk_mask(k0, q0, klen):
            k0 = k0 + key_offset
            if k0 + klen <= q0 + 1:
                return "all"
            if k0 >= q0 + ATT_QC:
                return "none"
            kk = lax.broadcasted_iota(jnp.int32, (klen, ATT_QC), 0) + k0
            qq = lax.broadcasted_iota(jnp.int32, (klen, ATT_QC), 1) + q0
            return kk <= qq
        return chunk_mask

    def finalize():
        d = C_HEAD_DIM
        o = jnp.concatenate([aa[:d, :] / aa[d:d + 1, :], ab[:d, :] / ab[d:d + 1, :]], axis=0)
        o_ref[...] = o.T.astype(o_ref.dtype)

    _attend_tile(qi, ki, run, mask_for, finalize)


def _fox_attention(qt, k, vt, B, S):
    tq, tk = ATT_TQ, ATT_TK
    nq, nk = S // tq, S // tk
    qi, ki = _tri_schedule(nq)
    n_pairs = C_HEADS // 2
    stat = pltpu.VMEM((1, tq), F32)
    acc = pltpu.VMEM((FOX_VROWS, tq), F32)
    grid_spec = pltpu.PrefetchScalarGridSpec(
        num_scalar_prefetch=2,
        grid=(B, n_pairs, int(qi.shape[0])),
        in_specs=[
            pl.BlockSpec((None, 2 * LANES, tq), lambda b, h, s, qi, ki: (b, h, qi[s])),
            pl.BlockSpec((tk, 2 * LANES), lambda b, h, s, qi, ki: (b * nk + ki[s], h)),
            pl.BlockSpec((None, 2 * FOX_VROWS, tk), lambda b, h, s, qi, ki: (b, h, ki[s])),
        ],
        out_specs=pl.BlockSpec((tq, LANES), lambda b, h, s, qi, ki: (b * nq + qi[s], h)),
        scratch_shapes=[stat, acc, stat, acc],
    )
    return pl.pallas_call(
        _fox_kernel,
        grid_spec=grid_spec,
        out_shape=jax.ShapeDtypeStruct((B * S, D_MODEL), BF16),
        compiler_params=_cparams(("parallel", "parallel", "arbitrary")),
        name="fox_attn",
    )(qi, ki, qt, k, vt)


def _even_mix(oa_ref, ob_ref, w_ref):
    h = jnp.dot(oa_ref[...], w_ref[:A_WIDTH, :], preferred_element_type=F32)
    return h + jnp.dot(ob_ref[...], w_ref[A_WIDTH:, :], preferred_element_type=F32)


def _odd_mix(o_ref, gate_ref, w_ref):
    o = o_ref[...].astype(F32) * jax.nn.sigmoid(gate_ref[...].astype(F32))
    return jnp.dot(o.astype(BF16), w_ref[...], preferred_element_type=F32)


def _tail_kernel(a0_ref, a1_ref, x_ref, wo_ref, g1_ref, b1_ref, w1_ref, w2_ref, g2_ref, b2_ref,
                 y_ref, act_ref, *, mix):
    x = _layer_norm_rows(ALPHA * x_ref[...] + mix(a0_ref, a1_ref, wo_ref), g1_ref[...], b1_ref[...])
    xb = x.astype(BF16)
    ch = MXU_DIM
    for j in range(D_FF // ch):
        gate = jnp.dot(xb, w1_ref[:, j * ch:(j + 1) * ch], preferred_element_type=F32)
        up = jnp.dot(xb, w1_ref[:, D_FF + j * ch:D_FF + (j + 1) * ch], preferred_element_type=F32)
        act_ref[:, j * ch:(j + 1) * ch] = (gate * jax.nn.sigmoid(gate) * up).astype(BF16)
    h = jnp.dot(act_ref[...], w2_ref[...], preferred_element_type=F32)
    y_ref[...] = _layer_norm_rows(ALPHA * x + h, g2_ref[...], b2_ref[...])


def _layer_tail(mix, acts, x2, wo_bf, g1, b1, w1_bf, w2_bf, g2, b2, name):
    T = x2.shape[0]
    tm = FFN_TM
    tok = lambda n: pl.BlockSpec((tm, n), lambda i: (i, 0))
    consts = [wo_bf, g1, b1, w1_bf, w2_bf, g2, b2]
    return pl.pallas_call(
        functools.partial(_tail_kernel, mix=mix),
        grid=(T // tm,),
        in_specs=[tok(a.shape[1]) for a in acts] + [tok(D_MODEL)] + [_full_spec(c.shape) for c in consts],
        out_specs=tok(D_MODEL),
        out_shape=jax.ShapeDtypeStruct((T, D_MODEL), F32),
        scratch_shapes=[pltpu.VMEM((tm, D_FF), BF16)],
        compiler_params=_cparams(("parallel",)),
        name=name,
    )(*acts, x2, *consts)


def kernel(x, even_w_in, even_w_out, hgrn_lb_logits, diff_lq1, diff_lk1, diff_lq2, diff_lk2,
           hgrn_norm_g, diff_norm_g, fox_w_in, fox_w_out, fox_b_f, fox_qnorm_g, fox_knorm_g,
           ffn_w1, ffn_w2, ln1_g, ln1_b, ln2_g, ln2_b):
    B, S, D = x.shape
    assert D == D_MODEL and S % ATT_TQ == 0 and S % PROJ_TM == 0 and S % HGRN_L == 0
    T = B * S
    x2 = x.reshape(T, D).astype(F32)

    cmat = jnp.asarray(_hgrn_cumsum_matrix(HGRN_L), BF16)
    tri = jnp.asarray(np.tril(np.ones((PROJ_TM, PROJ_TM), np.float32)), BF16)
    head_of = np.arange(MXU_DIM) // C_HEAD_DIM
    gmat = jnp.asarray((head_of[:, None] == head_of[None, :]).astype(np.float32) / C_HEAD_DIM, BF16)
    row = lambda v: v.astype(F32).reshape(1, -1)

    for l in range(DEPTH):
        j = l // 2
        if l % 2 == 0:
            aq, af, ai, ag, bq, bk, bv = _even_proj(x2, even_w_in[j].astype(BF16), B, S)
            o_a = _hgrn(aq, af, ai, ag, hgrn_lb_logits.astype(F32), row(hgrn_norm_g[j]), cmat, j, B, S)
            lamv = jnp.zeros((8, B_HEAD_DIM), F32).at[0:4].set(
                jnp.stack([diff_lq1[j], diff_lk1[j], diff_lq2[j], diff_lk2[j]]).astype(F32))
            lam_init = 0.8 - 0.6 * math.exp(-0.3 * l)
            o_b = _diff_attention(bq, bk, bv, lamv, row(diff_norm_g[j]), lam_init, B, S)
            mix, acts, w_out, name = _even_mix, [o_a, o_b], even_w_out[j], "even_tail"
        else:
            w_pad = jnp.pad(fox_w_in[j], ((0, 0), (0, LANES - C_HEADS))).astype(BF16)
            bf_pad = jnp.pad(fox_b_f[j].astype(F32), (0, LANES - C_HEADS)).reshape(1, LANES)
            tile4 = lambda v: jnp.tile(v.astype(F32), MXU_DIM // C_HEAD_DIM).reshape(1, MXU_DIM)
            qt, k, vt, g = _odd_proj(x2, w_pad, bf_pad, tile4(fox_qnorm_g[j]),
                                     tile4(fox_knorm_g[j]), gmat, tri, B, S)
            o = _fox_attention(qt, k, vt, B, S)
            mix, acts, w_out, name = _odd_mix, [o, g], fox_w_out[j], "odd_tail"
        x2 = _layer_tail(mix, acts, x2, w_out.astype(BF16), row(ln1_g[l]), row(ln1_b[l]),
                         ffn_w1[l].astype(BF16), ffn_w2[l].astype(BF16), row(ln2_g[l]), row(ln2_b[l]), name)
    return x2.reshape(B, S, D).astype(x.dtype)
```

```python
import functools
import math

import numpy as np
import jax
import jax.numpy as jnp
from jax import lax
from jax.experimental import pallas as pl
from jax.experimental.pallas import tpu as pltpu

F32 = jnp.float32
BF16 = jnp.bfloat16

D_MODEL = 1024
DEPTH = 4
A_WIDTH = 512
A_HEADS = 4
A_DK = 128
B_WIDTH = 512
B_HEADS = 4
B_HEAD_DIM = 64
DIFF_CHUNK = 64
C_HEADS = 16
C_HEAD_DIM = 64
D_FF = 2816
ALPHA = (2 * DEPTH) ** 0.25
MASK_VALUE = -1e30

LANES = 128
MXU_DIM = 256

PROJ_TM = 512
FFN_TM = 512
ATT_TQ = 4096
ATT_TK = 4096
ATT_QC = 256
ATT_KC = 512
ATT_SUB = 64
ATT_AHEAD = 2
ATT_BEHIND = 2
VT_ROWS = LANES + 16
HGRN_L = 256
LOG2E = math.log2(math.e)
VMEM_LIMIT = 56 * 1024 * 1024


def _cparams(sem):
    return pltpu.CompilerParams(dimension_semantics=sem, vmem_limit_bytes=VMEM_LIMIT)


def _full_spec(shape):
    nd = len(shape)
    return pl.BlockSpec(shape, lambda *_: (0,) * nd)


def _split3(x):
    h = x.astype(BF16)
    r = x - h.astype(F32)
    m = r.astype(BF16)
    l = (r - m.astype(F32)).astype(BF16)
    return jnp.concatenate([h, m, l], axis=1)


def _sum3(y):
    return y[:, :LANES] + y[:, LANES:2 * LANES] + y[:, 2 * LANES:]


def _layer_norm_rows(y, g, b):
    mu = jnp.mean(y, axis=-1, keepdims=True)
    d = y - mu
    var = jnp.mean(d * d, axis=-1, keepdims=True)
    return d * lax.rsqrt(var + 1e-5) * g + b


def _even_proj_kernel(x_ref, w_ref, aq, af, ai, ag, bq, bk, bv):
    xb = x_ref[...].astype(BF16)
    outs = (aq, af, ai, ag, bq, bk, bv)
    for n, o in enumerate(outs):
        acc = jnp.dot(xb, w_ref[:, n * 512:(n + 1) * 512], preferred_element_type=F32)
        if o is bq:
            qt = (acc * (B_HEAD_DIM ** -0.5 * LOG2E)).T.astype(o.dtype)
            zero = jnp.zeros((B_HEAD_DIM, qt.shape[1]), o.dtype)
            for h in range(B_HEADS):
                r = 2 * h * LANES
                o[r:r + B_HEAD_DIM, :] = qt[h * LANES:h * LANES + B_HEAD_DIM, :]
                o[r + B_HEAD_DIM:r + LANES, :] = zero
                o[r + LANES:r + LANES + B_HEAD_DIM, :] = zero
                o[r + LANES + B_HEAD_DIM:r + 2 * LANES, :] = qt[h * LANES + B_HEAD_DIM:(h + 1) * LANES, :]
        elif o is bv:
            vt = acc.T.astype(o.dtype)
            ones = jnp.ones((VT_ROWS - LANES, vt.shape[1]), o.dtype)
            for h in range(B_HEADS):
                o[h * VT_ROWS:h * VT_ROWS + LANES, :] = vt[h * LANES:(h + 1) * LANES, :]
                o[h * VT_ROWS + LANES:(h + 1) * VT_ROWS, :] = ones
        else:
            o[...] = acc.astype(o.dtype)


def _even_proj(x2, w_bf, B, S):
    T = x2.shape[0]
    tm = PROJ_TM
    nt = S // tm
    tok = lambda dt: jax.ShapeDtypeStruct((T, 512), dt)
    spec = pl.BlockSpec((tm, 512), lambda i: (i, 0))
    tspec = lambda rows: pl.BlockSpec((None, rows, tm), lambda i: (i // nt, 0, i % nt))
    return pl.pallas_call(
        _even_proj_kernel,
        grid=(T // tm,),
        in_specs=[pl.BlockSpec((tm, D_MODEL), lambda i: (i, 0)), _full_spec(w_bf.shape)],
        out_specs=[spec, spec, spec, spec, tspec(2 * B_WIDTH), spec, tspec(B_HEADS * VT_ROWS)],
        out_shape=[tok(BF16), tok(F32), tok(BF16), tok(BF16),
                   jax.ShapeDtypeStruct((B, 2 * B_WIDTH, S), BF16), tok(BF16),
                   jax.ShapeDtypeStruct((B, B_HEADS * VT_ROWS, S), BF16)],
        compiler_params=_cparams(("parallel",)),
        name="even_proj",
    )(x2, w_bf)


def _hgrn_cumsum_matrix(L):
    idx = np.arange(L)
    mats = []
    c = 1
    while c < L:
        start = (idx // c) * c
        end = start + c - 1
        right = ((idx // c) % 2) == 1
        u = idx[None, :]
        m_right = (u >= start[:, None]) & (u <= idx[:, None])
        m_left = (u > idx[:, None]) & (u <= end[:, None])
        mats.append(np.where(right[:, None], m_right, m_left))
        c *= 2
    u = idx[None, :]
    mats.append(u <= idx[:, None])
    mats.append(u > idx[:, None])
    return np.concatenate(mats, axis=0).astype(np.float32)


def _split2(x):
    h = x.astype(BF16)
    m = (x - h.astype(F32)).astype(BF16)
    return jnp.concatenate([h, m], axis=1)


def _hgrn_kernel(q_ref, f_ref, i_ref, g_ref, lbl_ref, gn_ref, cmat_ref, o_ref, state_ref, *, layer_j):
    L = HGRN_L
    n_levels = int(math.log2(L))
    nt = (((1,), (1,)), ((), ()))

    @pl.when(pl.program_id(1) == 0)
    def _():
        state_ref[...] = jnp.zeros_like(state_ref)

    lbl = lbl_ref[...]
    e = jnp.exp(lbl - jnp.max(lbl, axis=0, keepdims=True))
    soft = e / jnp.sum(e, axis=0, keepdims=True)
    lb_all = jnp.sum(soft[:layer_j + 1], axis=0, keepdims=True) - soft[0:1]

    row = lax.broadcasted_iota(jnp.int32, (L, 1), 0)
    xor = lax.broadcasted_iota(jnp.int32, (L, L), 0) ^ lax.broadcasted_iota(jnp.int32, (L, L), 1)
    cmat = cmat_ref[...]

    heads = range(A_HEADS)
    cols = [slice(h * LANES, (h + 1) * LANES) for h in heads]
    q, k, gdec = [], [], []
    for h in heads:
        lb = lb_all[:, cols[h]]
        f = lb + (1.0 - lb) * jax.nn.sigmoid(f_ref[:, cols[h]])
        k.append(1.0 - f)
        qr = q_ref[:, cols[h]].astype(F32)
        q.append(qr * jax.nn.sigmoid(qr))
        ex = jnp.dot(cmat, _split2(jnp.log(f)), preferred_element_type=F32)
        gdec.append(jnp.exp(ex[:, :LANES] + ex[:, LANES:]))

    scores = []
    for h in heads:
        s = jnp.where(xor == 0, lax.dot_general(q[h].astype(BF16), k[h].astype(BF16), nt,
                                                preferred_element_type=F32), 0.0)
        for lvl in range(n_levels):
            g_l = gdec[h][lvl * L:(lvl + 1) * L, :]
            right = ((row >> lvl) & 1).astype(F32)
            ql = (q[h] * g_l * right).astype(BF16)
            kl = (k[h] * g_l * (1.0 - right)).astype(BF16)
            s_l = lax.dot_general(ql, kl, nt, preferred_element_type=F32)
            s = s + jnp.where(xor < (2 << lvl), s_l, 0.0)
        scores.append(s.astype(BF16))

    for h in heads:
        iv = i_ref[:, cols[h]]
        g_full = gdec[h][n_levels * L:(n_levels + 1) * L, :]
        g_rest = gdec[h][(n_levels + 1) * L:(n_levels + 2) * L, :]
        state_t = state_ref[h]
        o = lax.dot_general((q[h] * g_full).astype(BF16), state_t.astype(BF16), nt,
                            preferred_element_type=F32)
        o = o + jnp.dot(scores[h], iv, preferred_element_type=F32)
        k_out = (k[h] * g_rest).astype(BF16)
        upd = jnp.dot(iv.astype(F32).T.astype(BF16), k_out, preferred_element_type=F32)
        state_ref[h] = state_t * g_full[L - 1:L, :] + upd
        ms = jnp.mean(o * o, axis=-1, keepdims=True)
        gate = g_ref[:, cols[h]].astype(F32)
        o = o * lax.rsqrt(ms + 1e-6) * gn_ref[...] * (gate * jax.nn.sigmoid(gate))
        o_ref[:, cols[h]] = o.astype(o_ref.dtype)


def _hgrn(aq, af, ai, ag, lb_logits, gn, cmat, layer_j, B, S):
    L = HGRN_L
    nb = S // L
    blk = pl.BlockSpec((L, A_WIDTH), lambda b, t: (b * nb + t, 0))
    return pl.pallas_call(
        functools.partial(_hgrn_kernel, layer_j=layer_j),
        grid=(B, nb),
        in_specs=[blk, blk, blk, blk, _full_spec(lb_logits.shape),
                  _full_spec(gn.shape), _full_spec(cmat.shape)],
        out_specs=blk,
        out_shape=jax.ShapeDtypeStruct((B * S, A_WIDTH), BF16),
        scratch_shapes=[pltpu.VMEM((A_HEADS, LANES, LANES), F32)],
        compiler_params=_cparams(("parallel", "arbitrary")),
        name="hgrn2",
    )(aq, af, ai, ag, lb_logits, gn, cmat)


def _attend(k_ref, vt_ref, qt_ref, streams, mask_fn):
    tk = k_ref.shape[0]
    kd = LANES
    tq = qt_ref.shape[1]
    def verdict(mask, word):
        return isinstance(mask, str) and mask == word

    chains = []
    for kc in range(0, tk, ATT_KC):
        for st in streams:
            for qc in range(0, tq, ATT_QC):
                n_keys, mask = ATT_KC, None
                if mask_fn is not None:
                    half = ATT_KC // 2
                    if verdict(mask_fn(kc + half, qc, half), "none"):
                        n_keys = half
                    mask = mask_fn(kc, qc, n_keys)
                    if verdict(mask, "none"):
                        continue
                    if verdict(mask, "all"):
                        mask = None
                chains.append((st, qc, kc, n_keys, mask))

    def scores(st, qc, kc, n_keys, mask):
        s = jnp.dot(k_ref[kc:kc + n_keys, st[4]:st[4] + kd], qt_ref[st[0]:st[0] + kd, qc:qc + ATT_QC],
                    preferred_element_type=F32).astype(BF16)
        if mask is not None:
            s = jnp.where(mask, s, jnp.asarray(MASK_VALUE, BF16))
        return s

    def softmax(st, qc, s):
        m_ref = st[1]
        cols = slice(qc, qc + ATT_QC)
        m_prev = m_ref[:, cols]
        m_run, parts, maxes = m_prev, [], []
        for r0 in range(0, s.shape[0], ATT_SUB):
            s_g = s[r0:r0 + ATT_SUB, :]
            m_run = jnp.maximum(m_run, jnp.max(s_g, axis=0, keepdims=True).astype(F32))
            parts.append(jnp.exp2(s_g - m_run.astype(BF16)))
            maxes.append(m_run)
        m_new = m_run
        parts = [p_g if m_g is m_new else p_g * jnp.exp2(m_g - m_new).astype(BF16)
                 for p_g, m_g in zip(parts, maxes)]
        p = parts[0] if len(parts) == 1 else jnp.concatenate(parts, axis=0)
        alpha = jnp.exp2(m_prev - m_new)
        m_ref[:, cols] = m_new
        return p, alpha

    def weighted_values(st, qc, kc, p, alpha):
        acc_ref, v0 = st[2], st[3]
        cols = slice(qc, qc + ATT_QC)
        acc_ref[:, cols] = alpha * acc_ref[:, cols] + jnp.dot(
            vt_ref[v0:v0 + acc_ref.shape[0], kc:kc + p.shape[0]], p, preferred_element_type=F32)

    n = len(chains)
    ready = [scores(*c) for c in chains[:ATT_AHEAD]]
    pending = []
    for i, (st, qc, kc, _, _) in enumerate(chains):
        s = ready.pop(0)
        if i + ATT_AHEAD < n:
            ready.append(scores(*chains[i + ATT_AHEAD]))
        pending.append((st, qc, kc, *softmax(st, qc, s)))
        if len(pending) > ATT_BEHIND:
            weighted_values(*pending.pop(0))
    for item in pending:
        weighted_values(*item)


ATT_MULT = ATT_TQ // ATT_TK


def _tri_schedule(nq):
    qi = [i for i in range(nq) for _ in range((i + 1) * ATT_MULT)]
    ki = [j for i in range(nq) for j in range((i + 1) * ATT_MULT)]
    return jnp.asarray(qi, jnp.int32), jnp.asarray(ki, jnp.int32)


def _attend_tile(qi, ki, run, mask_for, finalize):
    rel = ki - qi * ATT_MULT

    @pl.when(rel < 0)
    def _():
        run(None)

    for r in range(ATT_MULT):
        @pl.when(rel == r)
        def _(r=r):
            run(mask_for(r * ATT_TK))
            if r == ATT_MULT - 1:
                finalize()


def _init_stats(m_ref, acc_ref):
    m_ref[...] = jnp.full_like(m_ref, MASK_VALUE)
    acc_ref[...] = jnp.zeros_like(acc_ref)


def _diff_kernel(qi_ref, ki_ref, q_ref, k_ref, v_ref, lam_ref, gn_ref, o_ref,
                 m1, a1, m2, a2, *, lam_init):
    step = pl.program_id(2)
    qi = qi_ref[step]
    ki = ki_ref[step]

    @pl.when(ki == 0)
    def _():
        _init_stats(m1, a1)
        _init_stats(m2, a2)

    def run(mask_fn):
        _attend(k_ref, v_ref, q_ref, [(0, m1, a1, 0, 0), (LANES, m2, a2, 0, 0)], mask_fn)

    def mask_for(key_offset):
        def chunk_mask(k0, q0, klen):
            k0 = k0 + key_offset
            if k0 + klen <= q0 + DIFF_CHUNK:
                return "all"
            if k0 >= q0 + ATT_QC:
                return "none"
            sh = DIFF_CHUNK.bit_length() - 1
            kk = (lax.broadcasted_iota(jnp.int32, (klen, ATT_QC), 0) + k0) >> sh
            qq = (lax.broadcasted_iota(jnp.int32, (klen, ATT_QC), 1) + q0) >> sh
            return kk <= qq
        return chunk_mask

    def finalize():
        lv = lam_ref[...]
        lam = (jnp.exp(jnp.sum(lv[0:1] * lv[1:2], axis=-1, keepdims=True))
               - jnp.exp(jnp.sum(lv[2:3] * lv[3:4], axis=-1, keepdims=True)) + lam_init)
        o = (a1[:LANES, :] / a1[LANES:LANES + 1, :]
             - lam * (a2[:LANES, :] / a2[LANES:LANES + 1, :])).T
        ms = jnp.mean(o * o, axis=-1, keepdims=True)
        o = o * lax.rsqrt(ms + 1e-6) * gn_ref[...] * (1.0 - lam_init)
        o_ref[...] = o.astype(o_ref.dtype)

    _attend_tile(qi, ki, run, mask_for, finalize)


def _diff_attention(bq, bk, bv, lamv, gn, lam_init, B, S):
    tq, tk = ATT_TQ, ATT_TK
    nq, nk = S // tq, S // tk
    qi, ki = _tri_schedule(nq)
    stat = pltpu.VMEM((1, tq), F32)
    acc = pltpu.VMEM((VT_ROWS, tq), F32)
    grid_spec = pltpu.PrefetchScalarGridSpec(
        num_scalar_prefetch=2,
        grid=(B, B_HEADS, int(qi.shape[0])),
        in_specs=[
            pl.BlockSpec((None, 2 * LANES, tq), lambda b, h, s, qi, ki: (b, h, qi[s])),
            pl.BlockSpec((tk, LANES), lambda b, h, s, qi, ki: (b * nk + ki[s], h)),
            pl.BlockSpec((None, VT_ROWS, tk), lambda b, h, s, qi, ki: (b, h, ki[s])),
            pl.BlockSpec(lamv.shape, lambda b, h, s, qi, ki: (0, 0)),
            pl.BlockSpec(gn.shape, lambda b, h, s, qi, ki: (0, 0)),
        ],
        out_specs=pl.BlockSpec((tq, LANES), lambda b, h, s, qi, ki: (b * nq + qi[s], h)),
        scratch_shapes=[stat, acc, stat, acc],
    )
    return pl.pallas_call(
        functools.partial(_diff_kernel, lam_init=lam_init),
        grid_spec=grid_spec,
        out_shape=jax.ShapeDtypeStruct((B * S, B_WIDTH), BF16),
        compiler_params=_cparams(("parallel", "parallel", "arbitrary")),
        name="diff_attn",
    )(qi, ki, bq, bk, bv, lamv, gn)


FOX_AUG = 16
FOX_VROWS = C_HEAD_DIM + 16
FOX_PAD = LANES - C_HEAD_DIM - FOX_AUG


def _fox_slots(h):
    base = (h // 2) * 2 * LANES
    if h % 2 == 0:
        return base, base + C_HEAD_DIM, base + C_HEAD_DIM + FOX_AUG
    return base + 2 * LANES - C_HEAD_DIM, base + LANES, base + LANES + FOX_AUG


def _fox_selectors():
    n_pairs = C_HEADS // 2
    selk = np.zeros((3 * LANES, n_pairs * LANES), np.float32)
    onek = np.zeros((1, n_pairs * LANES), np.float32)
    selq = np.zeros((C_HEADS * FOX_AUG, 3 * LANES), np.float32)
    oneq = np.zeros((C_HEADS * FOX_AUG, LANES), np.float32)
    for h in range(C_HEADS):
        p, odd = divmod(h, 2)
        lane0 = p * LANES + (0 if odd else C_HEAD_DIM)
        for piece in range(3):
            onek[0, lane0 + piece] = 1.0
            selk[piece * LANES + h, lane0 + 3 + piece] = -1.0
            selq[h * FOX_AUG + piece, piece * LANES + h] = 1.0
            oneq[h * FOX_AUG + 3 + piece, :] = 1.0
    return selk, onek, selq, oneq


def _odd_proj_kernel(x_ref, w_ref, bf_ref, qg_ref, kg_ref, gmat_ref, tri_ref,
                     selk_ref, onek_ref, selq_ref, oneq_ref,
                     qt_o, k_o, vt_o, g_o, carry_ref):
    @pl.when(pl.program_id(1) == 0)
    def _():
        carry_ref[...] = jnp.zeros_like(carry_ref)

    xb = x_ref[...].astype(BF16)
    tm = xb.shape[0]
    ch = MXU_DIM
    hpc = ch // C_HEAD_DIM
    gmat = gmat_ref[...]

    def headnorm(acc, g):
        ms = jnp.dot((acc * acc).astype(BF16), gmat, preferred_element_type=F32)
        return acc * lax.rsqrt(ms + 1e-6) * g

    zero_pad = jnp.zeros((FOX_PAD, tm), BF16)
    ones = jnp.ones((FOX_VROWS - C_HEAD_DIM, tm), BF16)
    for j in range(D_MODEL // ch):
        acc = jnp.dot(xb, w_ref[:, j * ch:(j + 1) * ch], preferred_element_type=F32)
        qt = (headnorm(acc, qg_ref[...]) * (C_HEAD_DIM ** -0.5 * LOG2E)).T.astype(BF16)
        acc = jnp.dot(xb, w_ref[:, 2 * D_MODEL + j * ch:2 * D_MODEL + (j + 1) * ch],
                      preferred_element_type=F32)
        vt = acc.T.astype(BF16)
        for i in range(hpc):
            h = j * hpc + i
            src = slice(i * C_HEAD_DIM, (i + 1) * C_HEAD_DIM)
            val0, _, zero0 = _fox_slots(h)
            qt_o[val0:val0 + C_HEAD_DIM, :] = qt[src, :]
            qt_o[zero0:zero0 + FOX_PAD, :] = zero_pad
            vt_o[h * FOX_VROWS:h * FOX_VROWS + C_HEAD_DIM, :] = vt[src, :]
            vt_o[h * FOX_VROWS + C_HEAD_DIM:(h + 1) * FOX_VROWS, :] = ones
        acc = jnp.dot(xb, w_ref[:, D_MODEL + j * ch:D_MODEL + (j + 1) * ch], preferred_element_type=F32)
        kn = headnorm(acc, kg_ref[...]).astype(BF16)
        for i in range(hpc):
            val0 = _fox_slots(j * hpc + i)[0]
            k_o[:, val0:val0 + C_HEAD_DIM] = kn[:, i * C_HEAD_DIM:(i + 1) * C_HEAD_DIM]
        acc = jnp.dot(xb, w_ref[:, 3 * D_MODEL + j * ch:3 * D_MODEL + (j + 1) * ch],
                      preferred_element_type=F32)
        g_o[:, j * ch:(j + 1) * ch] = acc.astype(g_o.dtype)

    fl = jnp.dot(xb, w_ref[:, 4 * D_MODEL:4 * D_MODEL + LANES], preferred_element_type=F32) + bf_ref[...]
    logf = jnp.minimum(fl, 0.0) - jnp.log(1.0 + jnp.exp(-jnp.abs(fl)))
    c = _sum3(jnp.dot(tri_ref[...], _split3(logf), preferred_element_type=F32)) + carry_ref[...]
    carry_ref[...] = c[tm - 1:tm, :]
    c3 = _split3(c * LOG2E)
    aug_k = (jnp.dot(c3, selk_ref[...], preferred_element_type=F32) + onek_ref[...]).astype(BF16)
    half = LANES // 2
    for p in range(C_HEADS // 2):
        k_o[:, 2 * p * LANES + half:2 * p * LANES + LANES] = aug_k[:, p * LANES + half:(p + 1) * LANES]
        k_o[:, (2 * p + 1) * LANES:(2 * p + 1) * LANES + half] = aug_k[:, p * LANES:p * LANES + half]
    aug_q = lax.dot_general(selq_ref[...], c3, (((1,), (1,)), ((), ())), preferred_element_type=F32)
    aug_q = (aug_q + jnp.concatenate([oneq_ref[...]] * (tm // LANES), axis=1)).astype(BF16)
    for h in range(C_HEADS):
        aug0 = _fox_slots(h)[1]
        qt_o[aug0:aug0 + FOX_AUG, :] = aug_q[h * FOX_AUG:(h + 1) * FOX_AUG, :]


def _odd_proj(x2, w_bf, bf_pad, qg, kg, gmat, tri, B, S):
    T = B * S
    tm = PROJ_TM
    nt = S // tm
    tok = lambda n: pl.BlockSpec((tm, n), lambda b, t: (b * nt + t, 0))
    tspec = lambda rows: pl.BlockSpec((None, rows, tm), lambda b, t: (b, 0, t))
    selk, onek, selq, oneq = _fox_selectors()
    consts = [jnp.asarray(selk, BF16), jnp.asarray(onek, F32), jnp.asarray(selq, BF16),
              jnp.asarray(oneq, F32)]
    k_width = C_HEADS * LANES
    return pl.pallas_call(
        _odd_proj_kernel,
        grid=(B, nt),
        in_specs=[tok(D_MODEL), _full_spec(w_bf.shape), _full_spec(bf_pad.shape), _full_spec(qg.shape),
                  _full_spec(kg.shape), _full_spec(gmat.shape), _full_spec(tri.shape)]
                 + [_full_spec(c.shape) for c in consts],
        out_specs=[tspec(k_width), tok(k_width), tspec(C_HEADS * FOX_VROWS), tok(D_MODEL)],
        out_shape=[jax.ShapeDtypeStruct((B, k_width, S), BF16),
                   jax.ShapeDtypeStruct((T, k_width), BF16),
                   jax.ShapeDtypeStruct((B, C_HEADS * FOX_VROWS, S), BF16),
                   jax.ShapeDtypeStruct((T, D_MODEL), BF16)],
        scratch_shapes=[pltpu.VMEM((1, LANES), F32)],
        compiler_params=_cparams(("parallel", "arbitrary")),
        name="odd_proj",
    )(x2, w_bf, bf_pad, qg, kg, gmat, tri, *consts)


def _fox_kernel(qi_ref, ki_ref, q_ref, k_ref, v_ref, o_ref, ma, aa, mb, ab):
    step = pl.program_id(2)
    qi = qi_ref[step]
    ki = ki_ref[step]

    @pl.when(ki == 0)
    def _():
        _init_stats(ma, aa)
        _init_stats(mb, ab)

    def run(mask_fn):
        _attend(k_ref, v_ref, q_ref, [(0, ma, aa, 0, 0), (LANES, mb, ab, FOX_VROWS, LANES)], mask_fn)

    def mask_for(key_offset):
        def chunk_mask(k0, q0, klen):
            k0 = k0 + key_offset
            if k0 + klen <= q0 + 1:
                return "all"
            if k0 >= q0 + ATT_QC:
                return "none"
            kk = lax.broadcasted_iota(jnp.int32, (klen, ATT_QC), 0) + k0
            qq = lax.broadcasted_iota(jnp.int32, (klen, ATT_QC), 1) + q0
            return kk <= qq
        return chunk_mask

    def finalize():
        d = C_HEAD_DIM
        o = jnp.concatenate([aa[:d, :] / aa[d:d + 1, :], ab[:d, :] / ab[d:d + 1, :]], axis=0)
        o_ref[...] = o.T.astype(o_ref.dtype)

    _attend_tile(qi, ki, run, mask_for, finalize)


def _fox_attention(qt, k, vt, B, S):
    tq, tk = ATT_TQ, ATT_TK
    nq, nk = S // tq, S // tk
    qi, ki = _tri_schedule(nq)
    n_pairs = C_HEADS // 2
    stat = pltpu.VMEM((1, tq), F32)
    acc = pltpu.VMEM((FOX_VROWS, tq), F32)
    grid_spec = pltpu.PrefetchScalarGridSpec(
        num_scalar_prefetch=2,
        grid=(B, n_pairs, int(qi.shape[0])),
        in_specs=[
            pl.BlockSpec((None, 2 * LANES, tq), lambda b, h, s, qi, ki: (b, h, qi[s])),
            pl.BlockSpec((tk, 2 * LANES), lambda b, h, s, qi, ki: (b * nk + ki[s], h)),
            pl.BlockSpec((None, 2 * FOX_VROWS, tk), lambda b, h, s, qi, ki: (b, h, ki[s])),
        ],
        out_specs=pl.BlockSpec((tq, LANES), lambda b, h, s, qi, ki: (b * nq + qi[s], h)),
        scratch_shapes=[stat, acc, stat, acc],
    )
    return pl.pallas_call(
        _fox_kernel,
        grid_spec=grid_spec,
        out_shape=jax.ShapeDtypeStruct((B * S, D_MODEL), BF16),
        compiler_params=_cparams(("parallel", "parallel", "arbitrary")),
        name="fox_attn",
    )(qi, ki, qt, k, vt)


def _even_mix(oa_ref, ob_ref, w_ref):
    h = jnp.dot(oa_ref[...], w_ref[:A_WIDTH, :], preferred_element_type=F32)
    return h + jnp.dot(ob_ref[...], w_ref[A_WIDTH:, :], preferred_element_type=F32)


def _odd_mix(o_ref, gate_ref, w_ref):
    o = o_ref[...].astype(F32) * jax.nn.sigmoid(gate_ref[...].astype(F32))
    return jnp.dot(o.astype(BF16), w_ref[...], preferred_element_type=F32)


def _tail_kernel(a0_ref, a1_ref, x_ref, wo_ref, g1_ref, b1_ref, w1_ref, w2_ref, g2_ref, b2_ref,
                 y_ref, act_ref, *, mix):
    x = _layer_norm_rows(ALPHA * x_ref[...] + mix(a0_ref, a1_ref, wo_ref), g1_ref[...], b1_ref[...])
    xb = x.astype(BF16)
    ch = MXU_DIM
    for j in range(D_FF // ch):
        gate = jnp.dot(xb, w1_ref[:, j * ch:(j + 1) * ch], preferred_element_type=F32)
        up = jnp.dot(xb, w1_ref[:, D_FF + j * ch:D_FF + (j + 1) * ch], preferred_element_type=F32)
        act_ref[:, j * ch:(j + 1) * ch] = (gate * jax.nn.sigmoid(gate) * up).astype(BF16)
    h = jnp.dot(act_ref[...], w2_ref[...], preferred_element_type=F32)
    y_ref[...] = _layer_norm_rows(ALPHA * x + h, g2_ref[...], b2_ref[...])


def _layer_tail(mix, acts, x2, wo_bf, g1, b1, w1_bf, w2_bf, g2, b2, name):
    T = x2.shape[0]
    tm = FFN_TM
    tok = lambda n: pl.BlockSpec((tm, n), lambda i: (i, 0))
    consts = [wo_bf, g1, b1, w1_bf, w2_bf, g2, b2]
    return pl.pallas_call(
        functools.partial(_tail_kernel, mix=mix),
        grid=(T // tm,),
        in_specs=[tok(a.shape[1]) for a in acts] + [tok(D_MODEL)] + [_full_spec(c.shape) for c in consts],
        out_specs=tok(D_MODEL),
        out_shape=jax.ShapeDtypeStruct((T, D_MODEL), F32),
        scratch_shapes=[pltpu.VMEM((tm, D_FF), BF16)],
        compiler_params=_cparams(("parallel",)),
        name=name,
    )(*acts, x2, *consts)


def kernel(x, even_w_in, even_w_out, hgrn_lb_logits, diff_lq1, diff_lk1, diff_lq2, diff_lk2,
           hgrn_norm_g, diff_norm_g, fox_w_in, fox_w_out, fox_b_f, fox_qnorm_g, fox_knorm_g,
           ffn_w1, ffn_w2, ln1_g, ln1_b, ln2_g, ln2_b):
    B, S, D = x.shape
    assert D == D_MODEL and S % ATT_TQ == 0 and S % PROJ_TM == 0 and S % HGRN_L == 0
    T = B * S
    x2 = x.reshape(T, D).astype(F32)

    cmat = jnp.asarray(_hgrn_cumsum_matrix(HGRN_L), BF16)
    tri = jnp.asarray(np.tril(np.ones((PROJ_TM, PROJ_TM), np.float32)), BF16)
    head_of = np.arange(MXU_DIM) // C_HEAD_DIM
    gmat = jnp.asarray((head_of[:, None] == head_of[None, :]).astype(np.float32) / C_HEAD_DIM, BF16)
    row = lambda v: v.astype(F32).reshape(1, -1)

    for l in range(DEPTH):
        j = l // 2
        if l % 2 == 0:
            aq, af, ai, ag, bq, bk, bv = _even_proj(x2, even_w_in[j].astype(BF16), B, S)
            o_a = _hgrn(aq, af, ai, ag, hgrn_lb_logits.astype(F32), row(hgrn_norm_g[j]), cmat, j, B, S)
            lamv = jnp.zeros((8, B_HEAD_DIM), F32).at[0:4].set(
                jnp.stack([diff_lq1[j], diff_lk1[j], diff_lq2[j], diff_lk2[j]]).astype(F32))
            lam_init = 0.8 - 0.6 * math.exp(-0.3 * l)
            o_b = _diff_attention(bq, bk, bv, lamv, row(diff_norm_g[j]), lam_init, B, S)
            mix, acts, w_out, name = _even_mix, [o_a, o_b], even_w_out[j], "even_tail"
        else:
            w_pad = jnp.pad(fox_w_in[j], ((0, 0), (0, LANES - C_HEADS))).astype(BF16)
            bf_pad = jnp.pad(fox_b_f[j].astype(F32), (0, LANES - C_HEADS)).reshape(1, LANES)
            tile4 = lambda v: jnp.tile(v.astype(F32), MXU_DIM // C_HEAD_DIM).reshape(1, MXU_DIM)
            qt, k, vt, g = _odd_proj(x2, w_pad, bf_pad, tile4(fox_qnorm_g[j]),
                                     tile4(fox_knorm_g[j]), gmat, tri, B, S)
            o = _fox_attention(qt, k, vt, B, S)
            mix, acts, w_out, name = _odd_mix, [o, g], fox_w_out[j], "odd_tail"
        x2 = _layer_tail(mix, acts, x2, w_out.astype(BF16), row(ln1_g[l]), row(ln1_b[l]),
                         ffn_w1[l].astype(BF16), ffn_w2[l].astype(BF16), row(ln2_g[l]), row(ln2_b[l]), name)
    return x2.reshape(B, S, D).astype(x.dtype)
```

```python
import functools
import math

import numpy as np
import jax
import jax.numpy as jnp
from jax import lax
from jax.experimental import pallas as pl
from jax.experimental.pallas import tpu as pltpu

F32 = jnp.float32
BF16 = jnp.bfloat16

D_MODEL = 1024
DEPTH = 4
A_WIDTH = 512
A_HEADS = 4
A_DK = 128
B_WIDTH = 512
B_HEADS = 4
B_HEAD_DIM = 64
DIFF_CHUNK = 64
C_HEADS = 16
C_HEAD_DIM = 64
D_FF = 2816
ALPHA = (2 * DEPTH) ** 0.25
MASK_VALUE = -1e30

LANES = 128
MXU_DIM = 256

PROJ_TM = 512
FFN_TM = 512
ATT_TQ = 4096
ATT_TK = 2048
ATT_QC = 512
ATT_KC = 512
ATT_SUB = 64
ATT_AHEAD = 2
ATT_BEHIND = 2
VT_ROWS = LANES + 16
HGRN_L = 256
LOG2E = math.log2(math.e)
VMEM_LIMIT = 56 * 1024 * 1024


def _cparams(sem):
    return pltpu.CompilerParams(dimension_semantics=sem, vmem_limit_bytes=VMEM_LIMIT)


def _full_spec(shape):
    nd = len(shape)
    return pl.BlockSpec(shape, lambda *_: (0,) * nd)


def _split3(x):
    h = x.astype(BF16)
    r = x - h.astype(F32)
    m = r.astype(BF16)
    l = (r - m.astype(F32)).astype(BF16)
    return jnp.concatenate([h, m, l], axis=1)


def _sum3(y):
    return y[:, :LANES] + y[:, LANES:2 * LANES] + y[:, 2 * LANES:]


def _layer_norm_rows(y, g, b):
    mu = jnp.mean(y, axis=-1, keepdims=True)
    d = y - mu
    var = jnp.mean(d * d, axis=-1, keepdims=True)
    return d * lax.rsqrt(var + 1e-5) * g + b


def _even_proj_kernel(x_ref, w_ref, aq, af, ai, ag, bq, bk, bv):
    xb = x_ref[...].astype(BF16)
    outs = (aq, af, ai, ag, bq, bk, bv)
    for n, o in enumerate(outs):
        acc = jnp.dot(xb, w_ref[:, n * 512:(n + 1) * 512], preferred_element_type=F32)
        if o is bq:
            qt = (acc * (B_HEAD_DIM ** -0.5 * LOG2E)).T.astype(o.dtype)
            zero = jnp.zeros((B_HEAD_DIM, qt.shape[1]), o.dtype)
            for h in range(B_HEADS):
                r = 2 * h * LANES
                o[r:r + B_HEAD_DIM, :] = qt[h * LANES:h * LANES + B_HEAD_DIM, :]
                o[r + B_HEAD_DIM:r + LANES, :] = zero
                o[r + LANES:r + LANES + B_HEAD_DIM, :] = zero
                o[r + LANES + B_HEAD_DIM:r + 2 * LANES, :] = qt[h * LANES + B_HEAD_DIM:(h + 1) * LANES, :]
        elif o is bv:
            vt = acc.T.astype(o.dtype)
            ones = jnp.ones((VT_ROWS - LANES, vt.shape[1]), o.dtype)
            for h in range(B_HEADS):
                o[h * VT_ROWS:h * VT_ROWS + LANES, :] = vt[h * LANES:(h + 1) * LANES, :]
                o[h * VT_ROWS + LANES:(h + 1) * VT_ROWS, :] = ones
        else:
            o[...] = acc.astype(o.dtype)


def _even_proj(x2, w_bf, B, S):
    T = x2.shape[0]
    tm = PROJ_TM
    nt = S // tm
    tok = lambda dt: jax.ShapeDtypeStruct((T, 512), dt)
    spec = pl.BlockSpec((tm, 512), lambda i: (i, 0))
    tspec = lambda rows: pl.BlockSpec((None, rows, tm), lambda i: (i // nt, 0, i % nt))
    return pl.pallas_call(
        _even_proj_kernel,
        grid=(T // tm,),
        in_specs=[pl.BlockSpec((tm, D_MODEL), lambda i: (i, 0)), _full_spec(w_bf.shape)],
        out_specs=[spec, spec, spec, spec, tspec(2 * B_WIDTH), spec, tspec(B_HEADS * VT_ROWS)],
        out_shape=[tok(BF16), tok(F32), tok(BF16), tok(BF16),
                   jax.ShapeDtypeStruct((B, 2 * B_WIDTH, S), BF16), tok(BF16),
                   jax.ShapeDtypeStruct((B, B_HEADS * VT_ROWS, S), BF16)],
        compiler_params=_cparams(("parallel",)),
        name="even_proj",
    )(x2, w_bf)


def _hgrn_cumsum_matrix(L):
    idx = np.arange(L)
    mats = []
    c = 1
    while c < L:
        start = (idx // c) * c
        end = start + c - 1
        right = ((idx // c) % 2) == 1
        u = idx[None, :]
        m_right = (u >= start[:, None]) & (u <= idx[:, None])
        m_left = (u > idx[:, None]) & (u <= end[:, None])
        mats.append(np.where(right[:, None], m_right, m_left))
        c *= 2
    u = idx[None, :]
    mats.append(u <= idx[:, None])
    mats.append(u > idx[:, None])
    return np.concatenate(mats, axis=0).astype(np.float32)


def _split2(x):
    h = x.astype(BF16)
    m = (x - h.astype(F32)).astype(BF16)
    return jnp.concatenate([h, m], axis=1)


def _hgrn_kernel(q_ref, f_ref, i_ref, g_ref, lbl_ref, gn_ref, cmat_ref, o_ref, state_ref, *, layer_j):
    L = HGRN_L
    n_levels = int(math.log2(L))
    nt = (((1,), (1,)), ((), ()))

    @pl.when(pl.program_id(1) == 0)
    def _():
        state_ref[...] = jnp.zeros_like(state_ref)

    lbl = lbl_ref[...]
    e = jnp.exp(lbl - jnp.max(lbl, axis=0, keepdims=True))
    soft = e / jnp.sum(e, axis=0, keepdims=True)
    lb_all = jnp.sum(soft[:layer_j + 1], axis=0, keepdims=True) - soft[0:1]

    row = lax.broadcasted_iota(jnp.int32, (L, 1), 0)
    xor = lax.broadcasted_iota(jnp.int32, (L, L), 0) ^ lax.broadcasted_iota(jnp.int32, (L, L), 1)
    cmat = cmat_ref[...]

    heads = range(A_HEADS)
    cols = [slice(h * LANES, (h + 1) * LANES) for h in heads]
    q, k, gdec = [], [], []
    for h in heads:
        lb = lb_all[:, cols[h]]
        f = lb + (1.0 - lb) * jax.nn.sigmoid(f_ref[:, cols[h]])
        k.append(1.0 - f)
        qr = q_ref[:, cols[h]].astype(F32)
        q.append(qr * jax.nn.sigmoid(qr))
        ex = jnp.dot(cmat, _split2(jnp.log(f)), preferred_element_type=F32)
        gdec.append(jnp.exp(ex[:, :LANES] + ex[:, LANES:]))

    scores = []
    for h in heads:
        s = jnp.where(xor == 0, lax.dot_general(q[h].astype(BF16), k[h].astype(BF16), nt,
                                                preferred_element_type=F32), 0.0)
        for lvl in range(n_levels):
            g_l = gdec[h][lvl * L:(lvl + 1) * L, :]
            right = ((row >> lvl) & 1).astype(F32)
            ql = (q[h] * g_l * right).astype(BF16)
            kl = (k[h] * g_l * (1.0 - right)).astype(BF16)
            s_l = lax.dot_general(ql, kl, nt, preferred_element_type=F32)
            s = s + jnp.where(xor < (2 << lvl), s_l, 0.0)
        scores.append(s.astype(BF16))

    for h in heads:
        iv = i_ref[:, cols[h]]
        g_full = gdec[h][n_levels * L:(n_levels + 1) * L, :]
        g_rest = gdec[h][(n_levels + 1) * L:(n_levels + 2) * L, :]
        state_t = state_ref[h]
        o = lax.dot_general((q[h] * g_full).astype(BF16), state_t.astype(BF16), nt,
                            preferred_element_type=F32)
        o = o + jnp.dot(scores[h], iv, preferred_element_type=F32)
        k_out = (k[h] * g_rest).astype(BF16)
        upd = jnp.dot(iv.astype(F32).T.astype(BF16), k_out, preferred_element_type=F32)
        state_ref[h] = state_t * g_full[L - 1:L, :] + upd
        ms = jnp.mean(o * o, axis=-1, keepdims=True)
        gate = g_ref[:, cols[h]].astype(F32)
        o = o * lax.rsqrt(ms + 1e-6) * gn_ref[...] * (gate * jax.nn.sigmoid(gate))
        o_ref[:, cols[h]] = o.astype(o_ref.dtype)


def _hgrn(aq, af, ai, ag, lb_logits, gn, cmat, layer_j, B, S):
    L = HGRN_L
    nb = S // L
    blk = pl.BlockSpec((L, A_WIDTH), lambda b, t: (b * nb + t, 0))
    return pl.pallas_call(
        functools.partial(_hgrn_kernel, layer_j=layer_j),
        grid=(B, nb),
        in_specs=[blk, blk, blk, blk, _full_spec(lb_logits.shape),
                  _full_spec(gn.shape), _full_spec(cmat.shape)],
        out_specs=blk,
        out_shape=jax.ShapeDtypeStruct((B * S, A_WIDTH), BF16),
        scratch_shapes=[pltpu.VMEM((A_HEADS, LANES, LANES), F32)],
        compiler_params=_cparams(("parallel", "arbitrary")),
        name="hgrn2",
    )(aq, af, ai, ag, lb_logits, gn, cmat)


def _attend(k_ref, vt_ref, qt_ref, streams, mask_fn):
    tk = k_ref.shape[0]
    kd = LANES
    tq = qt_ref.shape[1]
    def verdict(mask, word):
        return isinstance(mask, str) and mask == word

    chains = []
    for kc in range(0, tk, ATT_KC):
        for st in streams:
            for qc in range(0, tq, ATT_QC):
                n_keys, mask = ATT_KC, None
                if mask_fn is not None:
                    half = ATT_KC // 2
                    if verdict(mask_fn(kc + half, qc, half), "none"):
                        n_keys = half
                    mask = mask_fn(kc, qc, n_keys)
                    if verdict(mask, "none"):
                        continue
                    if verdict(mask, "all"):
                        mask = None
                chains.append((st, qc, kc, n_keys, mask))

    def scores(st, qc, kc, n_keys, mask):
        s = jnp.dot(k_ref[kc:kc + n_keys, st[4]:st[4] + kd], qt_ref[st[0]:st[0] + kd, qc:qc + ATT_QC],
                    preferred_element_type=F32).astype(BF16)
        if mask is not None:
            s = jnp.where(mask, s, jnp.asarray(MASK_VALUE, BF16))
        return s

    def softmax(st, qc, s):
        m_ref = st[1]
        cols = slice(qc, qc + ATT_QC)
        m_prev = m_ref[:, cols]
        m_run, parts, maxes = m_prev, [], []
        for r0 in range(0, s.shape[0], ATT_SUB):
            s_g = s[r0:r0 + ATT_SUB, :]
            m_run = jnp.maximum(m_run, jnp.max(s_g, axis=0, keepdims=True).astype(F32))
            parts.append(jnp.exp2(s_g - m_run.astype(BF16)))
            maxes.append(m_run)
        m_new = m_run
        parts = [p_g if m_g is m_new else p_g * jnp.exp2(m_g - m_new).astype(BF16)
                 for p_g, m_g in zip(parts, maxes)]
        p = parts[0] if len(parts) == 1 else jnp.concatenate(parts, axis=0)
        alpha = jnp.exp2(m_prev - m_new)
        m_ref[:, cols] = m_new
        return p, alpha

    def weighted_values(st, qc, kc, p, alpha):
        acc_ref, v0 = st[2], st[3]
        cols = slice(qc, qc + ATT_QC)
        acc_ref[:, cols] = alpha * acc_ref[:, cols] + jnp.dot(
            vt_ref[v0:v0 + acc_ref.shape[0], kc:kc + p.shape[0]], p, preferred_element_type=F32)

    n = len(chains)
    ready = [scores(*c) for c in chains[:ATT_AHEAD]]
    pending = []
    for i, (st, qc, kc, _, _) in enumerate(chains):
        s = ready.pop(0)
        if i + ATT_AHEAD < n:
            ready.append(scores(*chains[i + ATT_AHEAD]))
        pending.append((st, qc, kc, *softmax(st, qc, s)))
        if len(pending) > ATT_BEHIND:
            weighted_values(*pending.pop(0))
    for item in pending:
        weighted_values(*item)


ATT_MULT = ATT_TQ // ATT_TK


def _tri_schedule(nq):
    qi = [i for i in range(nq) for _ in range((i + 1) * ATT_MULT)]
    ki = [j for i in range(nq) for j in range((i + 1) * ATT_MULT)]
    return jnp.asarray(qi, jnp.int32), jnp.asarray(ki, jnp.int32)


def _attend_tile(qi, ki, run, mask_for, finalize):
    rel = ki - qi * ATT_MULT

    @pl.when(rel < 0)
    def _():
        run(None)

    for r in range(ATT_MULT):
        @pl.when(rel == r)
        def _(r=r):
            run(mask_for(r * ATT_TK))
            if r == ATT_MULT - 1:
                finalize()


def _init_stats(m_ref, acc_ref):
    m_ref[...] = jnp.full_like(m_ref, MASK_VALUE)
    acc_ref[...] = jnp.zeros_like(acc_ref)


def _diff_kernel(qi_ref, ki_ref, q_ref, k_ref, v_ref, lam_ref, gn_ref, o_ref,
                 m1, a1, m2, a2, *, lam_init):
    step = pl.program_id(2)
    qi = qi_ref[step]
    ki = ki_ref[step]

    @pl.when(ki == 0)
    def _():
        _init_stats(m1, a1)
        _init_stats(m2, a2)

    def run(mask_fn):
        _attend(k_ref, v_ref, q_ref, [(0, m1, a1, 0, 0), (LANES, m2, a2, 0, 0)], mask_fn)

    def mask_for(key_offset):
        def chunk_mask(k0, q0, klen):
            k0 = k0 + key_offset
            if k0 + klen <= q0 + DIFF_CHUNK:
                return "all"
            if k0 >= q0 + ATT_QC:
                return "none"
            sh = DIFF_CHUNK.bit_length() - 1
            kk = (lax.broadcasted_iota(jnp.int32, (klen, ATT_QC), 0) + k0) >> sh
            qq = (lax.broadcasted_iota(jnp.int32, (klen, ATT_QC), 1) + q0) >> sh
            return kk <= qq
        return chunk_mask

    def finalize():
        lv = lam_ref[...]
        lam = (jnp.exp(jnp.sum(lv[0:1] * lv[1:2], axis=-1, keepdims=True))
               - jnp.exp(jnp.sum(lv[2:3] * lv[3:4], axis=-1, keepdims=True)) + lam_init)
        o = (a1[:LANES, :] / a1[LANES:LANES + 1, :]
             - lam * (a2[:LANES, :] / a2[LANES:LANES + 1, :])).T
        ms = jnp.mean(o * o, axis=-1, keepdims=True)
        o = o * lax.rsqrt(ms + 1e-6) * gn_ref[...] * (1.0 - lam_init)
        o_ref[...] = o.astype(o_ref.dtype)

    _attend_tile(qi, ki, run, mask_for, finalize)


def _diff_attention(bq, bk, bv, lamv, gn, lam_init, B, S):
    tq, tk = ATT_TQ, ATT_TK
    nq, nk = S // tq, S // tk
    qi, ki = _tri_schedule(nq)
    stat = pltpu.VMEM((1, tq), F32)
    acc = pltpu.VMEM((VT_ROWS, tq), F32)
    grid_spec = pltpu.PrefetchScalarGridSpec(
        num_scalar_prefetch=2,
        grid=(B, B_HEADS, int(qi.shape[0])),
        in_specs=[
            pl.BlockSpec((None, 2 * LANES, tq), lambda b, h, s, qi, ki: (b, h, qi[s])),
            pl.BlockSpec((tk, LANES), lambda b, h, s, qi, ki: (b * nk + ki[s], h)),
            pl.BlockSpec((None, VT_ROWS, tk), lambda b, h, s, qi, ki: (b, h, ki[s])),
            pl.BlockSpec(lamv.shape, lambda b, h, s, qi, ki: (0, 0)),
            pl.BlockSpec(gn.shape, lambda b, h, s, qi, ki: (0, 0)),
        ],
        out_specs=pl.BlockSpec((tq, LANES), lambda b, h, s, qi, ki: (b * nq + qi[s], h)),
        scratch_shapes=[stat, acc, stat, acc],
    )
    return pl.pallas_call(
        functools.partial(_diff_kernel, lam_init=lam_init),
        grid_spec=grid_spec,
        out_shape=jax.ShapeDtypeStruct((B * S, B_WIDTH), BF16),
        compiler_params=_cparams(("parallel", "parallel", "arbitrary")),
        name="diff_attn",
    )(qi, ki, bq, bk, bv, lamv, gn)


FOX_AUG = 16
FOX_VROWS = C_HEAD_DIM + 16
FOX_PAD = LANES - C_HEAD_DIM - FOX_AUG


def _fox_slots(h):
    base = (h // 2) * 2 * LANES
    if h % 2 == 0:
        return base, base + C_HEAD_DIM, base + C_HEAD_DIM + FOX_AUG
    return base + 2 * LANES - C_HEAD_DIM, base + LANES, base + LANES + FOX_AUG


def _fox_selectors():
    n_pairs = C_HEADS // 2
    selk = np.zeros((3 * LANES, n_pairs * LANES), np.float32)
    onek = np.zeros((1, n_pairs * LANES), np.float32)
    selq = np.zeros((C_HEADS * FOX_AUG, 3 * LANES), np.float32)
    oneq = np.zeros((C_HEADS * FOX_AUG, LANES), np.float32)
    for h in range(C_HEADS):
        p, odd = divmod(h, 2)
        lane0 = p * LANES + (0 if odd else C_HEAD_DIM)
        for piece in range(3):
            onek[0, lane0 + piece] = 1.0
            selk[piece * LANES + h, lane0 + 3 + piece] = -1.0
            selq[h * FOX_AUG + piece, piece * LANES + h] = 1.0
            oneq[h * FOX_AUG + 3 + piece, :] = 1.0
    return selk, onek, selq, oneq


def _odd_proj_kernel(x_ref, w_ref, bf_ref, qg_ref, kg_ref, gmat_ref, tri_ref,
                     selk_ref, onek_ref, selq_ref, oneq_ref,
                     qt_o, k_o, vt_o, g_o, carry_ref):
    @pl.when(pl.program_id(1) == 0)
    def _():
        carry_ref[...] = jnp.zeros_like(carry_ref)

    xb = x_ref[...].astype(BF16)
    tm = xb.shape[0]
    ch = MXU_DIM
    hpc = ch // C_HEAD_DIM
    gmat = gmat_ref[...]

    def headnorm(acc, g):
        ms = jnp.dot((acc * acc).astype(BF16), gmat, preferred_element_type=F32)
        return acc * lax.rsqrt(ms + 1e-6) * g

    zero_pad = jnp.zeros((FOX_PAD, tm), BF16)
    ones = jnp.ones((FOX_VROWS - C_HEAD_DIM, tm), BF16)
    for j in range(D_MODEL // ch):
        acc = jnp.dot(xb, w_ref[:, j * ch:(j + 1) * ch], preferred_element_type=F32)
        qt = (headnorm(acc, qg_ref[...]) * (C_HEAD_DIM ** -0.5 * LOG2E)).T.astype(BF16)
        acc = jnp.dot(xb, w_ref[:, 2 * D_MODEL + j * ch:2 * D_MODEL + (j + 1) * ch],
                      preferred_element_type=F32)
        vt = acc.T.astype(BF16)
        for i in range(hpc):
            h = j * hpc + i
            src = slice(i * C_HEAD_DIM, (i + 1) * C_HEAD_DIM)
            val0, _, zero0 = _fox_slots(h)
            qt_o[val0:val0 + C_HEAD_DIM, :] = qt[src, :]
            qt_o[zero0:zero0 + FOX_PAD, :] = zero_pad
            vt_o[h * FOX_VROWS:h * FOX_VROWS + C_HEAD_DIM, :] = vt[src, :]
            vt_o[h * FOX_VROWS + C_HEAD_DIM:(h + 1) * FOX_VROWS, :] = ones
        acc = jnp.dot(xb, w_ref[:, D_MODEL + j * ch:D_MODEL + (j + 1) * ch], preferred_element_type=F32)
        kn = headnorm(acc, kg_ref[...]).astype(BF16)
        for i in range(hpc):
            val0 = _fox_slots(j * hpc + i)[0]
            k_o[:, val0:val0 + C_HEAD_DIM] = kn[:, i * C_HEAD_DIM:(i + 1) * C_HEAD_DIM]
        acc = jnp.dot(xb, w_ref[:, 3 * D_MODEL + j * ch:3 * D_MODEL + (j + 1) * ch],
                      preferred_element_type=F32)
        g_o[:, j * ch:(j + 1) * ch] = acc.astype(g_o.dtype)

    fl = jnp.dot(xb, w_ref[:, 4 * D_MODEL:4 * D_MODEL + LANES], preferred_element_type=F32) + bf_ref[...]
    logf = jnp.minimum(fl, 0.0) - jnp.log(1.0 + jnp.exp(-jnp.abs(fl)))
    c = _sum3(jnp.dot(tri_ref[...], _split3(logf), preferred_element_type=F32)) + carry_ref[...]
    carry_ref[...] = c[tm - 1:tm, :]
    c3 = _split3(c * LOG2E)
    aug_k = (jnp.dot(c3, selk_ref[...], preferred_element_type=F32) + onek_ref[...]).astype(BF16)
    half = LANES // 2
    for p in range(C_HEADS // 2):
        k_o[:, 2 * p * LANES + half:2 * p * LANES + LANES] = aug_k[:, p * LANES + half:(p + 1) * LANES]
        k_o[:, (2 * p + 1) * LANES:(2 * p + 1) * LANES + half] = aug_k[:, p * LANES:p * LANES + half]
    aug_q = lax.dot_general(selq_ref[...], c3, (((1,), (1,)), ((), ())), preferred_element_type=F32)
    aug_q = (aug_q + jnp.concatenate([oneq_ref[...]] * (tm // LANES), axis=1)).astype(BF16)
    for h in range(C_HEADS):
        aug0 = _fox_slots(h)[1]
        qt_o[aug0:aug0 + FOX_AUG, :] = aug_q[h * FOX_AUG:(h + 1) * FOX_AUG, :]


def _odd_proj(x2, w_bf, bf_pad, qg, kg, gmat, tri, B, S):
    T = B * S
    tm = PROJ_TM
    nt = S // tm
    tok = lambda n: pl.BlockSpec((tm, n), lambda b, t: (b * nt + t, 0))
    tspec = lambda rows: pl.BlockSpec((None, rows, tm), lambda b, t: (b, 0, t))
    selk, onek, selq, oneq = _fox_selectors()
    consts = [jnp.asarray(selk, BF16), jnp.asarray(onek, F32), jnp.asarray(selq, BF16),
              jnp.asarray(oneq, F32)]
    k_width = C_HEADS * LANES
    return pl.pallas_call(
        _odd_proj_kernel,
        grid=(B, nt),
        in_specs=[tok(D_MODEL), _full_spec(w_bf.shape), _full_spec(bf_pad.shape), _full_spec(qg.shape),
                  _full_spec(kg.shape), _full_spec(gmat.shape), _full_spec(tri.shape)]
                 + [_full_spec(c.shape) for c in consts],
        out_specs=[tspec(k_width), tok(k_width), tspec(C_HEADS * FOX_VROWS), tok(D_MODEL)],
        out_shape=[jax.ShapeDtypeStruct((B, k_width, S), BF16),
                   jax.ShapeDtypeStruct((T, k_width), BF16),
                   jax.ShapeDtypeStruct((B, C_HEADS * FOX_VROWS, S), BF16),
                   jax.ShapeDtypeStruct((T, D_MODEL), BF16)],
        scratch_shapes=[pltpu.VMEM((1, LANES), F32)],
        compiler_params=_cparams(("parallel", "arbitrary")),
        name="odd_proj",
    )(x2, w_bf, bf_pad, qg, kg, gmat, tri, *consts)


def _fox_kernel(qi_ref, ki_ref, q_ref, k_ref, v_ref, o_ref, ma, aa, mb, ab):
    step = pl.program_id(2)
    qi = qi_ref[step]
    ki = ki_ref[step]

    @pl.when(ki == 0)
    def _():
        _init_stats(ma, aa)
        _init_stats(mb, ab)

    def run(mask_fn):
        _attend(k_ref, v_ref, q_ref, [(0, ma, aa, 0, 0), (LANES, mb, ab, FOX_VROWS, LANES)], mask_fn)

    def mask_for(key_offset):
        def chunk_mask(k0, q0, klen):
            k0 = k0 + key_offset
            if k0 + klen <= q0 + 1:
                return "all"
            if k0 >= q0 + ATT_QC:
                return "none"
            kk = lax.broadcasted_iota(jnp.int32, (klen, ATT_QC), 0) + k0
            qq = lax.broadcasted_iota(jnp.int32, (klen, ATT_QC), 1) + q0
            return kk <= qq
        return chunk_mask

    def finalize():
        d = C_HEAD_DIM
        o = jnp.concatenate([aa[:d, :] / aa[d:d + 1, :], ab[:d, :] / ab[d:d + 1, :]], axis=0)
        o_ref[...] = o.T.astype(o_ref.dtype)

    _attend_tile(qi, ki, run, mask_for, finalize)


def _fox_attention(qt, k, vt, B, S):
    tq, tk = ATT_TQ, ATT_TK
    nq, nk = S // tq, S // tk
    qi, ki = _tri_schedule(nq)
    n_pairs = C_HEADS // 2
    stat = pltpu.VMEM((1, tq), F32)
    acc = pltpu.VMEM((FOX_VROWS, tq), F32)
    grid_spec = pltpu.PrefetchScalarGridSpec(
        num_scalar_prefetch=2,
        grid=(B, n_pairs, int(qi.shape[0])),
        in_specs=[
            pl.BlockSpec((None, 2 * LANES, tq), lambda b, h, s, qi, ki: (b, h, qi[s])),
            pl.BlockSpec((tk, 2 * LANES), lambda b, h, s, qi, ki: (b * nk + ki[s], h)),
            pl.BlockSpec((None, 2 * FOX_VROWS, tk), lambda b, h, s, qi, ki: (b, h, ki[s])),
        ],
        out_specs=pl.BlockSpec((tq, LANES), lambda b, h, s, qi, ki: (b * nq + qi[s], h)),
        scratch_shapes=[stat, acc, stat, acc],
    )
    return pl.pallas_call(
        _fox_kernel,
        grid_spec=grid_spec,
        out_shape=jax.ShapeDtypeStruct((B * S, D_MODEL), BF16),
        compiler_params=_cparams(("parallel", "parallel", "arbitrary")),
        name="fox_attn",
    )(qi, ki, qt, k, vt)


def _even_mix(oa_ref, ob_ref, w_ref):
    h = jnp.dot(oa_ref[...], w_ref[:A_WIDTH, :], preferred_element_type=F32)
    return h + jnp.dot(ob_ref[...], w_ref[A_WIDTH:, :], preferred_element_type=F32)


def _odd_mix(o_ref, gate_ref, w_ref):
    o = o_ref[...].astype(F32) * jax.nn.sigmoid(gate_ref[...].astype(F32))
    return jnp.dot(o.astype(BF16), w_ref[...], preferred_element_type=F32)


def _tail_kernel(a0_ref, a1_ref, x_ref, wo_ref, g1_ref, b1_ref, w1_ref, w2_ref, g2_ref, b2_ref,
                 y_ref, act_ref, *, mix):
    x = _layer_norm_rows(ALPHA * x_ref[...] + mix(a0_ref, a1_ref, wo_ref), g1_ref[...], b1_ref[...])
    xb = x.astype(BF16)
    ch = MXU_DIM
    for j in range(D_FF // ch):
        gate = jnp.dot(xb, w1_ref[:, j * ch:(j + 1) * ch], preferred_element_type=F32)
        up = jnp.dot(xb, w1_ref[:, D_FF + j * ch:D_FF + (j + 1) * ch], preferred_element_type=F32)
        act_ref[:, j * ch:(j + 1) * ch] = (gate * jax.nn.sigmoid(gate) * up).astype(BF16)
    h = jnp.dot(act_ref[...], w2_ref[...], preferred_element_type=F32)
    y_ref[...] = _layer_norm_rows(ALPHA * x + h, g2_ref[...], b2_ref[...])


def _layer_tail(mix, acts, x2, wo_bf, g1, b1, w1_bf, w2_bf, g2, b2, name):
    T = x2.shape[0]
    tm = FFN_TM
    tok = lambda n: pl.BlockSpec((tm, n), lambda i: (i, 0))
    consts = [wo_bf, g1, b1, w1_bf, w2_bf, g2, b2]
    return pl.pallas_call(
        functools.partial(_tail_kernel, mix=mix),
        grid=(T // tm,),
        in_specs=[tok(a.shape[1]) for a in acts] + [tok(D_MODEL)] + [_full_spec(c.shape) for c in consts],
        out_specs=tok(D_MODEL),
        out_shape=jax.ShapeDtypeStruct((T, D_MODEL), F32),
        scratch_shapes=[pltpu.VMEM((tm, D_FF), BF16)],
        compiler_params=_cparams(("parallel",)),
        name=name,
    )(*acts, x2, *consts)


def kernel(x, even_w_in, even_w_out, hgrn_lb_logits, diff_lq1, diff_lk1, diff_lq2, diff_lk2,
           hgrn_norm_g, diff_norm_g, fox_w_in, fox_w_out, fox_b_f, fox_qnorm_g, fox_knorm_g,
           ffn_w1, ffn_w2, ln1_g, ln1_b, ln2_g, ln2_b):
    B, S, D = x.shape
    assert D == D_MODEL and S % ATT_TQ == 0 and S % PROJ_TM == 0 and S % HGRN_L == 0
    T = B * S
    x2 = x.reshape(T, D).astype(F32)

    cmat = jnp.asarray(_hgrn_cumsum_matrix(HGRN_L), BF16)
    tri = jnp.asarray(np.tril(np.ones((PROJ_TM, PROJ_TM), np.float32)), BF16)
    head_of = np.arange(MXU_DIM) // C_HEAD_DIM
    gmat = jnp.asarray((head_of[:, None] == head_of[None, :]).astype(np.float32) / C_HEAD_DIM, BF16)
    row = lambda v: v.astype(F32).reshape(1, -1)

    for l in range(DEPTH):
        j = l // 2
        if l % 2 == 0:
            aq, af, ai, ag, bq, bk, bv = _even_proj(x2, even_w_in[j].astype(BF16), B, S)
            o_a = _hgrn(aq, af, ai, ag, hgrn_lb_logits.astype(F32), row(hgrn_norm_g[j]), cmat, j, B, S)
            lamv = jnp.zeros((8, B_HEAD_DIM), F32).at[0:4].set(
                jnp.stack([diff_lq1[j], diff_lk1[j], diff_lq2[j], diff_lk2[j]]).astype(F32))
            lam_init = 0.8 - 0.6 * math.exp(-0.3 * l)
            o_b = _diff_attention(bq, bk, bv, lamv, row(diff_norm_g[j]), lam_init, B, S)
            mix, acts, w_out, name = _even_mix, [o_a, o_b], even_w_out[j], "even_tail"
        else:
            w_pad = jnp.pad(fox_w_in[j], ((0, 0), (0, LANES - C_HEADS))).astype(BF16)
            bf_pad = jnp.pad(fox_b_f[j].astype(F32), (0, LANES - C_HEADS)).reshape(1, LANES)
            tile4 = lambda v: jnp.tile(v.astype(F32), MXU_DIM // C_HEAD_DIM).reshape(1, MXU_DIM)
            qt, k, vt, g = _odd_proj(x2, w_pad, bf_pad, tile4(fox_qnorm_g[j]),
                                     tile4(fox_knorm_g[j]), gmat, tri, B, S)
            o = _fox_attention(qt, k, vt, B, S)
            mix, acts, w_out, name = _odd_mix, [o, g], fox_w_out[j], "odd_tail"
        x2 = _layer_tail(mix, acts, x2, w_out.astype(BF16), row(ln1_g[l]), row(ln1_b[l]),
                         ffn_w1[l].astype(BF16), ffn_w2[l].astype(BF16), row(ln2_g[l]), row(ln2_b[l]), name)
    return x2.reshape(B, S, D).astype(x.dtype)
```

```python
import functools
import math

import numpy as np
import jax
import jax.numpy as jnp
from jax import lax
from jax.experimental import pallas as pl
from jax.experimental.pallas import tpu as pltpu

F32 = jnp.float32
BF16 = jnp.bfloat16

D_MODEL = 1024
DEPTH = 4
A_WIDTH = 512
A_HEADS = 4
A_DK = 128
B_WIDTH = 512
B_HEADS = 4
B_HEAD_DIM = 64
DIFF_CHUNK = 64
C_HEADS = 16
C_HEAD_DIM = 64
D_FF = 2816
ALPHA = (2 * DEPTH) ** 0.25
MASK_VALUE = -1e30

LANES = 128
MXU_DIM = 256

PROJ_TM = 512
FFN_TM = 512
ATT_TQ = 4096
ATT_TK = 2048
ATT_QC = 256
ATT_KC = 512
ATT_SUB = 64
ATT_AHEAD = 2
ATT_BEHIND = 2
VT_ROWS = LANES + 16
HGRN_L = 256
LOG2E = math.log2(math.e)
VMEM_LIMIT = 56 * 1024 * 1024


def _cparams(sem):
    return pltpu.CompilerParams(dimension_semantics=sem, vmem_limit_bytes=VMEM_LIMIT)


def _full_spec(shape):
    nd = len(shape)
    return pl.BlockSpec(shape, lambda *_: (0,) * nd)


def _split3(x):
    h = x.astype(BF16)
    r = x - h.astype(F32)
    m = r.astype(BF16)
    l = (r - m.astype(F32)).astype(BF16)
    return jnp.concatenate([h, m, l], axis=1)


def _sum3(y):
    return y[:, :LANES] + y[:, LANES:2 * LANES] + y[:, 2 * LANES:]


def _layer_norm_rows(y, g, b):
    mu = jnp.mean(y, axis=-1, keepdims=True)
    d = y - mu
    var = jnp.mean(d * d, axis=-1, keepdims=True)
    return d * lax.rsqrt(var + 1e-5) * g + b


def _even_proj_kernel(x_ref, w_ref, aq, af, ai, ag, bq, bk, bv):
    xb = x_ref[...].astype(BF16)
    outs = (aq, af, ai, ag, bq, bk, bv)
    for n, o in enumerate(outs):
        acc = jnp.dot(xb, w_ref[:, n * 512:(n + 1) * 512], preferred_element_type=F32)
        if o is bq:
            qt = (acc * (B_HEAD_DIM ** -0.5 * LOG2E)).T.astype(o.dtype)
            zero = jnp.zeros((B_HEAD_DIM, qt.shape[1]), o.dtype)
            for h in range(B_HEADS):
                r = 2 * h * LANES
                o[r:r + B_HEAD_DIM, :] = qt[h * LANES:h * LANES + B_HEAD_DIM, :]
                o[r + B_HEAD_DIM:r + LANES, :] = zero
                o[r + LANES:r + LANES + B_HEAD_DIM, :] = zero
                o[r + LANES + B_HEAD_DIM:r + 2 * LANES, :] = qt[h * LANES + B_HEAD_DIM:(h + 1) * LANES, :]
        elif o is bv:
            vt = acc.T.astype(o.dtype)
            ones = jnp.ones((VT_ROWS - LANES, vt.shape[1]), o.dtype)
            for h in range(B_HEADS):
                o[h * VT_ROWS:h * VT_ROWS + LANES, :] = vt[h * LANES:(h + 1) * LANES, :]
                o[h * VT_ROWS + LANES:(h + 1) * VT_ROWS, :] = ones
        else:
            o[...] = acc.astype(o.dtype)


def _even_proj(x2, w_bf, B, S):
    T = x2.shape[0]
    tm = PROJ_TM
    nt = S // tm
    tok = lambda dt: jax.ShapeDtypeStruct((T, 512), dt)
    spec = pl.BlockSpec((tm, 512), lambda i: (i, 0))
    tspec = lambda rows: pl.BlockSpec((None, rows, tm), lambda i: (i // nt, 0, i % nt))
    return pl.pallas_call(
        _even_proj_kernel,
        grid=(T // tm,),
        in_specs=[pl.BlockSpec((tm, D_MODEL), lambda i: (i, 0)), _full_spec(w_bf.shape)],
        out_specs=[spec, spec, spec, spec, tspec(2 * B_WIDTH), spec, tspec(B_HEADS * VT_ROWS)],
        out_shape=[tok(BF16), tok(F32), tok(BF16), tok(BF16),
                   jax.ShapeDtypeStruct((B, 2 * B_WIDTH, S), BF16), tok(BF16),
                   jax.ShapeDtypeStruct((B, B_HEADS * VT_ROWS, S), BF16)],
        compiler_params=_cparams(("parallel",)),
        name="even_proj",
    )(x2, w_bf)


def _hgrn_cumsum_matrix(L):
    idx = np.arange(L)
    mats = []
    c = 1
    while c < L:
        start = (idx // c) * c
        end = start + c - 1
        right = ((idx // c) % 2) == 1
        u = idx[None, :]
        m_right = (u >= start[:, None]) & (u <= idx[:, None])
        m_left = (u > idx[:, None]) & (u <= end[:, None])
        mats.append(np.where(right[:, None], m_right, m_left))
        c *= 2
    u = idx[None, :]
    mats.append(u <= idx[:, None])
    mats.append(u > idx[:, None])
    return np.concatenate(mats, axis=0).astype(np.float32)


def _split2(x):
    h = x.astype(BF16)
    m = (x - h.astype(F32)).astype(BF16)
    return jnp.concatenate([h, m], axis=1)


def _hgrn_kernel(q_ref, f_ref, i_ref, g_ref, lbl_ref, gn_ref, cmat_ref, o_ref, state_ref, *, layer_j):
    L = HGRN_L
    n_levels = int(math.log2(L))
    nt = (((1,), (1,)), ((), ()))

    @pl.when(pl.program_id(1) == 0)
    def _():
        state_ref[...] = jnp.zeros_like(state_ref)

    lbl = lbl_ref[...]
    e = jnp.exp(lbl - jnp.max(lbl, axis=0, keepdims=True))
    soft = e / jnp.sum(e, axis=0, keepdims=True)
    lb_all = jnp.sum(soft[:layer_j + 1], axis=0, keepdims=True) - soft[0:1]

    row = lax.broadcasted_iota(jnp.int32, (L, 1), 0)
    xor = lax.broadcasted_iota(jnp.int32, (L, L), 0) ^ lax.broadcasted_iota(jnp.int32, (L, L), 1)
    cmat = cmat_ref[...]

    heads = range(A_HEADS)
    cols = [slice(h * LANES, (h + 1) * LANES) for h in heads]
    q, k, gdec = [], [], []
    for h in heads:
        lb = lb_all[:, cols[h]]
        f = lb + (1.0 - lb) * jax.nn.sigmoid(f_ref[:, cols[h]])
        k.append(1.0 - f)
        qr = q_ref[:, cols[h]].astype(F32)
        q.append(qr * jax.nn.sigmoid(qr))
        ex = jnp.dot(cmat, _split2(jnp.log(f)), preferred_element_type=F32)
        gdec.append(jnp.exp(ex[:, :LANES] + ex[:, LANES:]))

    scores = []
    for h in heads:
        s = jnp.where(xor == 0, lax.dot_general(q[h].astype(BF16), k[h].astype(BF16), nt,
                                                preferred_element_type=F32), 0.0)
        for lvl in range(n_levels):
            g_l = gdec[h][lvl * L:(lvl + 1) * L, :]
            right = ((row >> lvl) & 1).astype(F32)
            ql = (q[h] * g_l * right).astype(BF16)
            kl = (k[h] * g_l * (1.0 - right)).astype(BF16)
            s_l = lax.dot_general(ql, kl, nt, preferred_element_type=F32)
            s = s + jnp.where(xor < (2 << lvl), s_l, 0.0)
        scores.append(s.astype(BF16))

    for h in heads:
        iv = i_ref[:, cols[h]]
        g_full = gdec[h][n_levels * L:(n_levels + 1) * L, :]
        g_rest = gdec[h][(n_levels + 1) * L:(n_levels + 2) * L, :]
        state_t = state_ref[h]
        o = lax.dot_general((q[h] * g_full).astype(BF16), state_t.astype(BF16), nt,
                            preferred_element_type=F32)
        o = o + jnp.dot(scores[h], iv, preferred_element_type=F32)
        k_out = (k[h] * g_rest).astype(BF16)
        upd = jnp.dot(iv.astype(F32).T.astype(BF16), k_out, preferred_element_type=F32)
        state_ref[h] = state_t * g_full[L - 1:L, :] + upd
        ms = jnp.mean(o * o, axis=-1, keepdims=True)
        gate = g_ref[:, cols[h]].astype(F32)
        o = o * lax.rsqrt(ms + 1e-6) * gn_ref[...] * (gate * jax.nn.sigmoid(gate))
        o_ref[:, cols[h]] = o.astype(o_ref.dtype)


def _hgrn(aq, af, ai, ag, lb_logits, gn, cmat, layer_j, B, S):
    L = HGRN_L
    nb = S // L
    blk = pl.BlockSpec((L, A_WIDTH), lambda b, t: (b * nb + t, 0))
    return pl.pallas_call(
        functools.partial(_hgrn_kernel, layer_j=layer_j),
        grid=(B, nb),
        in_specs=[blk, blk, blk, blk, _full_spec(lb_logits.shape),
                  _full_spec(gn.shape), _full_spec(cmat.shape)],
        out_specs=blk,
        out_shape=jax.ShapeDtypeStruct((B * S, A_WIDTH), BF16),
        scratch_shapes=[pltpu.VMEM((A_HEADS, LANES, LANES), F32)],
        compiler_params=_cparams(("parallel", "arbitrary")),
        name="hgrn2",
    )(aq, af, ai, ag, lb_logits, gn, cmat)


def _attend(k_ref, vt_ref, qt_ref, streams, mask_fn):
    tk = k_ref.shape[0]
    kd = LANES
    tq = qt_ref.shape[1]
    def verdict(mask, word):
        return isinstance(mask, str) and mask == word

    chains = []
    for kc in range(0, tk, ATT_KC):
        for st in streams:
            for qc in range(0, tq, ATT_QC):
                n_keys, mask = ATT_KC, None
                if mask_fn is not None:
                    half = ATT_KC // 2
                    if verdict(mask_fn(kc + half, qc, half), "none"):
                        n_keys = half
                    mask = mask_fn(kc, qc, n_keys)
                    if verdict(mask, "none"):
                        continue
                    if verdict(mask, "all"):
                        mask = None
                chains.append((st, qc, kc, n_keys, mask))

    def scores(st, qc, kc, n_keys, mask):
        s = jnp.dot(k_ref[kc:kc + n_keys, st[4]:st[4] + kd], qt_ref[st[0]:st[0] + kd, qc:qc + ATT_QC],
                    preferred_element_type=F32)
        if mask is not None:
            s = jnp.where(mask, s, MASK_VALUE)
        return s

    def softmax(st, qc, s):
        m_ref = st[1]
        cols = slice(qc, qc + ATT_QC)
        m_prev = m_ref[:, cols]
        m_run, parts, maxes = m_prev, [], []
        for r0 in range(0, s.shape[0], ATT_SUB):
            s_g = s[r0:r0 + ATT_SUB, :]
            m_run = jnp.maximum(m_run, jnp.max(s_g, axis=0, keepdims=True))
            parts.append(jnp.exp2((s_g - m_run).astype(BF16)))
            maxes.append(m_run)
        m_new = m_run
        parts = [p_g if m_g is m_new else p_g * jnp.exp2(m_g - m_new).astype(BF16)
                 for p_g, m_g in zip(parts, maxes)]
        p = parts[0] if len(parts) == 1 else jnp.concatenate(parts, axis=0)
        alpha = jnp.exp2(m_prev - m_new)
        m_ref[:, cols] = m_new
        return p, alpha

    def weighted_values(st, qc, kc, p, alpha):
        acc_ref, v0 = st[2], st[3]
        cols = slice(qc, qc + ATT_QC)
        acc_ref[:, cols] = alpha * acc_ref[:, cols] + jnp.dot(
            vt_ref[v0:v0 + acc_ref.shape[0], kc:kc + p.shape[0]], p, preferred_element_type=F32)

    n = len(chains)
    ready = [scores(*c) for c in chains[:ATT_AHEAD]]
    pending = []
    for i, (st, qc, kc, _, _) in enumerate(chains):
        s = ready.pop(0)
        if i + ATT_AHEAD < n:
            ready.append(scores(*chains[i + ATT_AHEAD]))
        pending.append((st, qc, kc, *softmax(st, qc, s)))
        if len(pending) > ATT_BEHIND:
            weighted_values(*pending.pop(0))
    for item in pending:
        weighted_values(*item)


ATT_MULT = ATT_TQ // ATT_TK


def _tri_schedule(nq):
    qi = [i for i in range(nq) for _ in range((i + 1) * ATT_MULT)]
    ki = [j for i in range(nq) for j in range((i + 1) * ATT_MULT)]
    return jnp.asarray(qi, jnp.int32), jnp.asarray(ki, jnp.int32)


def _attend_tile(qi, ki, run, mask_for, finalize):
    rel = ki - qi * ATT_MULT

    @pl.when(rel < 0)
    def _():
        run(None)

    for r in range(ATT_MULT):
        @pl.when(rel == r)
        def _(r=r):
            run(mask_for(r * ATT_TK))
            if r == ATT_MULT - 1:
                finalize()


def _init_stats(m_ref, acc_ref):
    m_ref[...] = jnp.full_like(m_ref, MASK_VALUE)
    acc_ref[...] = jnp.zeros_like(acc_ref)


def _diff_kernel(qi_ref, ki_ref, q_ref, k_ref, v_ref, lam_ref, gn_ref, o_ref,
                 m1, a1, m2, a2, *, lam_init):
    step = pl.program_id(2)
    qi = qi_ref[step]
    ki = ki_ref[step]

    @pl.when(ki == 0)
    def _():
        _init_stats(m1, a1)
        _init_stats(m2, a2)

    def run(mask_fn):
        _attend(k_ref, v_ref, q_ref, [(0, m1, a1, 0, 0), (LANES, m2, a2, 0, 0)], mask_fn)

    def mask_for(key_offset):
        def chunk_mask(k0, q0, klen):
            k0 = k0 + key_offset
            if k0 + klen <= q0 + DIFF_CHUNK:
                return "all"
            if k0 >= q0 + ATT_QC:
                return "none"
            sh = DIFF_CHUNK.bit_length() - 1
            kk = (lax.broadcasted_iota(jnp.int32, (klen, ATT_QC), 0) + k0) >> sh
            qq = (lax.broadcasted_iota(jnp.int32, (klen, ATT_QC), 1) + q0) >> sh
            return kk <= qq
        return chunk_mask

    def finalize():
        lv = lam_ref[...]
        lam = (jnp.exp(jnp.sum(lv[0:1] * lv[1:2], axis=-1, keepdims=True))
               - jnp.exp(jnp.sum(lv[2:3] * lv[3:4], axis=-1, keepdims=True)) + lam_init)
        o = (a1[:LANES, :] / a1[LANES:LANES + 1, :]
             - lam * (a2[:LANES, :] / a2[LANES:LANES + 1, :])).T
        ms = jnp.mean(o * o, axis=-1, keepdims=True)
        o = o * lax.rsqrt(ms + 1e-6) * gn_ref[...] * (1.0 - lam_init)
        o_ref[...] = o.astype(o_ref.dtype)

    _attend_tile(qi, ki, run, mask_for, finalize)


def _diff_attention(bq, bk, bv, lamv, gn, lam_init, B, S):
    tq, tk = ATT_TQ, ATT_TK
    nq, nk = S // tq, S // tk
    qi, ki = _tri_schedule(nq)
    stat = pltpu.VMEM((1, tq), F32)
    acc = pltpu.VMEM((VT_ROWS, tq), F32)
    grid_spec = pltpu.PrefetchScalarGridSpec(
        num_scalar_prefetch=2,
        grid=(B, B_HEADS, int(qi.shape[0])),
        in_specs=[
            pl.BlockSpec((None, 2 * LANES, tq), lambda b, h, s, qi, ki: (b, h, qi[s])),
            pl.BlockSpec((tk, LANES), lambda b, h, s, qi, ki: (b * nk + ki[s], h)),
            pl.BlockSpec((None, VT_ROWS, tk), lambda b, h, s, qi, ki: (b, h, ki[s])),
            pl.BlockSpec(lamv.shape, lambda b, h, s, qi, ki: (0, 0)),
            pl.BlockSpec(gn.shape, lambda b, h, s, qi, ki: (0, 0)),
        ],
        out_specs=pl.BlockSpec((tq, LANES), lambda b, h, s, qi, ki: (b * nq + qi[s], h)),
        scratch_shapes=[stat, acc, stat, acc],
    )
    return pl.pallas_call(
        functools.partial(_diff_kernel, lam_init=lam_init),
        grid_spec=grid_spec,
        out_shape=jax.ShapeDtypeStruct((B * S, B_WIDTH), BF16),
        compiler_params=_cparams(("parallel", "parallel", "arbitrary")),
        name="diff_attn",
    )(qi, ki, bq, bk, bv, lamv, gn)


FOX_AUG = 16
FOX_VROWS = C_HEAD_DIM + 16
FOX_PAD = LANES - C_HEAD_DIM - FOX_AUG


def _fox_slots(h):
    base = (h // 2) * 2 * LANES
    if h % 2 == 0:
        return base, base + C_HEAD_DIM, base + C_HEAD_DIM + FOX_AUG
    return base + 2 * LANES - C_HEAD_DIM, base + LANES, base + LANES + FOX_AUG


def _fox_selectors():
    n_pairs = C_HEADS // 2
    selk = np.zeros((3 * LANES, n_pairs * LANES), np.float32)
    onek = np.zeros((1, n_pairs * LANES), np.float32)
    selq = np.zeros((C_HEADS * FOX_AUG, 3 * LANES), np.float32)
    oneq = np.zeros((C_HEADS * FOX_AUG, LANES), np.float32)
    for h in range(C_HEADS):
        p, odd = divmod(h, 2)
        lane0 = p * LANES + (0 if odd else C_HEAD_DIM)
        for piece in range(3):
            onek[0, lane0 + piece] = 1.0
            selk[piece * LANES + h, lane0 + 3 + piece] = -1.0
            selq[h * FOX_AUG + piece, piece * LANES + h] = 1.0
            oneq[h * FOX_AUG + 3 + piece, :] = 1.0
    return selk, onek, selq, oneq


def _odd_proj_kernel(x_ref, w_ref, bf_ref, qg_ref, kg_ref, gmat_ref, tri_ref,
                     selk_ref, onek_ref, selq_ref, oneq_ref,
                     qt_o, k_o, vt_o, g_o, carry_ref):
    @pl.when(pl.program_id(1) == 0)
    def _():
        carry_ref[...] = jnp.zeros_like(carry_ref)

    xb = x_ref[...].astype(BF16)
    tm = xb.shape[0]
    ch = MXU_DIM
    hpc = ch // C_HEAD_DIM
    gmat = gmat_ref[...]

    def headnorm(acc, g):
        ms = jnp.dot((acc * acc).astype(BF16), gmat, preferred_element_type=F32)
        return acc * lax.rsqrt(ms + 1e-6) * g

    zero_pad = jnp.zeros((FOX_PAD, tm), BF16)
    ones = jnp.ones((FOX_VROWS - C_HEAD_DIM, tm), BF16)
    for j in range(D_MODEL // ch):
        acc = jnp.dot(xb, w_ref[:, j * ch:(j + 1) * ch], preferred_element_type=F32)
        qt = (headnorm(acc, qg_ref[...]) * (C_HEAD_DIM ** -0.5 * LOG2E)).T.astype(BF16)
        acc = jnp.dot(xb, w_ref[:, 2 * D_MODEL + j * ch:2 * D_MODEL + (j + 1) * ch],
                      preferred_element_type=F32)
        vt = acc.T.astype(BF16)
        for i in range(hpc):
            h = j * hpc + i
            src = slice(i * C_HEAD_DIM, (i + 1) * C_HEAD_DIM)
            val0, _, zero0 = _fox_slots(h)
            qt_o[val0:val0 + C_HEAD_DIM, :] = qt[src, :]
            qt_o[zero0:zero0 + FOX_PAD, :] = zero_pad
            vt_o[h * FOX_VROWS:h * FOX_VROWS + C_HEAD_DIM, :] = vt[src, :]
            vt_o[h * FOX_VROWS + C_HEAD_DIM:(h + 1) * FOX_VROWS, :] = ones
        acc = jnp.dot(xb, w_ref[:, D_MODEL + j * ch:D_MODEL + (j + 1) * ch], preferred_element_type=F32)
        kn = headnorm(acc, kg_ref[...]).astype(BF16)
        for i in range(hpc):
            val0 = _fox_slots(j * hpc + i)[0]
            k_o[:, val0:val0 + C_HEAD_DIM] = kn[:, i * C_HEAD_DIM:(i + 1) * C_HEAD_DIM]
        acc = jnp.dot(xb, w_ref[:, 3 * D_MODEL + j * ch:3 * D_MODEL + (j + 1) * ch],
                      preferred_element_type=F32)
        g_o[:, j * ch:(j + 1) * ch] = acc.astype(g_o.dtype)

    fl = jnp.dot(xb, w_ref[:, 4 * D_MODEL:4 * D_MODEL + LANES], preferred_element_type=F32) + bf_ref[...]
    logf = jnp.minimum(fl, 0.0) - jnp.log(1.0 + jnp.exp(-jnp.abs(fl)))
    c = _sum3(jnp.dot(tri_ref[...], _split3(logf), preferred_element_type=F32)) + carry_ref[...]
    carry_ref[...] = c[tm - 1:tm, :]
    c3 = _split3(c * LOG2E)
    aug_k = (jnp.dot(c3, selk_ref[...], preferred_element_type=F32) + onek_ref[...]).astype(BF16)
    half = LANES // 2
    for p in range(C_HEADS // 2):
        k_o[:, 2 * p * LANES + half:2 * p * LANES + LANES] = aug_k[:, p * LANES + half:(p + 1) * LANES]
        k_o[:, (2 * p + 1) * LANES:(2 * p + 1) * LANES + half] = aug_k[:, p * LANES:p * LANES + half]
    aug_q = lax.dot_general(selq_ref[...], c3, (((1,), (1,)), ((), ())), preferred_element_type=F32)
    aug_q = (aug_q + jnp.concatenate([oneq_ref[...]] * (tm // LANES), axis=1)).astype(BF16)
    for h in range(C_HEADS):
        aug0 = _fox_slots(h)[1]
        qt_o[aug0:aug0 + FOX_AUG, :] = aug_q[h * FOX_AUG:(h + 1) * FOX_AUG, :]


def _odd_proj(x2, w_bf, bf_pad, qg, kg, gmat, tri, B, S):
    T = B * S
    tm = PROJ_TM
    nt = S // tm
    tok = lambda n: pl.BlockSpec((tm, n), lambda b, t: (b * nt + t, 0))
    tspec = lambda rows: pl.BlockSpec((None, rows, tm), lambda b, t: (b, 0, t))
    selk, onek, selq, oneq = _fox_selectors()
    consts = [jnp.asarray(selk, BF16), jnp.asarray(onek, F32), jnp.asarray(selq, BF16),
              jnp.asarray(oneq, F32)]
    k_width = C_HEADS * LANES
    return pl.pallas_call(
        _odd_proj_kernel,
        grid=(B, nt),
        in_specs=[tok(D_MODEL), _full_spec(w_bf.shape), _full_spec(bf_pad.shape), _full_spec(qg.shape),
                  _full_spec(kg.shape), _full_spec(gmat.shape), _full_spec(tri.shape)]
                 + [_full_spec(c.shape) for c in consts],
        out_specs=[tspec(k_width), tok(k_width), tspec(C_HEADS * FOX_VROWS), tok(D_MODEL)],
        out_shape=[jax.ShapeDtypeStruct((B, k_width, S), BF16),
                   jax.ShapeDtypeStruct((T, k_width), BF16),
                   jax.ShapeDtypeStruct((B, C_HEADS * FOX_VROWS, S), BF16),
                   jax.ShapeDtypeStruct((T, D_MODEL), BF16)],
        scratch_shapes=[pltpu.VMEM((1, LANES), F32)],
        compiler_params=_cparams(("parallel", "arbitrary")),
        name="odd_proj",
    )(x2, w_bf, bf_pad, qg, kg, gmat, tri, *consts)


def _fox_kernel(qi_ref, ki_ref, q_ref, k_ref, v_ref, o_ref, ma, aa, mb, ab):
    step = pl.program_id(2)
    qi = qi_ref[step]
    ki = ki_ref[step]

    @pl.when(ki == 0)
    def _():
        _init_stats(ma, aa)
        _init_stats(mb, ab)

    def run(mask_fn):
        _attend(k_ref, v_ref, q_ref, [(0, ma, aa, 0, 0), (LANES, mb, ab, FOX_VROWS, LANES)], mask_fn)

    def mask_for(key_offset):
        def chunk_mask(k0, q0, klen):
            k0 = k0 + key_offset
            if k0 + klen <= q0 + 1:
                return "all"
            if k0 >= q0 + ATT_QC:
                return "none"
            kk = lax.broadcasted_iota(jnp.int32, (klen, ATT_QC), 0) + k0
            qq = lax.broadcasted_iota(jnp.int32, (klen, ATT_QC), 1) + q0
            return kk <= qq
        return chunk_mask

    def finalize():
        d = C_HEAD_DIM
        o = jnp.concatenate([aa[:d, :] / aa[d:d + 1, :], ab[:d, :] / ab[d:d + 1, :]], axis=0)
        o_ref[...] = o.T.astype(o_ref.dtype)

    _attend_tile(qi, ki, run, mask_for, finalize)


def _fox_attention(qt, k, vt, B, S):
    tq, tk = ATT_TQ, ATT_TK
    nq, nk = S // tq, S // tk
    qi, ki = _tri_schedule(nq)
    n_pairs = C_HEADS // 2
    stat = pltpu.VMEM((1, tq), F32)
    acc = pltpu.VMEM((FOX_VROWS, tq), F32)
    grid_spec = pltpu.PrefetchScalarGridSpec(
        num_scalar_prefetch=2,
        grid=(B, n_pairs, int(qi.shape[0])),
        in_specs=[
            pl.BlockSpec((None, 2 * LANES, tq), lambda b, h, s, qi, ki: (b, h, qi[s])),
            pl.BlockSpec((tk, 2 * LANES), lambda b, h, s, qi, ki: (b * nk + ki[s], h)),
            pl.BlockSpec((None, 2 * FOX_VROWS, tk), lambda b, h, s, qi, ki: (b, h, ki[s])),
        ],
        out_specs=pl.BlockSpec((tq, LANES), lambda b, h, s, qi, ki: (b * nq + qi[s], h)),
        scratch_shapes=[stat, acc, stat, acc],
    )
    return pl.pallas_call(
        _fox_kernel,
        grid_spec=grid_spec,
        out_shape=jax.ShapeDtypeStruct((B * S, D_MODEL), BF16),
        compiler_params=_cparams(("parallel", "parallel", "arbitrary")),
        name="fox_attn",
    )(qi, ki, qt, k, vt)


def _even_mix(oa_ref, ob_ref, w_ref):
    h = jnp.dot(oa_ref[...], w_ref[:A_WIDTH, :], preferred_element_type=F32)
    return h + jnp.dot(ob_ref[...], w_ref[A_WIDTH:, :], preferred_element_type=F32)


def _odd_mix(o_ref, gate_ref, w_ref):
    o = o_ref[...].astype(F32) * jax.nn.sigmoid(gate_ref[...].astype(F32))
    return jnp.dot(o.astype(BF16), w_ref[...], preferred_element_type=F32)


def _tail_kernel(a0_ref, a1_ref, x_ref, wo_ref, g1_ref, b1_ref, w1_ref, w2_ref, g2_ref, b2_ref,
                 y_ref, act_ref, *, mix):
    x = _layer_norm_rows(ALPHA * x_ref[...] + mix(a0_ref, a1_ref, wo_ref), g1_ref[...], b1_ref[...])
    xb = x.astype(BF16)
    ch = MXU_DIM
    for j in range(D_FF // ch):
        gate = jnp.dot(xb, w1_ref[:, j * ch:(j + 1) * ch], preferred_element_type=F32)
        up = jnp.dot(xb, w1_ref[:, D_FF + j * ch:D_FF + (j + 1) * ch], preferred_element_type=F32)
        act_ref[:, j * ch:(j + 1) * ch] = (gate * jax.nn.sigmoid(gate) * up).astype(BF16)
    h = jnp.dot(act_ref[...], w2_ref[...], preferred_element_type=F32)
    y_ref[...] = _layer_norm_rows(ALPHA * x + h, g2_ref[...], b2_ref[...])


def _layer_tail(mix, acts, x2, wo_bf, g1, b1, w1_bf, w2_bf, g2, b2, name):
    T = x2.shape[0]
    tm = FFN_TM
    tok = lambda n: pl.BlockSpec((tm, n), lambda i: (i, 0))
    consts = [wo_bf, g1, b1, w1_bf, w2_bf, g2, b2]
    return pl.pallas_call(
        functools.partial(_tail_kernel, mix=mix),
        grid=(T // tm,),
        in_specs=[tok(a.shape[1]) for a in acts] + [tok(D_MODEL)] + [_full_spec(c.shape) for c in consts],
        out_specs=tok(D_MODEL),
        out_shape=jax.ShapeDtypeStruct((T, D_MODEL), F32),
        scratch_shapes=[pltpu.VMEM((tm, D_FF), BF16)],
        compiler_params=_cparams(("parallel",)),
        name=name,
    )(*acts, x2, *consts)


def kernel(x, even_w_in, even_w_out, hgrn_lb_logits, diff_lq1, diff_lk1, diff_lq2, diff_lk2,
           hgrn_norm_g, diff_norm_g, fox_w_in, fox_w_out, fox_b_f, fox_qnorm_g, fox_knorm_g,
           ffn_w1, ffn_w2, ln1_g, ln1_b, ln2_g, ln2_b):
    B, S, D = x.shape
    assert D == D_MODEL and S % ATT_TQ == 0 and S % PROJ_TM == 0 and S % HGRN_L == 0
    T = B * S
    x2 = x.reshape(T, D).astype(F32)

    cmat = jnp.asarray(_hgrn_cumsum_matrix(HGRN_L), BF16)
    tri = jnp.asarray(np.tril(np.ones((PROJ_TM, PROJ_TM), np.float32)), BF16)
    head_of = np.arange(MXU_DIM) // C_HEAD_DIM
    gmat = jnp.asarray((head_of[:, None] == head_of[None, :]).astype(np.float32) / C_HEAD_DIM, BF16)
    row = lambda v: v.astype(F32).reshape(1, -1)

    for l in range(DEPTH):
        j = l // 2
        if l % 2 == 0:
            aq, af, ai, ag, bq, bk, bv = _even_proj(x2, even_w_in[j].astype(BF16), B, S)
            o_a = _hgrn(aq, af, ai, ag, hgrn_lb_logits.astype(F32), row(hgrn_norm_g[j]), cmat, j, B, S)
            lamv = jnp.zeros((8, B_HEAD_DIM), F32).at[0:4].set(
                jnp.stack([diff_lq1[j], diff_lk1[j], diff_lq2[j], diff_lk2[j]]).astype(F32))
            lam_init = 0.8 - 0.6 * math.exp(-0.3 * l)
            o_b = _diff_attention(bq, bk, bv, lamv, row(diff_norm_g[j]), lam_init, B, S)
            mix, acts, w_out, name = _even_mix, [o_a, o_b], even_w_out[j], "even_tail"
        else:
            w_pad = jnp.pad(fox_w_in[j], ((0, 0), (0, LANES - C_HEADS))).astype(BF16)
            bf_pad = jnp.pad(fox_b_f[j].astype(F32), (0, LANES - C_HEADS)).reshape(1, LANES)
            tile4 = lambda v: jnp.tile(v.astype(F32), MXU_DIM // C_HEAD_DIM).reshape(1, MXU_DIM)
            qt, k, vt, g = _odd_proj(x2, w_pad, bf_pad, tile4(fox_qnorm_g[j]),
                                     tile4(fox_knorm_g[j]), gmat, tri, B, S)
            o = _fox_attention(qt, k, vt, B, S)
            mix, acts, w_out, name = _odd_mix, [o, g], fox_w_out[j], "odd_tail"
        x2 = _layer_tail(mix, acts, x2, w_out.astype(BF16), row(ln1_g[l]), row(ln1_b[l]),
                         ffn_w1[l].astype(BF16), ffn_w2[l].astype(BF16), row(ln2_g[l]), row(ln2_b[l]), name)
    return x2.reshape(B, S, D).astype(x.dtype)
```

```python
import functools
import math

import numpy as np
import jax
import jax.numpy as jnp
from jax import lax
from jax.experimental import pallas as pl
from jax.experimental.pallas import tpu as pltpu

F32 = jnp.float32
BF16 = jnp.bfloat16

D_MODEL = 1024
DEPTH = 4
A_WIDTH = 512
A_HEADS = 4
A_DK = 128
B_WIDTH = 512
B_HEADS = 4
B_HEAD_DIM = 64
DIFF_CHUNK = 64
C_HEADS = 16
C_HEAD_DIM = 64
D_FF = 2816
ALPHA = (2 * DEPTH) ** 0.25
MASK_VALUE = -1e30

LANES = 128
MXU_DIM = 256

PROJ_TM = 512
FFN_TM = 512
ATT_TQ = 4096
ATT_TK = 2048
ATT_QC = 256
ATT_KC = 512
ATT_SUB = 64
ATT_AHEAD = 2
ATT_BEHIND = 2
VT_ROWS = LANES + 16
HGRN_L = 256
LOG2E = math.log2(math.e)
VMEM_LIMIT = 56 * 1024 * 1024


def _cparams(sem):
    return pltpu.CompilerParams(dimension_semantics=sem, vmem_limit_bytes=VMEM_LIMIT)


def _full_spec(shape):
    nd = len(shape)
    return pl.BlockSpec(shape, lambda *_: (0,) * nd)


def _split3(x):
    h = x.astype(BF16)
    r = x - h.astype(F32)
    m = r.astype(BF16)
    l = (r - m.astype(F32)).astype(BF16)
    return jnp.concatenate([h, m, l], axis=1)


def _sum3(y):
    return y[:, :LANES] + y[:, LANES:2 * LANES] + y[:, 2 * LANES:]


def _layer_norm_rows(y, g, b):
    mu = jnp.mean(y, axis=-1, keepdims=True)
    d = y - mu
    var = jnp.mean(d * d, axis=-1, keepdims=True)
    return d * lax.rsqrt(var + 1e-5) * g + b


def _even_proj_kernel(x_ref, w_ref, aq, af, ai, ag, bq, bk, bv):
    xb = x_ref[...].astype(BF16)
    outs = (aq, af, ai, ag, bq, bk, bv)
    for n, o in enumerate(outs):
        acc = jnp.dot(xb, w_ref[:, n * 512:(n + 1) * 512], preferred_element_type=F32)
        if o is bq:
            qt = (acc * (B_HEAD_DIM ** -0.5 * LOG2E)).T.astype(o.dtype)
            zero = jnp.zeros((B_HEAD_DIM, qt.shape[1]), o.dtype)
            for h in range(B_HEADS):
                r = 2 * h * LANES
                o[r:r + B_HEAD_DIM, :] = qt[h * LANES:h * LANES + B_HEAD_DIM, :]
                o[r + B_HEAD_DIM:r + LANES, :] = zero
                o[r + LANES:r + LANES + B_HEAD_DIM, :] = zero
                o[r + LANES + B_HEAD_DIM:r + 2 * LANES, :] = qt[h * LANES + B_HEAD_DIM:(h + 1) * LANES, :]
        elif o is bv:
            vt = acc.T.astype(o.dtype)
            ones = jnp.ones((VT_ROWS - LANES, vt.shape[1]), o.dtype)
            for h in range(B_HEADS):
                o[h * VT_ROWS:h * VT_ROWS + LANES, :] = vt[h * LANES:(h + 1) * LANES, :]
                o[h * VT_ROWS + LANES:(h + 1) * VT_ROWS, :] = ones
        else:
            o[...] = acc.astype(o.dtype)


def _even_proj(x2, w_bf, B, S):
    T = x2.shape[0]
    tm = PROJ_TM
    nt = S // tm
    tok = lambda dt: jax.ShapeDtypeStruct((T, 512), dt)
    spec = pl.BlockSpec((tm, 512), lambda i: (i, 0))
    tspec = lambda rows: pl.BlockSpec((None, rows, tm), lambda i: (i // nt, 0, i % nt))
    return pl.pallas_call(
        _even_proj_kernel,
        grid=(T // tm,),
        in_specs=[pl.BlockSpec((tm, D_MODEL), lambda i: (i, 0)), _full_spec(w_bf.shape)],
        out_specs=[spec, spec, spec, spec, tspec(2 * B_WIDTH), spec, tspec(B_HEADS * VT_ROWS)],
        out_shape=[tok(BF16), tok(F32), tok(BF16), tok(BF16),
                   jax.ShapeDtypeStruct((B, 2 * B_WIDTH, S), BF16), tok(BF16),
                   jax.ShapeDtypeStruct((B, B_HEADS * VT_ROWS, S), BF16)],
        compiler_params=_cparams(("parallel",)),
        name="even_proj",
    )(x2, w_bf)


def _hgrn_cumsum_matrix(L):
    idx = np.arange(L)
    mats = []
    c = 1
    while c < L:
        start = (idx // c) * c
        end = start + c - 1
        right = ((idx // c) % 2) == 1
        u = idx[None, :]
        m_right = (u >= start[:, None]) & (u <= idx[:, None])
        m_left = (u > idx[:, None]) & (u <= end[:, None])
        mats.append(np.where(right[:, None], m_right, m_left))
        c *= 2
    u = idx[None, :]
    mats.append(u <= idx[:, None])
    mats.append(u > idx[:, None])
    return np.concatenate(mats, axis=0).astype(np.float32)


def _split2(x):
    h = x.astype(BF16)
    m = (x - h.astype(F32)).astype(BF16)
    return jnp.concatenate([h, m], axis=1)


def _hgrn_kernel(q_ref, f_ref, i_ref, g_ref, lbl_ref, gn_ref, cmat_ref, o_ref, state_ref, *, layer_j):
    L = HGRN_L
    n_levels = int(math.log2(L))
    nt = (((1,), (1,)), ((), ()))

    @pl.when(pl.program_id(1) == 0)
    def _():
        state_ref[...] = jnp.zeros_like(state_ref)

    lbl = lbl_ref[...]
    e = jnp.exp(lbl - jnp.max(lbl, axis=0, keepdims=True))
    soft = e / jnp.sum(e, axis=0, keepdims=True)
    lb_all = jnp.sum(soft[:layer_j + 1], axis=0, keepdims=True) - soft[0:1]

    row = lax.broadcasted_iota(jnp.int32, (L, 1), 0)
    xor = lax.broadcasted_iota(jnp.int32, (L, L), 0) ^ lax.broadcasted_iota(jnp.int32, (L, L), 1)
    cmat = cmat_ref[...]

    heads = range(A_HEADS)
    cols = [slice(h * LANES, (h + 1) * LANES) for h in heads]
    q, k, gdec = [], [], []
    for h in heads:
        lb = lb_all[:, cols[h]]
        f = lb + (1.0 - lb) * jax.nn.sigmoid(f_ref[:, cols[h]])
        k.append(1.0 - f)
        qr = q_ref[:, cols[h]].astype(F32)
        q.append(qr * jax.nn.sigmoid(qr))
        ex = jnp.dot(cmat, _split2(jnp.log(f)), preferred_element_type=F32)
        gdec.append(jnp.exp(ex[:, :LANES] + ex[:, LANES:]))

    scores = []
    for h in heads:
        s = jnp.where(xor == 0, lax.dot_general(q[h].astype(BF16), k[h].astype(BF16), nt,
                                                preferred_element_type=F32), 0.0)
        for lvl in range(n_levels):
            g_l = gdec[h][lvl * L:(lvl + 1) * L, :]
            right = ((row >> lvl) & 1).astype(F32)
            ql = (q[h] * g_l * right).astype(BF16)
            kl = (k[h] * g_l * (1.0 - right)).astype(BF16)
            s_l = lax.dot_general(ql, kl, nt, preferred_element_type=F32)
            s = s + jnp.where(xor < (2 << lvl), s_l, 0.0)
        scores.append(s.astype(BF16))

    for h in heads:
        iv = i_ref[:, cols[h]]
        g_full = gdec[h][n_levels * L:(n_levels + 1) * L, :]
        g_rest = gdec[h][(n_levels + 1) * L:(n_levels + 2) * L, :]
        state_t = state_ref[h]
        o = lax.dot_general((q[h] * g_full).astype(BF16), state_t.astype(BF16), nt,
                            preferred_element_type=F32)
        o = o + jnp.dot(scores[h], iv, preferred_element_type=F32)
        k_out = (k[h] * g_rest).astype(BF16)
        upd = jnp.dot(iv.astype(F32).T.astype(BF16), k_out, preferred_element_type=F32)
        state_ref[h] = state_t * g_full[L - 1:L, :] + upd
        ms = jnp.mean(o * o, axis=-1, keepdims=True)
        gate = g_ref[:, cols[h]].astype(F32)
        o = o * lax.rsqrt(ms + 1e-6) * gn_ref[...] * (gate * jax.nn.sigmoid(gate))
        o_ref[:, cols[h]] = o.astype(o_ref.dtype)


def _hgrn(aq, af, ai, ag, lb_logits, gn, cmat, layer_j, B, S):
    L = HGRN_L
    nb = S // L
    blk = pl.BlockSpec((L, A_WIDTH), lambda b, t: (b * nb + t, 0))
    return pl.pallas_call(
        functools.partial(_hgrn_kernel, layer_j=layer_j),
        grid=(B, nb),
        in_specs=[blk, blk, blk, blk, _full_spec(lb_logits.shape),
                  _full_spec(gn.shape), _full_spec(cmat.shape)],
        out_specs=blk,
        out_shape=jax.ShapeDtypeStruct((B * S, A_WIDTH), BF16),
        scratch_shapes=[pltpu.VMEM((A_HEADS, LANES, LANES), F32)],
        compiler_params=_cparams(("parallel", "arbitrary")),
        name="hgrn2",
    )(aq, af, ai, ag, lb_logits, gn, cmat)


def _attend(k_ref, vt_ref, qt_ref, streams, mask_fn):
    tk = k_ref.shape[0]
    kd = LANES
    tq = qt_ref.shape[1]
    def verdict(mask, word):
        return isinstance(mask, str) and mask == word

    chains = []
    for kc in range(0, tk, ATT_KC):
        for st in streams:
            for qc in range(0, tq, ATT_QC):
                n_keys, mask = ATT_KC, None
                if mask_fn is not None:
                    half = ATT_KC // 2
                    if verdict(mask_fn(kc + half, qc, half), "none"):
                        n_keys = half
                    mask = mask_fn(kc, qc, n_keys)
                    if verdict(mask, "none"):
                        continue
                    if verdict(mask, "all"):
                        mask = None
                chains.append((st, qc, kc, n_keys, mask))

    def scores(st, qc, kc, n_keys, mask):
        s = jnp.dot(k_ref[kc:kc + n_keys, st[4]:st[4] + kd], qt_ref[st[0]:st[0] + kd, qc:qc + ATT_QC],
                    preferred_element_type=F32).astype(BF16)
        if mask is not None:
            s = jnp.where(mask, s, jnp.asarray(MASK_VALUE, BF16))
        return s

    def softmax(st, qc, s):
        m_ref = st[1]
        cols = slice(qc, qc + ATT_QC)
        m_prev = m_ref[:, cols]
        m_run, parts, maxes = m_prev, [], []
        for r0 in range(0, s.shape[0], ATT_SUB):
            s_g = s[r0:r0 + ATT_SUB, :]
            m_run = jnp.maximum(m_run, jnp.max(s_g, axis=0, keepdims=True).astype(F32))
            parts.append(jnp.exp2(s_g - m_run.astype(BF16)))
            maxes.append(m_run)
        m_new = m_run
        parts = [p_g if m_g is m_new else p_g * jnp.exp2(m_g - m_new).astype(BF16)
                 for p_g, m_g in zip(parts, maxes)]
        p = parts[0] if len(parts) == 1 else jnp.concatenate(parts, axis=0)
        alpha = jnp.exp2(m_prev - m_new)
        m_ref[:, cols] = m_new
        return p, alpha

    def weighted_values(st, qc, kc, p, alpha):
        acc_ref, v0 = st[2], st[3]
        cols = slice(qc, qc + ATT_QC)
        acc_ref[:, cols] = alpha * acc_ref[:, cols] + jnp.dot(
            vt_ref[v0:v0 + acc_ref.shape[0], kc:kc + p.shape[0]], p, preferred_element_type=F32)

    n = len(chains)
    ready = [scores(*c) for c in chains[:ATT_AHEAD]]
    pending = []
    for i, (st, qc, kc, _, _) in enumerate(chains):
        s = ready.pop(0)
        if i + ATT_AHEAD < n:
            ready.append(scores(*chains[i + ATT_AHEAD]))
        pending.append((st, qc, kc, *softmax(st, qc, s)))
        if len(pending) > ATT_BEHIND:
            weighted_values(*pending.pop(0))
    for item in pending:
        weighted_values(*item)


ATT_MULT = ATT_TQ // ATT_TK


def _tri_schedule(nq):
    qi = [i for i in range(nq) for _ in range((i + 1) * ATT_MULT)]
    ki = [j for i in range(nq) for j in range((i + 1) * ATT_MULT)]
    return jnp.asarray(qi, jnp.int32), jnp.asarray(ki, jnp.int32)


def _attend_tile(qi, ki, run, mask_for, finalize):
    rel = ki - qi * ATT_MULT

    @pl.when(rel < 0)
    def _():
        run(None)

    for r in range(ATT_MULT):
        @pl.when(rel == r)
        def _(r=r):
            run(mask_for(r * ATT_TK))
            if r == ATT_MULT - 1:
                finalize()


def _init_stats(m_ref, acc_ref):
    m_ref[...] = jnp.full_like(m_ref, MASK_VALUE)
    acc_ref[...] = jnp.zeros_like(acc_ref)


def _diff_kernel(qi_ref, ki_ref, q_ref, k_ref, v_ref, lam_ref, gn_ref, o_ref,
                 m1, a1, m2, a2, *, lam_init):
    step = pl.program_id(2)
    qi = qi_ref[step]
    ki = ki_ref[step]

    @pl.when(ki == 0)
    def _():
        _init_stats(m1, a1)
        _init_stats(m2, a2)

    def run(mask_fn):
        _attend(k_ref, v_ref, q_ref, [(0, m1, a1, 0, 0), (LANES, m2, a2, 0, 0)], mask_fn)

    def mask_for(key_offset):
        def chunk_mask(k0, q0, klen):
            k0 = k0 + key_offset
            if k0 + klen <= q0 + DIFF_CHUNK:
                return "all"
            if k0 >= q0 + ATT_QC:
                return "none"
            sh = DIFF_CHUNK.bit_length() - 1
            kk = (lax.broadcasted_iota(jnp.int32, (klen, ATT_QC), 0) + k0) >> sh
            qq = (lax.broadcasted_iota(jnp.int32, (klen, ATT_QC), 1) + q0) >> sh
            return kk <= qq
        return chunk_mask

    def finalize():
        lv = lam_ref[...]
        lam = (jnp.exp(jnp.sum(lv[0:1] * lv[1:2], axis=-1, keepdims=True))
               - jnp.exp(jnp.sum(lv[2:3] * lv[3:4], axis=-1, keepdims=True)) + lam_init)
        o = (a1[:LANES, :] / a1[LANES:LANES + 1, :]
             - lam * (a2[:LANES, :] / a2[LANES:LANES + 1, :])).T
        ms = jnp.mean(o * o, axis=-1, keepdims=True)
        o = o * lax.rsqrt(ms + 1e-6) * gn_ref[...] * (1.0 - lam_init)
        o_ref[...] = o.astype(o_ref.dtype)

    _attend_tile(qi, ki, run, mask_for, finalize)


def _diff_attention(bq, bk, bv, lamv, gn, lam_init, B, S):
    tq, tk = ATT_TQ, ATT_TK
    nq, nk = S // tq, S // tk
    qi, ki = _tri_schedule(nq)
    stat = pltpu.VMEM((1, tq), F32)
    acc = pltpu.VMEM((VT_ROWS, tq), F32)
    grid_spec = pltpu.PrefetchScalarGridSpec(
        num_scalar_prefetch=2,
        grid=(B, B_HEADS, int(qi.shape[0])),
        in_specs=[
            pl.BlockSpec((None, 2 * LANES, tq), lambda b, h, s, qi, ki: (b, h, qi[s])),
            pl.BlockSpec((tk, LANES), lambda b, h, s, qi, ki: (b * nk + ki[s], h)),
            pl.BlockSpec((None, VT_ROWS, tk), lambda b, h, s, qi, ki: (b, h, ki[s])),
            pl.BlockSpec(lamv.shape, lambda b, h, s, qi, ki: (0, 0)),
            pl.BlockSpec(gn.shape, lambda b, h, s, qi, ki: (0, 0)),
        ],
        out_specs=pl.BlockSpec((tq, LANES), lambda b, h, s, qi, ki: (b * nq + qi[s], h)),
        scratch_shapes=[stat, acc, stat, acc],
    )
    return pl.pallas_call(
        functools.partial(_diff_kernel, lam_init=lam_init),
        grid_spec=grid_spec,
        out_shape=jax.ShapeDtypeStruct((B * S, B_WIDTH), BF16),
        compiler_params=_cparams(("parallel", "parallel", "arbitrary")),
        name="diff_attn",
    )(qi, ki, bq, bk, bv, lamv, gn)


FOX_AUG = 16
FOX_VROWS = C_HEAD_DIM + 16
FOX_PAD = LANES - C_HEAD_DIM - FOX_AUG


def _fox_slots(h):
    base = (h // 2) * 2 * LANES
    if h % 2 == 0:
        return base, base + C_HEAD_DIM, base + C_HEAD_DIM + FOX_AUG
    return base + 2 * LANES - C_HEAD_DIM, base + LANES, base + LANES + FOX_AUG


def _fox_selectors():
    n_pairs = C_HEADS // 2
    selk = np.zeros((3 * LANES, n_pairs * LANES), np.float32)
    onek = np.zeros((1, n_pairs * LANES), np.float32)
    selq = np.zeros((C_HEADS * FOX_AUG, 3 * LANES), np.float32)
    oneq = np.zeros((C_HEADS * FOX_AUG, LANES), np.float32)
    for h in range(C_HEADS):
        p, odd = divmod(h, 2)
        lane0 = p * LANES + (0 if odd else C_HEAD_DIM)
        for piece in range(3):
            onek[0, lane0 + piece] = 1.0
            selk[piece * LANES + h, lane0 + 3 + piece] = -1.0
            selq[h * FOX_AUG + piece, piece * LANES + h] = 1.0
            oneq[h * FOX_AUG + 3 + piece, :] = 1.0
    return selk, onek, selq, oneq


def _odd_proj_kernel(x_ref, w_ref, bf_ref, qg_ref, kg_ref, gmat_ref, tri_ref,
                     selk_ref, onek_ref, selq_ref, oneq_ref,
                     qt_o, k_o, vt_o, g_o, carry_ref):
    @pl.when(pl.program_id(1) == 0)
    def _():
        carry_ref[...] = jnp.zeros_like(carry_ref)

    xb = x_ref[...].astype(BF16)
    tm = xb.shape[0]
    ch = MXU_DIM
    hpc = ch // C_HEAD_DIM
    gmat = gmat_ref[...]

    def headnorm(acc, g):
        ms = jnp.dot((acc * acc).astype(BF16), gmat, preferred_element_type=F32)
        return acc * lax.rsqrt(ms + 1e-6) * g

    zero_pad = jnp.zeros((FOX_PAD, tm), BF16)
    ones = jnp.ones((FOX_VROWS - C_HEAD_DIM, tm), BF16)
    for j in range(D_MODEL // ch):
        acc = jnp.dot(xb, w_ref[:, j * ch:(j + 1) * ch], preferred_element_type=F32)
        qt = (headnorm(acc, qg_ref[...]) * (C_HEAD_DIM ** -0.5 * LOG2E)).T.astype(BF16)
        acc = jnp.dot(xb, w_ref[:, 2 * D_MODEL + j * ch:2 * D_MODEL + (j + 1) * ch],
                      preferred_element_type=F32)
        vt = acc.T.astype(BF16)
        for i in range(hpc):
            h = j * hpc + i
            src = slice(i * C_HEAD_DIM, (i + 1) * C_HEAD_DIM)
            val0, _, zero0 = _fox_slots(h)
            qt_o[val0:val0 + C_HEAD_DIM, :] = qt[src, :]
            qt_o[zero0:zero0 + FOX_PAD, :] = zero_pad
            vt_o[h * FOX_VROWS:h * FOX_VROWS + C_HEAD_DIM, :] = vt[src, :]
            vt_o[h * FOX_VROWS + C_HEAD_DIM:(h + 1) * FOX_VROWS, :] = ones
        acc = jnp.dot(xb, w_ref[:, D_MODEL + j * ch:D_MODEL + (j + 1) * ch], preferred_element_type=F32)
        kn = headnorm(acc, kg_ref[...]).astype(BF16)
        for i in range(hpc):
            val0 = _fox_slots(j * hpc + i)[0]
            k_o[:, val0:val0 + C_HEAD_DIM] = kn[:, i * C_HEAD_DIM:(i + 1) * C_HEAD_DIM]
        acc = jnp.dot(xb, w_ref[:, 3 * D_MODEL + j * ch:3 * D_MODEL + (j + 1) * ch],
                      preferred_element_type=F32)
        g_o[:, j * ch:(j + 1) * ch] = acc.astype(g_o.dtype)

    fl = jnp.dot(xb, w_ref[:, 4 * D_MODEL:4 * D_MODEL + LANES], preferred_element_type=F32) + bf_ref[...]
    logf = jnp.minimum(fl, 0.0) - jnp.log(1.0 + jnp.exp(-jnp.abs(fl)))
    c = _sum3(jnp.dot(tri_ref[...], _split3(logf), preferred_element_type=F32)) + carry_ref[...]
    carry_ref[...] = c[tm - 1:tm, :]
    c3 = _split3(c * LOG2E)
    aug_k = (jnp.dot(c3, selk_ref[...], preferred_element_type=F32) + onek_ref[...]).astype(BF16)
    half = LANES // 2
    for p in range(C_HEADS // 2):
        k_o[:, 2 * p * LANES + half:2 * p * LANES + LANES] = aug_k[:, p * LANES + half:(p + 1) * LANES]
        k_o[:, (2 * p + 1) * LANES:(2 * p + 1) * LANES + half] = aug_k[:, p * LANES:p * LANES + half]
    aug_q = lax.dot_general(selq_ref[...], c3, (((1,), (1,)), ((), ())), preferred_element_type=F32)
    aug_q = (aug_q + jnp.concatenate([oneq_ref[...]] * (tm // LANES), axis=1)).astype(BF16)
    for h in range(C_HEADS):
        aug0 = _fox_slots(h)[1]
        qt_o[aug0:aug0 + FOX_AUG, :] = aug_q[h * FOX_AUG:(h + 1) * FOX_AUG, :]


def _odd_proj(x2, w_bf, bf_pad, qg, kg, gmat, tri, B, S):
    T = B * S
    tm = PROJ_TM
    nt = S // tm
    tok = lambda n: pl.BlockSpec((tm, n), lambda b, t: (b * nt + t, 0))
    tspec = lambda rows: pl.BlockSpec((None, rows, tm), lambda b, t: (b, 0, t))
    selk, onek, selq, oneq = _fox_selectors()
    consts = [jnp.asarray(selk, BF16), jnp.asarray(onek, F32), jnp.asarray(selq, BF16),
              jnp.asarray(oneq, F32)]
    k_width = C_HEADS * LANES
    return pl.pallas_call(
        _odd_proj_kernel,
        grid=(B, nt),
        in_specs=[tok(D_MODEL), _full_spec(w_bf.shape), _full_spec(bf_pad.shape), _full_spec(qg.shape),
                  _full_spec(kg.shape), _full_spec(gmat.shape), _full_spec(tri.shape)]
                 + [_full_spec(c.shape) for c in consts],
        out_specs=[tspec(k_width), tok(k_width), tspec(C_HEADS * FOX_VROWS), tok(D_MODEL)],
        out_shape=[jax.ShapeDtypeStruct((B, k_width, S), BF16),
                   jax.ShapeDtypeStruct((T, k_width), BF16),
                   jax.ShapeDtypeStruct((B, C_HEADS * FOX_VROWS, S), BF16),
                   jax.ShapeDtypeStruct((T, D_MODEL), BF16)],
        scratch_shapes=[pltpu.VMEM((1, LANES), F32)],
        compiler_params=_cparams(("parallel", "arbitrary")),
        name="odd_proj",
    )(x2, w_bf, bf_pad, qg, kg, gmat, tri, *consts)


def _fox_kernel(qi_ref, ki_ref, q_ref, k_ref, v_ref, o_ref, ma, aa, mb, ab):
    step = pl.program_id(2)
    qi = qi_ref[step]
    ki = ki_ref[step]

    @pl.when(ki == 0)
    def _():
        _init_stats(ma, aa)
        _init_stats(mb, ab)

    def run(mask_fn):
        _attend(k_ref, v_ref, q_ref, [(0, ma, aa, 0, 0), (LANES, mb, ab, FOX_VROWS, LANES)], mask_fn)

    def mask_for(key_offset):
        def chunk_mask(k0, q0, klen):
            k0 = k0 + key_offset
            if k0 + klen <= q0 + 1:
                return "all"
            if k0 >= q0 + ATT_QC:
                return "none"
            kk = lax.broadcasted_iota(jnp.int32, (klen, ATT_QC), 0) + k0
            qq = lax.broadcasted_iota(jnp.int32, (klen, ATT_QC), 1) + q0
            return kk <= qq
        return chunk_mask

    def finalize():
        d = C_HEAD_DIM
        o = jnp.concatenate([aa[:d, :] / aa[d:d + 1, :], ab[:d, :] / ab[d:d + 1, :]], axis=0)
        o_ref[...] = o.T.astype(o_ref.dtype)

    _attend_tile(qi, ki, run, mask_for, finalize)


def _fox_attention(qt, k, vt, B, S):
    tq, tk = ATT_TQ, ATT_TK
    nq, nk = S // tq, S // tk
    qi, ki = _tri_schedule(nq)
    n_pairs = C_HEADS // 2
    stat = pltpu.VMEM((1, tq), F32)
    acc = pltpu.VMEM((FOX_VROWS, tq), F32)
    grid_spec = pltpu.PrefetchScalarGridSpec(
        num_scalar_prefetch=2,
        grid=(B, n_pairs, int(qi.shape[0])),
        in_specs=[
            pl.BlockSpec((None, 2 * LANES, tq), lambda b, h, s, qi, ki: (b, h, qi[s])),
            pl.BlockSpec((tk, 2 * LANES), lambda b, h, s, qi, ki: (b * nk + ki[s], h)),
            pl.BlockSpec((None, 2 * FOX_VROWS, tk), lambda b, h, s, qi, ki: (b, h, ki[s])),
        ],
        out_specs=pl.BlockSpec((tq, LANES), lambda b, h, s, qi, ki: (b * nq + qi[s], h)),
        scratch_shapes=[stat, acc, stat, acc],
    )
    return pl.pallas_call(
        _fox_kernel,
        grid_spec=grid_spec,
        out_shape=jax.ShapeDtypeStruct((B * S, D_MODEL), BF16),
        compiler_params=_cparams(("parallel", "parallel", "arbitrary")),
        name="fox_attn",
    )(qi, ki, qt, k, vt)


def _even_mix(oa_ref, ob_ref, w_ref, rows):
    h = jnp.dot(oa_ref[rows, :], w_ref[:A_WIDTH, :], preferred_element_type=F32)
    return h + jnp.dot(ob_ref[rows, :], w_ref[A_WIDTH:, :], preferred_element_type=F32)


def _odd_mix(o_ref, gate_ref, w_ref, rows):
    o = o_ref[rows, :].astype(F32) * jax.nn.sigmoid(gate_ref[rows, :].astype(F32))
    return jnp.dot(o.astype(BF16), w_ref[...], preferred_element_type=F32)


def _tail_kernel(a0_ref, a1_ref, x_ref, wo_ref, g1_ref, b1_ref, w1_ref, w2_ref, g2_ref, b2_ref,
                 y_ref, act_ref, *, mix):
    tm = x_ref.shape[0]
    halves = (slice(0, tm // 2), slice(tm // 2, tm))
    mixed = [mix(a0_ref, a1_ref, wo_ref, r) for r in halves]
    xs = [_layer_norm_rows(ALPHA * x_ref[r, :] + h, g1_ref[...], b1_ref[...])
          for r, h in zip(halves, mixed)]
    xb = jnp.concatenate(xs, axis=0).astype(BF16)
    ch = MXU_DIM
    for j in range(D_FF // ch):
        gate = jnp.dot(xb, w1_ref[:, j * ch:(j + 1) * ch], preferred_element_type=F32)
        up = jnp.dot(xb, w1_ref[:, D_FF + j * ch:D_FF + (j + 1) * ch], preferred_element_type=F32)
        act_ref[:, j * ch:(j + 1) * ch] = (gate * jax.nn.sigmoid(gate) * up).astype(BF16)
    down = [jnp.dot(act_ref[r, :], w2_ref[...], preferred_element_type=F32) for r in halves]
    for r, x, h in zip(halves, xs, down):
        y_ref[r, :] = _layer_norm_rows(ALPHA * x + h, g2_ref[...], b2_ref[...])


def _layer_tail(mix, acts, x2, wo_bf, g1, b1, w1_bf, w2_bf, g2, b2, name):
    T = x2.shape[0]
    tm = FFN_TM
    tok = lambda n: pl.BlockSpec((tm, n), lambda i: (i, 0))
    consts = [wo_bf, g1, b1, w1_bf, w2_bf, g2, b2]
    return pl.pallas_call(
        functools.partial(_tail_kernel, mix=mix),
        grid=(T // tm,),
        in_specs=[tok(a.shape[1]) for a in acts] + [tok(D_MODEL)] + [_full_spec(c.shape) for c in consts],
        out_specs=tok(D_MODEL),
        out_shape=jax.ShapeDtypeStruct((T, D_MODEL), F32),
        scratch_shapes=[pltpu.VMEM((tm, D_FF), BF16)],
        compiler_params=_cparams(("parallel",)),
        name=name,
    )(*acts, x2, *consts)


def kernel(x, even_w_in, even_w_out, hgrn_lb_logits, diff_lq1, diff_lk1, diff_lq2, diff_lk2,
           hgrn_norm_g, diff_norm_g, fox_w_in, fox_w_out, fox_b_f, fox_qnorm_g, fox_knorm_g,
           ffn_w1, ffn_w2, ln1_g, ln1_b, ln2_g, ln2_b):
    B, S, D = x.shape
    assert D == D_MODEL and S % ATT_TQ == 0 and S % PROJ_TM == 0 and S % HGRN_L == 0
    T = B * S
    x2 = x.reshape(T, D).astype(F32)

    cmat = jnp.asarray(_hgrn_cumsum_matrix(HGRN_L), BF16)
    tri = jnp.asarray(np.tril(np.ones((PROJ_TM, PROJ_TM), np.float32)), BF16)
    head_of = np.arange(MXU_DIM) // C_HEAD_DIM
    gmat = jnp.asarray((head_of[:, None] == head_of[None, :]).astype(np.float32) / C_HEAD_DIM, BF16)
    row = lambda v: v.astype(F32).reshape(1, -1)

    for l in range(DEPTH):
        j = l // 2
        if l % 2 == 0:
            aq, af, ai, ag, bq, bk, bv = _even_proj(x2, even_w_in[j].astype(BF16), B, S)
            o_a = _hgrn(aq, af, ai, ag, hgrn_lb_logits.astype(F32), row(hgrn_norm_g[j]), cmat, j, B, S)
            lamv = jnp.zeros((8, B_HEAD_DIM), F32).at[0:4].set(
                jnp.stack([diff_lq1[j], diff_lk1[j], diff_lq2[j], diff_lk2[j]]).astype(F32))
            lam_init = 0.8 - 0.6 * math.exp(-0.3 * l)
            o_b = _diff_attention(bq, bk, bv, lamv, row(diff_norm_g[j]), lam_init, B, S)
            mix, acts, w_out, name = _even_mix, [o_a, o_b], even_w_out[j], "even_tail"
        else:
            w_pad = jnp.pad(fox_w_in[j], ((0, 0), (0, LANES - C_HEADS))).astype(BF16)
            bf_pad = jnp.pad(fox_b_f[j].astype(F32), (0, LANES - C_HEADS)).reshape(1, LANES)
            tile4 = lambda v: jnp.tile(v.astype(F32), MXU_DIM // C_HEAD_DIM).reshape(1, MXU_DIM)
            qt, k, vt, g = _odd_proj(x2, w_pad, bf_pad, tile4(fox_qnorm_g[j]),
                                     tile4(fox_knorm_g[j]), gmat, tri, B, S)
            o = _fox_attention(qt, k, vt, B, S)
            mix, acts, w_out, name = _odd_mix, [o, g], fox_w_out[j], "odd_tail"
        x2 = _layer_tail(mix, acts, x2, w_out.astype(BF16), row(ln1_g[l]), row(ln1_b[l]),
                         ffn_w1[l].astype(BF16), ffn_w2[l].astype(BF16), row(ln2_g[l]), row(ln2_b[l]), name)
    return x2.reshape(B, S, D).astype(x.dtype)
```
